```python
import math
import jax, jax.numpy as jnp
from jax import lax
import numpy as np

D_MODEL = 2048
BATCH = 4
SEQ = 2048
DEPTH = 4
DEC_BATCH = 32
DEC_SEQ = 1
PAST_LEN = 16384
PAGE_SIZE = 128

N_A_LAYERS = DEPTH // 2
N_B_LAYERS = DEPTH - N_A_LAYERS
D_POOL = D_MODEL
POOL_WINDOWS = (2, 4, 8, 16)
N_POOL_GROUPS = len(POOL_WINDOWS)
POOL_GROUP = D_POOL // N_POOL_GROUPS
POOL_STATE = max(POOL_WINDOWS) - 1
HEAD_DIM = 64
N_HEADS = D_MODEL // HEAD_DIM
N_KV_HEADS = N_HEADS // 8
GROUP = N_HEADS // N_KV_HEADS
D_ATTN = N_HEADS * HEAD_DIM
KV_DIM = N_KV_HEADS * HEAD_DIM
WINDOW = 128
ROT_DIM = HEAD_DIM // 4
ROPE_THETA = 500000.0
ALPHA = (2 * DEPTH) ** 0.25
BETA = (8 * DEPTH) ** -0.25
LN_EPS = 1e-5
NEG = -1e30

kernel_name = "yoco_pool_swa_sink_decoder_step"


def layer_norm(x, g, b):
    xf = x.astype(jnp.float32)
    mu = jnp.mean(xf, axis=-1, keepdims=True)
    var = jnp.mean(jnp.square(xf - mu), axis=-1, keepdims=True)
    return ((xf - mu) * lax.rsqrt(var + LN_EPS) * g.astype(jnp.float32) + b.astype(jnp.float32)).astype(x.dtype)


def rope_partial(x, pos):
    half = ROT_DIM // 2
    inv_freq = ROPE_THETA ** (-jnp.arange(0, ROT_DIM, 2, dtype=jnp.float32) / ROT_DIM)
    ang = pos.astype(jnp.float32)[:, None] * inv_freq[None, :]
    cos = jnp.cos(ang)[:, None, :]
    sin = jnp.sin(ang)[:, None, :]
    xf = x.astype(jnp.float32)
    x1, x2, rest = xf[..., :half], xf[..., half:ROT_DIM], xf[..., ROT_DIM:]
    out = jnp.concatenate([x1 * cos - x2 * sin, x2 * cos + x1 * sin, rest], axis=-1)
    return out.astype(x.dtype)


def multiscale_pool(u, prefix, pos):
    T = u.shape[1]
    P = POOL_STATE
    ext = jnp.concatenate([prefix.astype(u.dtype), u], axis=1)
    extf = ext.astype(jnp.float32)
    c = jnp.cumsum(extf, axis=1)
    c = jnp.concatenate([jnp.zeros_like(c[:, :1]), c], axis=1)
    hi = c[:, P + 1:P + 1 + T]
    outs = []
    for g, w in enumerate(POOL_WINDOWS):
        sl = slice(g * POOL_GROUP, (g + 1) * POOL_GROUP)
        lo = c[:, P + 1 - w:P + 1 - w + T, sl]
        cnt = jnp.minimum(w, pos + 1).astype(jnp.float32)[None, :, None]
        outs.append((hi[..., sl] - lo) / cnt)
    pooled = jnp.concatenate(outs, axis=-1) - extf[:, P:]
    return pooled.astype(u.dtype), ext[:, -P:]


def pool_layer(x, prefix, pos, w_in, w_grp, scale, w_out, g_ln, b_ln):
    B, T, _ = x.shape
    z = x @ w_in
    u, gate = z[..., :D_POOL], z[..., D_POOL:]
    d, new_state = multiscale_pool(u, prefix, pos)
    d = jnp.einsum('btgc,gce->btge', d.reshape(B, T, N_POOL_GROUPS, POOL_GROUP), w_grp)
    d = d.reshape(B, T, D_POOL) * scale
    y = (d * jax.nn.silu(gate)) @ w_out
    return layer_norm(ALPHA * x + y, g_ln, b_ln), new_state


def shared_kv(h, pos, w_kv):
    B, T, _ = h.shape
    kv = h @ w_kv
    k = kv[..., :KV_DIM].reshape(B, T, N_KV_HEADS, HEAD_DIM)
    v = kv[..., KV_DIM:].reshape(B, T, N_KV_HEADS, HEAD_DIM)
    return rope_partial(k, pos), v


def attn_core(q, k, v, q_pos, k_pos, sinks):
    s = jnp.einsum('...qkgd,...skd->...kgqs', q, k).astype(jnp.float32) * (HEAD_DIM ** -0.5)
    qp = q_pos[..., :, None]
    kp = k_pos[..., None, :]
    mask = (kp <= qp) & (qp - kp < WINDOW) & (kp >= 0)
    s = jnp.where(mask[..., None, None, :, :], s, NEG)
    sink = sinks.astype(jnp.float32).reshape(N_KV_HEADS, GROUP)[:, :, None, None]
    m = jnp.maximum(jnp.max(s, axis=-1, keepdims=True), sink)
    p = jnp.exp(s - m)
    p = p / (jnp.sum(p, axis=-1, keepdims=True) + jnp.exp(sink - m))
    return jnp.einsum('...kgqs,...skd->...qkgd', p.astype(v.dtype), v)


def attn_layer(x, pos, kb, vb, k_pos, q_pos, blocked, w_in, sinks, w_out, g_ln, b_ln):
    B, T, _ = x.shape
    z = x @ w_in
    q = rope_partial(z[..., :D_ATTN].reshape(B, T, N_HEADS, HEAD_DIM), pos)
    gate = z[..., D_ATTN:]
    if blocked:
        nb = T // WINDOW
        q = q.reshape(B, nb, WINDOW, N_KV_HEADS, GROUP, HEAD_DIM)
    else:
        q = q.reshape(B, T, N_KV_HEADS, GROUP, HEAD_DIM)
    o = attn_core(q, kb, vb, q_pos, k_pos, sinks).reshape(B, T, D_ATTN)
    y = (o * jax.nn.silu(gate)) @ w_out
    return layer_norm(ALPHA * x + y, g_ln, b_ln)


def banded_blocks(a):
    B, S = a.shape[:2]
    nb = S // WINDOW
    ax = jnp.concatenate([jnp.zeros_like(a[:, :WINDOW]), a], axis=1)
    ax = ax.reshape(B, nb + 1, WINDOW, *a.shape[2:])
    return jnp.concatenate([ax[:, :-1], ax[:, 1:]], axis=2)


def setup_inputs(seed: int = 0) -> dict:
    key = jax.random.key(seed)
    ks = jax.random.split(key, 20)
    f32 = jnp.float32
    nrm = lambda k, shape, s: jax.random.normal(k, shape, f32) * s
    w_kv = jnp.concatenate([nrm(ks[9], (D_MODEL, KV_DIM), D_MODEL ** -0.5),
                            nrm(ks[10], (D_MODEL, KV_DIM), BETA * D_MODEL ** -0.5)], axis=1)
    return {
        "x_prompt": nrm(ks[0], (BATCH, SEQ, D_MODEL), 1.0),
        "x_sample": nrm(ks[1], (DEC_BATCH, DEC_SEQ, D_MODEL), 1.0),
        "state_pool": nrm(ks[2], (N_A_LAYERS, DEC_BATCH, POOL_STATE, D_POOL), 1.0),
        "cache_k": nrm(ks[3], (DEC_BATCH, WINDOW, N_KV_HEADS, HEAD_DIM), 1.0),
        "cache_v": nrm(ks[4], (DEC_BATCH, WINDOW, N_KV_HEADS, HEAD_DIM), BETA),
        "w_in_a": nrm(ks[5], (N_A_LAYERS, D_MODEL, 2 * D_POOL), D_MODEL ** -0.5),
        "w_grp_a": nrm(ks[6], (N_A_LAYERS, N_POOL_GROUPS, POOL_GROUP, POOL_GROUP), POOL_GROUP ** -0.5),
        "scale_a": 1.0 + nrm(ks[7], (N_A_LAYERS, D_POOL), 0.1),
        "w_out_a": nrm(ks[8], (N_A_LAYERS, D_POOL, D_MODEL), BETA * D_POOL ** -0.5),
        "w_kv": w_kv,
        "w_in_b": nrm(ks[11], (N_B_LAYERS, D_MODEL, 2 * D_ATTN), D_MODEL ** -0.5),
        "sinks_b": nrm(ks[12], (N_B_LAYERS, N_HEADS), 1.0),
        "w_out_b": nrm(ks[13], (N_B_LAYERS, D_ATTN, D_MODEL), BETA * D_ATTN ** -0.5),
        "ln_g": 1.0 + nrm(ks[14], (DEPTH, D_MODEL), 0.1),
        "ln_b": nrm(ks[15], (DEPTH, D_MODEL), 0.1),
    }


def reference(x_prompt, x_sample, state_pool, cache_k, cache_v, w_in_a, w_grp_a, scale_a, w_out_a,
              w_kv, w_in_b, sinks_b, w_out_b, ln_g, ln_b):
    S = x_prompt.shape[1]
    T = x_sample.shape[1]
    pos_p = jnp.arange(S, dtype=jnp.int32)
    pos_s = PAST_LEN + jnp.arange(T, dtype=jnp.int32)
    xp, xs = x_prompt, x_sample
    prefix_p = jnp.zeros((xp.shape[0], POOL_STATE, D_POOL), xp.dtype)
    pool_p, pool_s = [], []
    for i in range(DEPTH):
        if i < N_A_LAYERS:
            xp, sp = pool_layer(xp, prefix_p, pos_p, w_in_a[i], w_grp_a[i], scale_a[i], w_out_a[i], ln_g[i], ln_b[i])
            xs, ss = pool_layer(xs, state_pool[i], pos_s, w_in_a[i], w_grp_a[i], scale_a[i], w_out_a[i], ln_g[i], ln_b[i])
            pool_p.append(sp)
            pool_s.append(ss)
            if i == N_A_LAYERS - 1:
                k_p, v_p = shared_kv(xp, pos_p, w_kv)
                k_s, v_s = shared_kv(xs, pos_s, w_kv)
                kb_p, vb_p = banded_blocks(k_p), banded_blocks(v_p)
                kpos_x = jnp.arange(-WINDOW, S, dtype=jnp.int32).reshape(S // WINDOW + 1, WINDOW)
                kpos_p = jnp.concatenate([kpos_x[:-1], kpos_x[1:]], axis=1)
                qpos_p = pos_p.reshape(S // WINDOW, WINDOW)
                kx_s = jnp.concatenate([cache_k.astype(k_s.dtype), k_s], axis=1)
                vx_s = jnp.concatenate([cache_v.astype(v_s.dtype), v_s], axis=1)
                kpos_s = PAST_LEN - WINDOW + jnp.arange(WINDOW + T, dtype=jnp.int32)
        else:
            j = i - N_A_LAYERS
            xp = attn_layer(xp, pos_p, kb_p, vb_p, kpos_p, qpos_p, True,
                            w_in_b[j], sinks_b[j], w_out_b[j], ln_g[i], ln_b[i])
            xs = attn_layer(xs, pos_s, kx_s, vx_s, kpos_s, pos_s, False,
                            w_in_b[j], sinks_b[j], w_out_b[j], ln_g[i], ln_b[i])
    new_pool_prompt = jnp.stack(pool_p, axis=0)
    new_pool_sample = jnp.stack(pool_s, axis=0)
    new_k_prompt = k_p[:, -WINDOW:]
    new_v_prompt = v_p[:, -WINDOW:]
    new_k_sample = kx_s[:, -WINDOW:]
    new_v_sample = vx_s[:, -WINDOW:]
    return (xp, xs, new_pool_prompt, new_pool_sample, new_k_prompt, new_v_prompt, new_k_sample, new_v_sample)
```

```python
import functools

import jax
import jax.numpy as jnp
from jax import lax
from jax.experimental import pallas as pl
from jax.experimental.pallas import tpu as pltpu

F32 = jnp.float32
BF16 = jnp.bfloat16

D_MODEL = 2048
DEPTH = 4
PAST_LEN = 16384
N_A_LAYERS = DEPTH // 2
POOL_WINDOWS = (2, 4, 8, 16)
POOL_GROUP = D_MODEL // len(POOL_WINDOWS)
POOL_STATE = max(POOL_WINDOWS) - 1
HEAD_DIM = 64
N_HEADS = D_MODEL // HEAD_DIM
N_KV_HEADS = N_HEADS // 8
GROUP = N_HEADS // N_KV_HEADS
KV_DIM = N_KV_HEADS * HEAD_DIM
WINDOW = 128
ROT_DIM = HEAD_DIM // 4
ROPE_THETA = 500000.0
ALPHA = (2 * DEPTH) ** 0.25
LN_EPS = 1e-5
NEG = -1e30
SM_SCALE = HEAD_DIM ** -0.5

LANES = 128
HALO = 16
COL_CHUNK = 512
N_CHUNKS = D_MODEL // COL_CHUNK
TM = 256
TK = 512
LN_ROWS = 32
SAMPLE_ATTN_BATCH = 8
VMEM_LIMIT_BYTES = 56 * 1024 * 1024


def _params(n_axes):
    return pltpu.CompilerParams(dimension_semantics=("arbitrary",) * n_axes,
                                vmem_limit_bytes=VMEM_LIMIT_BYTES)


def _resident(shape):
    zeros = (0,) * len(shape)
    return pl.BlockSpec(shape, lambda *_: zeros, pipeline_mode=pl.Buffered(1))


def _silu(g):
    return g / (1.0 + jnp.exp(-g))


def _rope(x, cos, sa, sb):
    return x * cos + pltpu.roll(x, LANES - ROT_DIM // 2, 1) * sa + pltpu.roll(x, ROT_DIM // 2, 1) * sb


def _outproj_ln(h_ref, x_rows, wout_ref, g_ref, b_ref, o_rows_set, o_rows_get, rows):
    for n in range(N_CHUNKS):
        cols = slice(n * COL_CHUNK, (n + 1) * COL_CHUNK)
        y = jnp.dot(h_ref[...], wout_ref[:, cols], preferred_element_type=F32)
        o_rows_set(slice(0, rows), cols, ALPHA * x_rows(slice(0, rows), cols) + y)
    step = min(LN_ROWS, rows)
    for r0 in range(0, rows, step):
        rs = slice(r0, r0 + step)
        r = o_rows_get(rs, slice(None))
        mu = jnp.mean(r, axis=-1, keepdims=True)
        c = r - mu
        var = jnp.mean(c * c, axis=-1, keepdims=True)
        o_rows_set(rs, slice(None), c * lax.rsqrt(var + LN_EPS) * g_ref[...] + b_ref[...])


def _pool_prompt_kernel(x_ref, win_ref, wgrp_ref, scale_ref, wout_ref, g_ref, b_ref,
                        o_ref, state_ref, u_buf, h_buf):
    t = pl.program_id(1)

    @pl.when(t == 0)
    def _():
        u_buf[0:HALO, :] = jnp.zeros((HALO, D_MODEL), F32)

    xb = x_ref[0].astype(BF16)
    row = lax.broadcasted_iota(jnp.int32, (TM, 1), 0) + t * TM
    for g, w in enumerate(POOL_WINDOWS):
        cols = slice(g * POOL_GROUP, (g + 1) * POOL_GROUP)
        u_buf[HALO:, cols] = jnp.dot(xb, win_ref[:, cols], preferred_element_type=F32)
        ext = u_buf[:, cols]
        s = ext
        shift = 1
        while shift < w:
            s = s + pltpu.roll(s, shift, 0)
            shift *= 2
        inv_cnt = 1.0 / jnp.minimum(w, row + 1).astype(F32)
        d = s[HALO:, :] * inv_cnt - ext[HALO:, :]
        d = jnp.dot(d.astype(BF16), wgrp_ref[g], preferred_element_type=F32) * scale_ref[:, cols]
        gate = jnp.dot(xb, win_ref[:, D_MODEL + g * POOL_GROUP:D_MODEL + (g + 1) * POOL_GROUP],
                       preferred_element_type=F32)
        h_buf[:, cols] = (d * _silu(gate)).astype(BF16)

    @pl.when(t == pl.num_programs(1) - 1)
    def _():
        state_ref[0] = u_buf[TM:TM + HALO, :]

    u_buf[0:HALO, :] = u_buf[TM:TM + HALO, :]

    def set_rows(rs, cs, v):
        o_ref[0, rs, cs] = v

    _outproj_ln(h_buf, lambda rs, cs: x_ref[0, rs, cs], wout_ref, g_ref, b_ref,
                set_rows, lambda rs, cs: o_ref[0, rs, cs], TM)


def _pool_prompt(x, win, wgrp, scale, wout, g, b):
    B, S, D = x.shape
    return pl.pallas_call(
        _pool_prompt_kernel,
        grid=(B, S // TM),
        in_specs=[
            pl.BlockSpec((1, TM, D), lambda bi, t: (bi, t, 0)),
            _resident(win.shape), _resident(wgrp.shape), _resident(scale.shape),
            _resident(wout.shape), _resident(g.shape), _resident(b.shape),
        ],
        out_specs=[
            pl.BlockSpec((1, TM, D), lambda bi, t: (bi, t, 0)),
            pl.BlockSpec((1, HALO, D), lambda bi, t: (bi, 0, 0)),
        ],
        out_shape=[jax.ShapeDtypeStruct((B, S, D), F32),
                   jax.ShapeDtypeStruct((B, HALO, D), F32)],
        scratch_shapes=[pltpu.VMEM((HALO + TM, D), F32), pltpu.VMEM((TM, D), BF16)],
        compiler_params=_params(2),
        name="pool_prompt",
    )(x, win, wgrp, scale, wout, g, b)


def _pool_sample_kernel(x_ref, st_ref, win_ref, wgrp_ref, scale_ref, wout_ref, g_ref, b_ref,
                        o_ref, nst_ref, h_buf):
    rows = x_ref.shape[0]
    xb = x_ref[...].astype(BF16)
    for g, w in enumerate(POOL_WINDOWS):
        cols = slice(g * POOL_GROUP, (g + 1) * POOL_GROUP)
        u = jnp.dot(xb, win_ref[:, cols], preferred_element_type=F32)
        acc = u
        for j in range(1, w):
            acc = acc + st_ref[POOL_STATE - j, :, cols]
        d = acc * (1.0 / min(w, PAST_LEN + 1)) - u
        d = jnp.dot(d.astype(BF16), wgrp_ref[g], preferred_element_type=F32) * scale_ref[:, cols]
        gate = jnp.dot(xb, win_ref[:, D_MODEL + g * POOL_GROUP:D_MODEL + (g + 1) * POOL_GROUP],
                       preferred_element_type=F32)
        h_buf[:, cols] = (d * _silu(gate)).astype(BF16)
        for j in range(POOL_STATE - 1):
            nst_ref[j, :, cols] = st_ref[j + 1, :, cols]
        nst_ref[POOL_STATE - 1, :, cols] = u

    def set_rows(rs, cs, v):
        o_ref[rs, cs] = v

    _outproj_ln(h_buf, lambda rs, cs: x_ref[rs, cs], wout_ref, g_ref, b_ref,
                set_rows, lambda rs, cs: o_ref[rs, cs], rows)


def _pool_sample(x, st, win, wgrp, scale, wout, g, b):
    R, D = x.shape
    return pl.pallas_call(
        _pool_sample_kernel,
        grid=(1,),
        in_specs=[_resident(a.shape) for a in (x, st, win, wgrp, scale, wout, g, b)],
        out_specs=[pl.BlockSpec((R, D), lambda i: (0, 0)),
                   pl.BlockSpec(st.shape, lambda i: (0, 0, 0))],
        out_shape=[jax.ShapeDtypeStruct((R, D), F32), jax.ShapeDtypeStruct(st.shape, F32)],
        scratch_shapes=[pltpu.VMEM((R, D), BF16)],
        compiler_params=_params(1),
        name="pool_sample",
    )(x, st, win, wgrp, scale, wout, g, b)


def _kv_kernel(x_ref, cos_ref, sa_ref, sb_ref, wkv_ref, k_ref, v_ref, kb_ref, vb_ref):
    xb = x_ref[...].astype(BF16)
    kv = jnp.dot(xb, wkv_ref[...], preferred_element_type=F32)
    for j in range(KV_DIM // LANES):
        cs = slice(j * LANES, (j + 1) * LANES)
        k = _rope(kv[:, cs], cos_ref[...], sa_ref[...], sb_ref[...])
        k_ref[:, cs] = k
        kb_ref[:, cs] = k.astype(BF16)
    v = kv[:, KV_DIM:]
    v_ref[...] = v
    vb_ref[...] = v.astype(BF16)


def _kv_proj(x, tables, wkv, tile, table_tiles):
    M, D = x.shape
    tab = pl.BlockSpec((tile, LANES), lambda i: (i % table_tiles, 0))
    out = pl.BlockSpec((tile, KV_DIM), lambda i: (i, 0))
    return pl.pallas_call(
        _kv_kernel,
        grid=(M // tile,),
        in_specs=[pl.BlockSpec((tile, D), lambda i: (i, 0)), tab, tab, tab, _resident(wkv.shape)],
        out_specs=[out, out, out, out],
        out_shape=[jax.ShapeDtypeStruct((M, KV_DIM), F32), jax.ShapeDtypeStruct((M, KV_DIM), F32),
                   jax.ShapeDtypeStruct((M, KV_DIM), BF16), jax.ShapeDtypeStruct((M, KV_DIM), BF16)],
        compiler_params=_params(1),
        name="kv_proj",
    )(x, *tables, wkv)


def _attn_prompt_kernel(x_ref, cos_ref, sa_ref, sb_ref, k_ref, v_ref, win_ref, sink_ref, wout_ref,
                        g_ref, b_ref, o_ref, q_buf, qx_buf, a_buf, h_buf):
    t = pl.program_id(1)
    xb = x_ref[0].astype(BF16)
    for n in range(N_CHUNKS):
        q = jnp.dot(xb, win_ref[:, n * COL_CHUNK:(n + 1) * COL_CHUNK], preferred_element_type=F32)
        for j in range(COL_CHUNK // LANES):
            qj = _rope(q[:, j * LANES:(j + 1) * LANES], cos_ref[...], sa_ref[...], sb_ref[...])
            c0 = n * COL_CHUNK + j * LANES
            q_buf[:, c0:c0 + LANES] = qj.astype(BF16)
            qx_buf[:, c0:c0 + LANES] = pltpu.roll(qj, HEAD_DIM, 1).astype(BF16)

    r = lax.broadcasted_iota(jnp.int32, (WINDOW, 2 * WINDOW), 0)
    c = lax.broadcasted_iota(jnp.int32, (WINDOW, 2 * WINDOW), 1)
    band = (c > r) & (c <= r + WINDOW)
    k_low = lax.broadcasted_iota(jnp.int32, (2 * WINDOW, LANES), 1) < HEAD_DIM
    o_low = lax.broadcasted_iota(jnp.int32, (WINDOW, LANES), 1) < HEAD_DIM
    for qb in range(TM // WINDOW):
        blk = t * (TM // WINDOW) + qb
        prev = pl.multiple_of(jnp.maximum(blk - 1, 0) * WINDOW, WINDOW)
        cur = pl.multiple_of(blk * WINDOW, WINDOW)
        vis = band & ((c >= WINDOW) | (blk > 0))
        rows = slice(qb * WINDOW, (qb + 1) * WINDOW)
        for kv in range(N_KV_HEADS):
            ks = slice((kv // 2) * LANES, (kv // 2 + 1) * LANES)
            kv_low = kv % 2 == 0
            kslab = jnp.concatenate([k_ref[0, pl.ds(prev, WINDOW), ks], k_ref[0, pl.ds(cur, WINDOW), ks]], axis=0)
            vslab = jnp.concatenate([v_ref[0, pl.ds(prev, WINDOW), ks], v_ref[0, pl.ds(cur, WINDOW), ks]], axis=0)
            kslab = jnp.where(k_low == kv_low, kslab, jnp.zeros_like(kslab))
            for pair in range(GROUP // 2):
                slab = kv * (GROUP // 2) + pair
                cs = slice(slab * LANES, (slab + 1) * LANES)
                halves = []
                for par in range(2):
                    h = 2 * slab + par
                    q_src = q_buf if (par == 0) == kv_low else qx_buf
                    s = lax.dot_general(q_src[rows, cs], kslab, (((1,), (1,)), ((), ())),
                                        preferred_element_type=F32) * SM_SCALE
                    s = jnp.where(vis, s, NEG)
                    sink = sink_ref[h]
                    m = jnp.maximum(jnp.max(s, axis=-1, keepdims=True), sink)
                    p = jnp.exp(s - m)
                    denom = jnp.sum(p, axis=-1, keepdims=True) + jnp.exp(sink - m)
                    o = jnp.dot(p.astype(BF16), vslab, preferred_element_type=F32) / denom
                    halves.append(o if (par == 0) == kv_low else pltpu.roll(o, HEAD_DIM, 1))
                a_buf[rows, cs] = jnp.where(o_low, halves[0], halves[1])

    for n in range(N_CHUNKS):
        cols = slice(n * COL_CHUNK, (n + 1) * COL_CHUNK)
        gate = jnp.dot(xb, win_ref[:, D_MODEL + n * COL_CHUNK:D_MODEL + (n + 1) * COL_CHUNK],
                       preferred_element_type=F32)
        h_buf[:, cols] = (a_buf[:, cols] * _silu(gate)).astype(BF16)

    def set_rows(rs, cs, v):
        o_ref[0, rs, cs] = v

    _outproj_ln(h_buf, lambda rs, cs: x_ref[0, rs, cs], wout_ref, g_ref, b_ref,
                set_rows, lambda rs, cs: o_ref[0, rs, cs], TM)


def _attn_prompt(x, tables, kb, vb, win, sinks, wout, g, b):
    B, S, D = x.shape
    tab = pl.BlockSpec((TM, LANES), lambda bi, t: (t, 0))
    kvspec = pl.BlockSpec((1, S, KV_DIM), lambda bi, t: (bi, 0, 0))
    return pl.pallas_call(
        _attn_prompt_kernel,
        grid=(B, S // TM),
        in_specs=[
            pl.BlockSpec((1, TM, D), lambda bi, t: (bi, t, 0)), tab, tab, tab, kvspec, kvspec,
            _resident(win.shape),
            pl.BlockSpec(memory_space=pltpu.SMEM),
            _resident(wout.shape), _resident(g.shape), _resident(b.shape),
        ],
        out_specs=pl.BlockSpec((1, TM, D), lambda bi, t: (bi, t, 0)),
        out_shape=jax.ShapeDtypeStruct((B, S, D), F32),
        scratch_shapes=[pltpu.VMEM((TM, D), BF16), pltpu.VMEM((TM, D), BF16),
                        pltpu.VMEM((TM, D), F32), pltpu.VMEM((TM, D), BF16)],
        compiler_params=_params(2),
        name="attn_prompt",
    )(x, *tables, kb, vb, win, sinks, wout, g, b)


def _qgate_sample_kernel(x_ref, cos_ref, sa_ref, sb_ref, win_ref, q_ref, gate_ref):
    xb = x_ref[...].astype(BF16)
    for n in range(N_CHUNKS):
        q = jnp.dot(xb, win_ref[:, n * COL_CHUNK:(n + 1) * COL_CHUNK], preferred_element_type=F32)
        for j in range(COL_CHUNK // LANES):
            c0 = n * COL_CHUNK + j * LANES
            q_ref[:, c0:c0 + LANES] = _rope(q[:, j * LANES:(j + 1) * LANES],
                                            cos_ref[...], sa_ref[...], sb_ref[...])
        gate_ref[:, n * COL_CHUNK:(n + 1) * COL_CHUNK] = jnp.dot(
            xb, win_ref[:, D_MODEL + n * COL_CHUNK:D_MODEL + (n + 1) * COL_CHUNK],
            preferred_element_type=F32)


def _qgate_sample(x, tables, win):
    R, D = x.shape
    return pl.pallas_call(
        _qgate_sample_kernel,
        grid=(1,),
        in_specs=[_resident(x.shape)] + [_resident(t.shape) for t in tables] + [_resident(win.shape)],
        out_specs=[pl.BlockSpec((R, D), lambda i: (0, 0)), pl.BlockSpec((R, D), lambda i: (0, 0))],
        out_shape=[jax.ShapeDtypeStruct((R, D), F32), jax.ShapeDtypeStruct((R, D), F32)],
        compiler_params=_params(1),
        name="qgate_sample",
    )(x, *tables, win)


def _attn_sample_kernel(q_ref, kn_ref, vn_ref, ck_ref, cv_ref, sink_ref, o_ref, nk_ref, nv_ref):
    nb = q_ref.shape[0]
    head_of_lane = lax.broadcasted_iota(jnp.int32, (N_HEADS, D_MODEL), 1) // HEAD_DIM
    own_head = head_of_lane == lax.broadcasted_iota(jnp.int32, (N_HEADS, D_MODEL), 0)
    low_half = lax.broadcasted_iota(jnp.int32, (N_HEADS, LANES), 1) < HEAD_DIM
    last_row = lax.broadcasted_iota(jnp.int32, (WINDOW, KV_DIM), 0) == WINDOW - 1
    sink = sink_ref[...]
    heads_per_slab = LANES // HEAD_DIM
    slabs_per_group = GROUP // heads_per_slab

    def body(i, carry):
        newk = jnp.where(last_row, kn_ref[pl.ds(i, 1), :], pltpu.roll(ck_ref[i], WINDOW - 1, 0))
        newv = jnp.where(last_row, vn_ref[pl.ds(i, 1), :], pltpu.roll(cv_ref[i], WINDOW - 1, 0))
        nk_ref[i] = newk
        nv_ref[i] = newv
        qh = jnp.where(own_head, jnp.broadcast_to(q_ref[pl.ds(i, 1), :], (N_HEADS, D_MODEL)), 0.0)
        folded = []
        for kv in range(N_KV_HEADS):
            w = qh[:, kv * GROUP * HEAD_DIM:kv * GROUP * HEAD_DIM + LANES]
            for sl in range(1, slabs_per_group):
                c0 = kv * GROUP * HEAD_DIM + sl * LANES
                w = w + qh[:, c0:c0 + LANES]
            folded.append(w + pltpu.roll(w, HEAD_DIM, 1))
        qg = jnp.concatenate([jnp.where(low_half, folded[2 * j], folded[2 * j + 1])
                              for j in range(N_KV_HEADS // 2)], axis=1)
        s = lax.dot_general(qg.astype(BF16), newk.astype(BF16), (((1,), (1,)), ((), ())),
                            preferred_element_type=F32) * SM_SCALE
        m = jnp.maximum(jnp.max(s, axis=-1, keepdims=True), sink)
        p = jnp.exp(s - m)
        denom = jnp.sum(p, axis=-1, keepdims=True) + jnp.exp(sink - m)
        og = jnp.dot(p.astype(BF16), newv.astype(BF16), preferred_element_type=F32) / denom
        slabs = []
        for kv in range(N_KV_HEADS):
            xs = og[:, (kv // 2) * LANES:(kv // 2 + 1) * LANES]
            rolled = pltpu.roll(xs, HEAD_DIM, 1)
            both = jnp.where(low_half, xs, rolled) if kv % 2 == 0 else jnp.where(low_half, rolled, xs)
            slabs.extend([both] * slabs_per_group)
        full = jnp.concatenate(slabs, axis=1)
        o_ref[pl.ds(i, 1), :] = jnp.sum(jnp.where(own_head, full, 0.0), axis=0, keepdims=True)
        return carry

    lax.fori_loop(0, nb, body, 0)


def _attn_sample(q, kn, vn, ck, cv, sinks_col):
    R, D = q.shape
    nb = SAMPLE_ATTN_BATCH
    row2 = lambda w: pl.BlockSpec((nb, w), lambda i: (i, 0))
    cache = pl.BlockSpec((nb, WINDOW, KV_DIM), lambda i: (i, 0, 0))
    return pl.pallas_call(
        _attn_sample_kernel,
        grid=(R // nb,),
        in_specs=[row2(D), row2(KV_DIM), row2(KV_DIM), cache, cache, _resident(sinks_col.shape)],
        out_specs=[row2(D), cache, cache],
        out_shape=[jax.ShapeDtypeStruct((R, D), F32),
                   jax.ShapeDtypeStruct(ck.shape, F32), jax.ShapeDtypeStruct(cv.shape, F32)],
        compiler_params=_params(1),
        name="attn_sample",
    )(q, kn, vn, ck, cv, sinks_col)


def _gated_out_sample_kernel(x_ref, a_ref, gate_ref, wout_ref, g_ref, b_ref, o_ref, h_buf):
    rows = x_ref.shape[0]
    h_buf[...] = (a_ref[...] * _silu(gate_ref[...])).astype(BF16)

    def set_rows(rs, cs, v):
        o_ref[rs, cs] = v

    _outproj_ln(h_buf, lambda rs, cs: x_ref[rs, cs], wout_ref, g_ref, b_ref,
                set_rows, lambda rs, cs: o_ref[rs, cs], rows)


def _gated_out_sample(x, a, gate, wout, g, b):
    R, D = x.shape
    return pl.pallas_call(
        _gated_out_sample_kernel,
        grid=(1,),
        in_specs=[_resident(t.shape) for t in (x, a, gate, wout, g, b)],
        out_specs=pl.BlockSpec((R, D), lambda i: (0, 0)),
        out_shape=jax.ShapeDtypeStruct((R, D), F32),
        scratch_shapes=[pltpu.VMEM((R, D), BF16)],
        compiler_params=_params(1),
        name="gated_out_sample",
    )(x, a, gate, wout, g, b)


def _rope_tables(pos):
    half = ROT_DIM // 2
    inv_freq = ROPE_THETA ** (-jnp.arange(0, ROT_DIM, 2, dtype=F32) / ROT_DIM)
    ang = pos.astype(F32)[:, None] * inv_freq[None, :]
    cos, sin = jnp.cos(ang), jnp.sin(ang)
    n = pos.shape[0]
    rest = jnp.zeros((n, HEAD_DIM - ROT_DIM), F32)
    zero = jnp.zeros((n, half), F32)
    cos_h = jnp.concatenate([cos, cos, rest + 1.0], axis=1)
    sa_h = jnp.concatenate([-sin, zero, rest], axis=1)
    sb_h = jnp.concatenate([zero, sin, rest], axis=1)
    rep = LANES // HEAD_DIM
    return tuple(jnp.tile(a, (1, rep)) for a in (cos_h, sa_h, sb_h))


def kernel(x_prompt, x_sample, state_pool, cache_k, cache_v, w_in_a, w_grp_a, scale_a, w_out_a,
           w_kv, w_in_b, sinks_b, w_out_b, ln_g, ln_b):
    B, S, D = x_prompt.shape
    R = x_sample.shape[0]
    xp = x_prompt
    xs = x_sample.reshape(R, D)
    tab_p = _rope_tables(jnp.arange(S, dtype=jnp.int32))
    tab_s = _rope_tables(jnp.full((R,), PAST_LEN, jnp.int32))
    pool_p, pool_s = [], []
    for i in range(N_A_LAYERS):
        win, wgrp, wout = w_in_a[i].astype(BF16), w_grp_a[i].astype(BF16), w_out_a[i].astype(BF16)
        scale, g, b = scale_a[i][None, :], ln_g[i][None, :], ln_b[i][None, :]
        xp, sp = _pool_prompt(xp, win, wgrp, scale, wout, g, b)
        xs, ss = _pool_sample(xs, state_pool[i].transpose(1, 0, 2), win, wgrp, scale, wout, g, b)
        pool_p.append(sp[:, HALO - POOL_STATE:])
        pool_s.append(ss.transpose(1, 0, 2))
    wkv = w_kv.astype(BF16)
    k_p, v_p, kb_p, vb_p = _kv_proj(xp.reshape(B * S, D), tab_p, wkv, TK, S // TK)
    k_s, v_s, _, _ = _kv_proj(xs, tab_s, wkv, R, 1)
    kb_p, vb_p = kb_p.reshape(B, S, KV_DIM), vb_p.reshape(B, S, KV_DIM)
    ck = cache_k.reshape(R, WINDOW, KV_DIM)
    cv = cache_v.reshape(R, WINDOW, KV_DIM)
    for j in range(DEPTH - N_A_LAYERS):
        i = N_A_LAYERS + j
        win, wout = w_in_b[j].astype(BF16), w_out_b[j].astype(BF16)
        g, b = ln_g[i][None, :], ln_b[i][None, :]
        xp = _attn_prompt(xp, tab_p, kb_p, vb_p, win, sinks_b[j], wout, g, b)
        q_s, gate_s = _qgate_sample(xs, tab_s, win)
        a_s, nk, nv = _attn_sample(q_s, k_s, v_s, ck, cv, sinks_b[j][:, None])
        xs = _gated_out_sample(xs, a_s, gate_s, wout, g, b)
    kv4 = (N_KV_HEADS, HEAD_DIM)
    new_k_p = k_p.reshape(B, S, KV_DIM)[:, S - WINDOW:].reshape(B, WINDOW, *kv4)
    new_v_p = v_p.reshape(B, S, KV_DIM)[:, S - WINDOW:].reshape(B, WINDOW, *kv4)
    return (xp, xs.reshape(R, 1, D), jnp.stack(pool_p, axis=0), jnp.stack(pool_s, axis=0),
            new_k_p, new_v_p, nk.reshape(R, WINDOW, *kv4), nv.reshape(R, WINDOW, *kv4))
```

```python
import functools

import jax
import jax.numpy as jnp
from jax import lax
from jax.experimental import pallas as pl
from jax.experimental.pallas import tpu as pltpu

F32 = jnp.float32
BF16 = jnp.bfloat16

D_MODEL = 2048
DEPTH = 4
PAST_LEN = 16384
N_A_LAYERS = DEPTH // 2
POOL_WINDOWS = (2, 4, 8, 16)
POOL_GROUP = D_MODEL // len(POOL_WINDOWS)
POOL_STATE = max(POOL_WINDOWS) - 1
HEAD_DIM = 64
N_HEADS = D_MODEL // HEAD_DIM
N_KV_HEADS = N_HEADS // 8
GROUP = N_HEADS // N_KV_HEADS
KV_DIM = N_KV_HEADS * HEAD_DIM
WINDOW = 128
ROT_DIM = HEAD_DIM // 4
ROPE_THETA = 500000.0
ALPHA = (2 * DEPTH) ** 0.25
LN_EPS = 1e-5
NEG = -1e30
SM_SCALE = HEAD_DIM ** -0.5

LANES = 128
HALO = 16
COL_CHUNK = 512
N_CHUNKS = D_MODEL // COL_CHUNK
TM = 256
TK = 512
LN_ROWS = 32
SAMPLE_ATTN_BATCH = 8
VMEM_LIMIT_BYTES = 56 * 1024 * 1024


def _params(n_axes):
    return pltpu.CompilerParams(dimension_semantics=("arbitrary",) * n_axes,
                                vmem_limit_bytes=VMEM_LIMIT_BYTES)


def _resident(shape):
    zeros = (0,) * len(shape)
    return pl.BlockSpec(shape, lambda *_: zeros, pipeline_mode=pl.Buffered(1))


def _silu(g):
    return g / (1.0 + jnp.exp(-g))


def _rope(x, cos, sa, sb):
    return x * cos + pltpu.roll(x, LANES - ROT_DIM // 2, 1) * sa + pltpu.roll(x, ROT_DIM // 2, 1) * sb


def _outproj_ln(h_ref, x_rows, wout_ref, g_ref, b_ref, o_rows_set, o_rows_get, rows):
    for n in range(N_CHUNKS):
        cols = slice(n * COL_CHUNK, (n + 1) * COL_CHUNK)
        y = jnp.dot(h_ref[...], wout_ref[:, cols], preferred_element_type=F32)
        o_rows_set(slice(0, rows), cols, ALPHA * x_rows(slice(0, rows), cols) + y)
    step = min(LN_ROWS, rows)
    for r0 in range(0, rows, step):
        rs = slice(r0, r0 + step)
        r = o_rows_get(rs, slice(None))
        mu = jnp.mean(r, axis=-1, keepdims=True)
        c = r - mu
        var = jnp.mean(c * c, axis=-1, keepdims=True)
        o_rows_set(rs, slice(None), c * lax.rsqrt(var + LN_EPS) * g_ref[...] + b_ref[...])


def _pool_prompt_kernel(x_ref, win_ref, wgrp_ref, scale_ref, wout_ref, g_ref, b_ref,
                        o_ref, state_ref, u_buf, h_buf):
    t = pl.program_id(1)

    @pl.when(t == 0)
    def _():
        u_buf[0:HALO, :] = jnp.zeros((HALO, D_MODEL), F32)

    xb = x_ref[0].astype(BF16)
    row = lax.broadcasted_iota(jnp.int32, (TM, 1), 0) + t * TM
    for g, w in enumerate(POOL_WINDOWS):
        cols = slice(g * POOL_GROUP, (g + 1) * POOL_GROUP)
        u_buf[HALO:, cols] = jnp.dot(xb, win_ref[:, cols], preferred_element_type=F32)
        ext = u_buf[:, cols]
        s = ext
        shift = 1
        while shift < w:
            s = s + pltpu.roll(s, shift, 0)
            shift *= 2
        inv_cnt = 1.0 / jnp.minimum(w, row + 1).astype(F32)
        d = s[HALO:, :] * inv_cnt - ext[HALO:, :]
        d = jnp.dot(d.astype(BF16), wgrp_ref[g], preferred_element_type=F32) * scale_ref[:, cols]
        gate = jnp.dot(xb, win_ref[:, D_MODEL + g * POOL_GROUP:D_MODEL + (g + 1) * POOL_GROUP],
                       preferred_element_type=F32)
        h_buf[:, cols] = (d * _silu(gate)).astype(BF16)

    @pl.when(t == pl.num_programs(1) - 1)
    def _():
        state_ref[0] = u_buf[TM:TM + HALO, :]

    u_buf[0:HALO, :] = u_buf[TM:TM + HALO, :]

    def set_rows(rs, cs, v):
        o_ref[0, rs, cs] = v

    _outproj_ln(h_buf, lambda rs, cs: x_ref[0, rs, cs], wout_ref, g_ref, b_ref,
                set_rows, lambda rs, cs: o_ref[0, rs, cs], TM)


def _pool_prompt(x, win, wgrp, scale, wout, g, b):
    B, S, D = x.shape
    return pl.pallas_call(
        _pool_prompt_kernel,
        grid=(B, S // TM),
        in_specs=[
            pl.BlockSpec((1, TM, D), lambda bi, t: (bi, t, 0)),
            _resident(win.shape), _resident(wgrp.shape), _resident(scale.shape),
            _resident(wout.shape), _resident(g.shape), _resident(b.shape),
        ],
        out_specs=[
            pl.BlockSpec((1, TM, D), lambda bi, t: (bi, t, 0)),
            pl.BlockSpec((1, HALO, D), lambda bi, t: (bi, 0, 0)),
        ],
        out_shape=[jax.ShapeDtypeStruct((B, S, D), F32),
                   jax.ShapeDtypeStruct((B, HALO, D), F32)],
        scratch_shapes=[pltpu.VMEM((HALO + TM, D), F32), pltpu.VMEM((TM, D), BF16)],
        compiler_params=_params(2),
        name="pool_prompt",
    )(x, win, wgrp, scale, wout, g, b)


def _pool_sample_kernel(x_ref, st_ref, win_ref, wgrp_ref, scale_ref, wout_ref, g_ref, b_ref,
                        o_ref, nst_ref, h_buf):
    rows = x_ref.shape[0]
    xb = x_ref[...].astype(BF16)
    for g, w in enumerate(POOL_WINDOWS):
        cols = slice(g * POOL_GROUP, (g + 1) * POOL_GROUP)
        u = jnp.dot(xb, win_ref[:, cols], preferred_element_type=F32)
        acc = u
        for j in range(1, w):
            acc = acc + st_ref[POOL_STATE - j, :, cols]
        d = acc * (1.0 / min(w, PAST_LEN + 1)) - u
        d = jnp.dot(d.astype(BF16), wgrp_ref[g], preferred_element_type=F32) * scale_ref[:, cols]
        gate = jnp.dot(xb, win_ref[:, D_MODEL + g * POOL_GROUP:D_MODEL + (g + 1) * POOL_GROUP],
                       preferred_element_type=F32)
        h_buf[:, cols] = (d * _silu(gate)).astype(BF16)
        for j in range(POOL_STATE - 1):
            nst_ref[j, :, cols] = st_ref[j + 1, :, cols]
        nst_ref[POOL_STATE - 1, :, cols] = u

    def set_rows(rs, cs, v):
        o_ref[rs, cs] = v

    _outproj_ln(h_buf, lambda rs, cs: x_ref[rs, cs], wout_ref, g_ref, b_ref,
                set_rows, lambda rs, cs: o_ref[rs, cs], rows)


def _pool_sample(x, st, win, wgrp, scale, wout, g, b):
    R, D = x.shape
    return pl.pallas_call(
        _pool_sample_kernel,
        grid=(1,),
        in_specs=[_resident(a.shape) for a in (x, st, win, wgrp, scale, wout, g, b)],
        out_specs=[pl.BlockSpec((R, D), lambda i: (0, 0)),
                   pl.BlockSpec(st.shape, lambda i: (0, 0, 0))],
        out_shape=[jax.ShapeDtypeStruct((R, D), F32), jax.ShapeDtypeStruct(st.shape, F32)],
        scratch_shapes=[pltpu.VMEM((R, D), BF16)],
        compiler_params=_params(1),
        name="pool_sample",
    )(x, st, win, wgrp, scale, wout, g, b)


def _kv_kernel(x_ref, cos_ref, sa_ref, sb_ref, wkv_ref, k_ref, v_ref, kb_ref, vb_ref):
    xb = x_ref[...].astype(BF16)
    kv = jnp.dot(xb, wkv_ref[...], preferred_element_type=F32)
    for j in range(KV_DIM // LANES):
        cs = slice(j * LANES, (j + 1) * LANES)
        k = _rope(kv[:, cs], cos_ref[...], sa_ref[...], sb_ref[...])
        k_ref[:, cs] = k
        kb_ref[:, cs] = k.astype(BF16)
    v = kv[:, KV_DIM:]
    v_ref[...] = v
    vb_ref[...] = v.astype(BF16)


def _kv_proj(x, tables, wkv, tile, table_tiles):
    M, D = x.shape
    tab = pl.BlockSpec((tile, LANES), lambda i: (i % table_tiles, 0))
    out = pl.BlockSpec((tile, KV_DIM), lambda i: (i, 0))
    return pl.pallas_call(
        _kv_kernel,
        grid=(M // tile,),
        in_specs=[pl.BlockSpec((tile, D), lambda i: (i, 0)), tab, tab, tab, _resident(wkv.shape)],
        out_specs=[out, out, out, out],
        out_shape=[jax.ShapeDtypeStruct((M, KV_DIM), F32), jax.ShapeDtypeStruct((M, KV_DIM), F32),
                   jax.ShapeDtypeStruct((M, KV_DIM), BF16), jax.ShapeDtypeStruct((M, KV_DIM), BF16)],
        compiler_params=_params(1),
        name="kv_proj",
    )(x, *tables, wkv)


def _attn_prompt_kernel(x_ref, cos_ref, sa_ref, sb_ref, k_ref, v_ref, win_ref, sink_ref, wout_ref,
                        g_ref, b_ref, o_ref, q_buf, qx_buf):
    h_buf = q_buf
    t = pl.program_id(1)
    xb = x_ref[0].astype(BF16)
    for n in range(N_CHUNKS):
        q = jnp.dot(xb, win_ref[:, n * COL_CHUNK:(n + 1) * COL_CHUNK], preferred_element_type=F32)
        for j in range(COL_CHUNK // LANES):
            qj = _rope(q[:, j * LANES:(j + 1) * LANES], cos_ref[...], sa_ref[...], sb_ref[...])
            c0 = n * COL_CHUNK + j * LANES
            q_buf[:, c0:c0 + LANES] = qj.astype(BF16)
            qx_buf[:, c0:c0 + LANES] = pltpu.roll(qj, HEAD_DIM, 1).astype(BF16)

    r = lax.broadcasted_iota(jnp.int32, (WINDOW, 2 * WINDOW), 0)
    c = lax.broadcasted_iota(jnp.int32, (WINDOW, 2 * WINDOW), 1)
    band = (c > r) & (c <= r + WINDOW)
    k_low = lax.broadcasted_iota(jnp.int32, (2 * WINDOW, LANES), 1) < HEAD_DIM
    o_low = lax.broadcasted_iota(jnp.int32, (WINDOW, LANES), 1) < HEAD_DIM
    def attend(qb, carry):
        blk = t * (TM // WINDOW) + qb
        prev = pl.multiple_of(jnp.maximum(blk - 1, 0) * WINDOW, WINDOW)
        cur = pl.multiple_of(blk * WINDOW, WINDOW)
        vis = band & ((c >= WINDOW) | (blk > 0))
        rows = pl.ds(pl.multiple_of(qb * WINDOW, WINDOW), WINDOW)
        for kv in range(N_KV_HEADS):
            ks = slice((kv // 2) * LANES, (kv // 2 + 1) * LANES)
            kv_low = kv % 2 == 0
            kslab = jnp.concatenate([k_ref[0, pl.ds(prev, WINDOW), ks], k_ref[0, pl.ds(cur, WINDOW), ks]], axis=0)
            vslab = jnp.concatenate([v_ref[0, pl.ds(prev, WINDOW), ks], v_ref[0, pl.ds(cur, WINDOW), ks]], axis=0)
            kslab = jnp.where(k_low == kv_low, kslab, jnp.zeros_like(kslab))
            vslab = jnp.where(k_low == kv_low, vslab, jnp.ones_like(vslab))
            heads = [(kv * (GROUP // 2) + pair, par) for pair in range(GROUP // 2) for par in range(2)]
            q_all = jnp.concatenate(
                [(q_buf if (par == 0) == kv_low else qx_buf)[rows, slab * LANES:(slab + 1) * LANES]
                 for slab, par in heads], axis=0)
            s_all = lax.dot_general(q_all, kslab, (((1,), (1,)), ((), ())), preferred_element_type=F32)
            p_all, sink_terms = [], []
            for i, (slab, par) in enumerate(heads):
                s = jnp.where(vis, s_all[i * WINDOW:(i + 1) * WINDOW], NEG)
                sink = sink_ref[2 * slab + par]
                m = jnp.maximum(jnp.max(s, axis=-1, keepdims=True), sink)
                p_all.append(jnp.exp(s - m).astype(BF16))
                sink_terms.append(jnp.exp(sink - m))
            o_all = jnp.dot(jnp.concatenate(p_all, axis=0), vslab, preferred_element_type=F32)
            for pair in range(GROUP // 2):
                halves = []
                for par in range(2):
                    i = 2 * pair + par
                    num = o_all[i * WINDOW:(i + 1) * WINDOW]
                    den = pltpu.roll(num, HEAD_DIM, 1)
                    if (par == 0) != kv_low:
                        num, den = den, num
                    halves.append(num / (den + sink_terms[i]))
                slab = kv * (GROUP // 2) + pair
                o_ref[0, rows, slab * LANES:(slab + 1) * LANES] = jnp.where(o_low, halves[0], halves[1])
        return carry

    lax.fori_loop(0, TM // WINDOW, attend, 0)

    for n in range(N_CHUNKS):
        cols = slice(n * COL_CHUNK, (n + 1) * COL_CHUNK)
        gate = jnp.dot(xb, win_ref[:, D_MODEL + n * COL_CHUNK:D_MODEL + (n + 1) * COL_CHUNK],
                       preferred_element_type=F32)
        h_buf[:, cols] = (o_ref[0, :, cols] * _silu(gate)).astype(BF16)

    def set_rows(rs, cs, v):
        o_ref[0, rs, cs] = v

    _outproj_ln(h_buf, lambda rs, cs: x_ref[0, rs, cs], wout_ref, g_ref, b_ref,
                set_rows, lambda rs, cs: o_ref[0, rs, cs], TM)


def _attn_prompt(x, tables, kb, vb, win, sinks, wout, g, b):
    B, S, D = x.shape
    tab = pl.BlockSpec((TM, LANES), lambda bi, t: (t, 0))
    kvspec = pl.BlockSpec((1, S, KV_DIM), lambda bi, t: (bi, 0, 0), pipeline_mode=pl.Buffered(1))
    return pl.pallas_call(
        _attn_prompt_kernel,
        grid=(B, S // TM),
        in_specs=[
            pl.BlockSpec((1, TM, D), lambda bi, t: (bi, t, 0)), tab, tab, tab, kvspec, kvspec,
            _resident(win.shape),
            pl.BlockSpec(memory_space=pltpu.SMEM),
            _resident(wout.shape), _resident(g.shape), _resident(b.shape),
        ],
        out_specs=pl.BlockSpec((1, TM, D), lambda bi, t: (bi, t, 0)),
        out_shape=jax.ShapeDtypeStruct((B, S, D), F32),
        scratch_shapes=[pltpu.VMEM((TM, D), BF16), pltpu.VMEM((TM, D), BF16)],
        compiler_params=_params(2),
        name="attn_prompt",
    )(x, *tables, kb, vb, win, sinks, wout, g, b)


def _qgate_sample_kernel(x_ref, cos_ref, sa_ref, sb_ref, win_ref, q_ref, gate_ref):
    xb = x_ref[...].astype(BF16)
    for n in range(N_CHUNKS):
        q = jnp.dot(xb, win_ref[:, n * COL_CHUNK:(n + 1) * COL_CHUNK], preferred_element_type=F32)
        for j in range(COL_CHUNK // LANES):
            c0 = n * COL_CHUNK + j * LANES
            q_ref[:, c0:c0 + LANES] = _rope(q[:, j * LANES:(j + 1) * LANES],
                                            cos_ref[...], sa_ref[...], sb_ref[...])
        gate_ref[:, n * COL_CHUNK:(n + 1) * COL_CHUNK] = jnp.dot(
            xb, win_ref[:, D_MODEL + n * COL_CHUNK:D_MODEL + (n + 1) * COL_CHUNK],
            preferred_element_type=F32)


def _qgate_sample(x, tables, win):
    R, D = x.shape
    return pl.pallas_call(
        _qgate_sample_kernel,
        grid=(1,),
        in_specs=[_resident(x.shape)] + [_resident(t.shape) for t in tables] + [_resident(win.shape)],
        out_specs=[pl.BlockSpec((R, D), lambda i: (0, 0)), pl.BlockSpec((R, D), lambda i: (0, 0))],
        out_shape=[jax.ShapeDtypeStruct((R, D), F32), jax.ShapeDtypeStruct((R, D), F32)],
        compiler_params=_params(1),
        name="qgate_sample",
    )(x, *tables, win)


def _attn_sample_kernel(q_ref, kn_ref, vn_ref, ck_ref, cv_ref, sink_ref, o_ref, nk_ref, nv_ref):
    nb = q_ref.shape[0]
    head_of_lane = lax.broadcasted_iota(jnp.int32, (N_HEADS, D_MODEL), 1) // HEAD_DIM
    own_head = head_of_lane == lax.broadcasted_iota(jnp.int32, (N_HEADS, D_MODEL), 0)
    low_half = lax.broadcasted_iota(jnp.int32, (N_HEADS, LANES), 1) < HEAD_DIM
    last_row = lax.broadcasted_iota(jnp.int32, (WINDOW, KV_DIM), 0) == WINDOW - 1
    sink = sink_ref[...]
    heads_per_slab = LANES // HEAD_DIM
    slabs_per_group = GROUP // heads_per_slab

    def body(i, carry):
        newk = jnp.where(last_row, kn_ref[pl.ds(i, 1), :], pltpu.roll(ck_ref[i], WINDOW - 1, 0))
        newv = jnp.where(last_row, vn_ref[pl.ds(i, 1), :], pltpu.roll(cv_ref[i], WINDOW - 1, 0))
        nk_ref[i] = newk
        nv_ref[i] = newv
        qh = jnp.where(own_head, jnp.broadcast_to(q_ref[pl.ds(i, 1), :], (N_HEADS, D_MODEL)), 0.0)
        folded = []
        for kv in range(N_KV_HEADS):
            w = qh[:, kv * GROUP * HEAD_DIM:kv * GROUP * HEAD_DIM + LANES]
            for sl in range(1, slabs_per_group):
                c0 = kv * GROUP * HEAD_DIM + sl * LANES
                w = w + qh[:, c0:c0 + LANES]
            folded.append(w + pltpu.roll(w, HEAD_DIM, 1))
        qg = jnp.concatenate([jnp.where(low_half, folded[2 * j], folded[2 * j + 1])
                              for j in range(N_KV_HEADS // 2)], axis=1)
        s = lax.dot_general(qg.astype(BF16), newk.astype(BF16), (((1,), (1,)), ((), ())),
                            preferred_element_type=F32)
        m = jnp.maximum(jnp.max(s, axis=-1, keepdims=True), sink)
        p = jnp.exp(s - m)
        denom = jnp.sum(p, axis=-1, keepdims=True) + jnp.exp(sink - m)
        og = jnp.dot(p.astype(BF16), newv.astype(BF16), preferred_element_type=F32) / denom
        slabs = []
        for kv in range(N_KV_HEADS):
            xs = og[:, (kv // 2) * LANES:(kv // 2 + 1) * LANES]
            rolled = pltpu.roll(xs, HEAD_DIM, 1)
            both = jnp.where(low_half, xs, rolled) if kv % 2 == 0 else jnp.where(low_half, rolled, xs)
            slabs.extend([both] * slabs_per_group)
        full = jnp.concatenate(slabs, axis=1)
        o_ref[pl.ds(i, 1), :] = jnp.sum(jnp.where(own_head, full, 0.0), axis=0, keepdims=True)
        return carry

    lax.fori_loop(0, nb, body, 0)


def _attn_sample(q, kn, vn, ck, cv, sinks_col):
    R, D = q.shape
    nb = SAMPLE_ATTN_BATCH
    row2 = lambda w: pl.BlockSpec((nb, w), lambda i: (i, 0))
    cache = pl.BlockSpec((nb, WINDOW, KV_DIM), lambda i: (i, 0, 0))
    return pl.pallas_call(
        _attn_sample_kernel,
        grid=(R // nb,),
        in_specs=[row2(D), row2(KV_DIM), row2(KV_DIM), cache, cache, _resident(sinks_col.shape)],
        out_specs=[row2(D), cache, cache],
        out_shape=[jax.ShapeDtypeStruct((R, D), F32),
                   jax.ShapeDtypeStruct(ck.shape, F32), jax.ShapeDtypeStruct(cv.shape, F32)],
        compiler_params=_params(1),
        name="attn_sample",
    )(q, kn, vn, ck, cv, sinks_col)


def _gated_out_sample_kernel(x_ref, a_ref, gate_ref, wout_ref, g_ref, b_ref, o_ref, h_buf):
    rows = x_ref.shape[0]
    h_buf[...] = (a_ref[...] * _silu(gate_ref[...])).astype(BF16)

    def set_rows(rs, cs, v):
        o_ref[rs, cs] = v

    _outproj_ln(h_buf, lambda rs, cs: x_ref[rs, cs], wout_ref, g_ref, b_ref,
                set_rows, lambda rs, cs: o_ref[rs, cs], rows)


def _gated_out_sample(x, a, gate, wout, g, b):
    R, D = x.shape
    return pl.pallas_call(
        _gated_out_sample_kernel,
        grid=(1,),
        in_specs=[_resident(t.shape) for t in (x, a, gate, wout, g, b)],
        out_specs=pl.BlockSpec((R, D), lambda i: (0, 0)),
        out_shape=jax.ShapeDtypeStruct((R, D), F32),
        scratch_shapes=[pltpu.VMEM((R, D), BF16)],
        compiler_params=_params(1),
        name="gated_out_sample",
    )(x, a, gate, wout, g, b)


def _rope_tables(pos):
    half = ROT_DIM // 2
    inv_freq = ROPE_THETA ** (-jnp.arange(0, ROT_DIM, 2, dtype=F32) / ROT_DIM)
    ang = pos.astype(F32)[:, None] * inv_freq[None, :]
    cos, sin = jnp.cos(ang), jnp.sin(ang)
    n = pos.shape[0]
    rest = jnp.zeros((n, HEAD_DIM - ROT_DIM), F32)
    zero = jnp.zeros((n, half), F32)
    cos_h = jnp.concatenate([cos, cos, rest + 1.0], axis=1)
    sa_h = jnp.concatenate([-sin, zero, rest], axis=1)
    sb_h = jnp.concatenate([zero, sin, rest], axis=1)
    rep = LANES // HEAD_DIM
    return tuple(jnp.tile(a, (1, rep)) for a in (cos_h, sa_h, sb_h))


def kernel(x_prompt, x_sample, state_pool, cache_k, cache_v, w_in_a, w_grp_a, scale_a, w_out_a,
           w_kv, w_in_b, sinks_b, w_out_b, ln_g, ln_b):
    B, S, D = x_prompt.shape
    R = x_sample.shape[0]
    xp = x_prompt
    xs = x_sample.reshape(R, D)
    tab_p = _rope_tables(jnp.arange(S, dtype=jnp.int32))
    tab_s = _rope_tables(jnp.full((R,), PAST_LEN, jnp.int32))
    qtab_p = tuple(a * SM_SCALE for a in tab_p)
    qtab_s = tuple(a * SM_SCALE for a in tab_s)
    pool_p, pool_s = [], []
    for i in range(N_A_LAYERS):
        win, wgrp, wout = w_in_a[i].astype(BF16), w_grp_a[i].astype(BF16), w_out_a[i].astype(BF16)
        scale, g, b = scale_a[i][None, :], ln_g[i][None, :], ln_b[i][None, :]
        xp, sp = _pool_prompt(xp, win, wgrp, scale, wout, g, b)
        xs, ss = _pool_sample(xs, state_pool[i].transpose(1, 0, 2), win, wgrp, scale, wout, g, b)
        pool_p.append(sp[:, HALO - POOL_STATE:])
        pool_s.append(ss.transpose(1, 0, 2))
    wkv = w_kv.astype(BF16)
    k_p, v_p, kb_p, vb_p = _kv_proj(xp.reshape(B * S, D), tab_p, wkv, TK, S // TK)
    k_s, v_s, _, _ = _kv_proj(xs, tab_s, wkv, R, 1)
    kb_p, vb_p = kb_p.reshape(B, S, KV_DIM), vb_p.reshape(B, S, KV_DIM)
    ck = cache_k.reshape(R, WINDOW, KV_DIM)
    cv = cache_v.reshape(R, WINDOW, KV_DIM)
    for j in range(DEPTH - N_A_LAYERS):
        i = N_A_LAYERS + j
        win, wout = w_in_b[j].astype(BF16), w_out_b[j].astype(BF16)
        g, b = ln_g[i][None, :], ln_b[i][None, :]
        xp = _attn_prompt(xp, qtab_p, kb_p, vb_p, win, sinks_b[j], wout, g, b)
        q_s, gate_s = _qgate_sample(xs, qtab_s, win)
        a_s, nk, nv = _attn_sample(q_s, k_s, v_s, ck, cv, sinks_b[j][:, None])
        xs = _gated_out_sample(xs, a_s, gate_s, wout, g, b)
    kv4 = (N_KV_HEADS, HEAD_DIM)
    new_k_p = k_p.reshape(B, S, KV_DIM)[:, S - WINDOW:].reshape(B, WINDOW, *kv4)
    new_v_p = v_p.reshape(B, S, KV_DIM)[:, S - WINDOW:].reshape(B, WINDOW, *kv4)
    return (xp, xs.reshape(R, 1, D), jnp.stack(pool_p, axis=0), jnp.stack(pool_s, axis=0),
            new_k_p, new_v_p, nk.reshape(R, WINDOW, *kv4), nv.reshape(R, WINDOW, *kv4))
```

```python
import functools

import jax
import jax.numpy as jnp
from jax import lax
from jax.experimental import pallas as pl
from jax.experimental.pallas import tpu as pltpu

F32 = jnp.float32
BF16 = jnp.bfloat16

D_MODEL = 2048
DEPTH = 4
PAST_LEN = 16384
N_A_LAYERS = DEPTH // 2
POOL_WINDOWS = (2, 4, 8, 16)
POOL_GROUP = D_MODEL // len(POOL_WINDOWS)
POOL_STATE = max(POOL_WINDOWS) - 1
HEAD_DIM = 64
N_HEADS = D_MODEL // HEAD_DIM
N_KV_HEADS = N_HEADS // 8
GROUP = N_HEADS // N_KV_HEADS
KV_DIM = N_KV_HEADS * HEAD_DIM
WINDOW = 128
ROT_DIM = HEAD_DIM // 4
ROPE_THETA = 500000.0
ALPHA = (2 * DEPTH) ** 0.25
LN_EPS = 1e-5
NEG = -1e30
SM_SCALE = HEAD_DIM ** -0.5

LANES = 128
HALO = 16
COL_CHUNK = 512
N_CHUNKS = D_MODEL // COL_CHUNK
TM = 256
TK = 512
LN_ROWS = 32
SAMPLE_ATTN_BATCH = 8
SUM_ROWS = 16
VMEM_LIMIT_BYTES = 56 * 1024 * 1024


def _params(n_axes):
    return pltpu.CompilerParams(dimension_semantics=("arbitrary",) * n_axes,
                                vmem_limit_bytes=VMEM_LIMIT_BYTES)


def _resident(shape):
    zeros = (0,) * len(shape)
    return pl.BlockSpec(shape, lambda *_: zeros, pipeline_mode=pl.Buffered(1))


def _silu(g):
    return g / (1.0 + jnp.exp(-g))


def _rope(x, cos, sa, sb):
    return x * cos + pltpu.roll(x, LANES - ROT_DIM // 2, 1) * sa + pltpu.roll(x, ROT_DIM // 2, 1) * sb


def _outproj_ln(h_ref, x_rows, wout_ref, g_ref, b_ref, o_rows_set, o_rows_get, rows):
    for n in range(N_CHUNKS):
        cols = slice(n * COL_CHUNK, (n + 1) * COL_CHUNK)
        y = jnp.dot(h_ref[...], wout_ref[:, cols], preferred_element_type=F32)
        o_rows_set(slice(0, rows), cols, ALPHA * x_rows(slice(0, rows), cols) + y)
    step = min(LN_ROWS, rows)
    for r0 in range(0, rows, step):
        rs = slice(r0, r0 + step)
        r = o_rows_get(rs, slice(None))
        mu = jnp.mean(r, axis=-1, keepdims=True)
        c = r - mu
        var = jnp.mean(c * c, axis=-1, keepdims=True)
        o_rows_set(rs, slice(None), c * lax.rsqrt(var + LN_EPS) * g_ref[...] + b_ref[...])


def _pool_prompt_kernel(x_ref, win_ref, wgrp_ref, scale_ref, wout_ref, g_ref, b_ref,
                        o_ref, state_ref, u_buf, h_buf):
    t = pl.program_id(1)

    @pl.when(t == 0)
    def _():
        u_buf[0:HALO, :] = jnp.zeros((HALO, D_MODEL), F32)

    xb = x_ref[0].astype(BF16)
    row = lax.broadcasted_iota(jnp.int32, (TM, 1), 0) + t * TM
    for g, w in enumerate(POOL_WINDOWS):
        cols = slice(g * POOL_GROUP, (g + 1) * POOL_GROUP)
        u_buf[HALO:, cols] = jnp.dot(xb, win_ref[:, cols], preferred_element_type=F32)
        ext = u_buf[:, cols]
        s = ext
        shift = 1
        while shift < w:
            s = s + pltpu.roll(s, shift, 0)
            shift *= 2
        inv_cnt = 1.0 / jnp.minimum(w, row + 1).astype(F32)
        d = s[HALO:, :] * inv_cnt - ext[HALO:, :]
        d = jnp.dot(d.astype(BF16), wgrp_ref[g], preferred_element_type=F32) * scale_ref[:, cols]
        gate = jnp.dot(xb, win_ref[:, D_MODEL + g * POOL_GROUP:D_MODEL + (g + 1) * POOL_GROUP],
                       preferred_element_type=F32)
        h_buf[:, cols] = (d * _silu(gate)).astype(BF16)

    @pl.when(t == pl.num_programs(1) - 1)
    def _():
        state_ref[0] = u_buf[TM:TM + HALO, :]

    u_buf[0:HALO, :] = u_buf[TM:TM + HALO, :]

    def set_rows(rs, cs, v):
        o_ref[0, rs, cs] = v

    _outproj_ln(h_buf, lambda rs, cs: x_ref[0, rs, cs], wout_ref, g_ref, b_ref,
                set_rows, lambda rs, cs: o_ref[0, rs, cs], TM)


def _pool_prompt(x, win, wgrp, scale, wout, g, b):
    B, S, D = x.shape
    return pl.pallas_call(
        _pool_prompt_kernel,
        grid=(B, S // TM),
        in_specs=[
            pl.BlockSpec((1, TM, D), lambda bi, t: (bi, t, 0)),
            _resident(win.shape), _resident(wgrp.shape), _resident(scale.shape),
            _resident(wout.shape), _resident(g.shape), _resident(b.shape),
        ],
        out_specs=[
            pl.BlockSpec((1, TM, D), lambda bi, t: (bi, t, 0)),
            pl.BlockSpec((1, HALO, D), lambda bi, t: (bi, 0, 0)),
        ],
        out_shape=[jax.ShapeDtypeStruct((B, S, D), F32),
                   jax.ShapeDtypeStruct((B, HALO, D), F32)],
        scratch_shapes=[pltpu.VMEM((HALO + TM, D), F32), pltpu.VMEM((TM, D), BF16)],
        compiler_params=_params(2),
        name="pool_prompt",
    )(x, win, wgrp, scale, wout, g, b)


def _pool_sample_kernel(x_ref, st_ref, win_ref, wgrp_ref, scale_ref, wout_ref, g_ref, b_ref,
                        o_ref, nst_ref, h_buf):
    rows = x_ref.shape[0]
    xb = x_ref[...].astype(BF16)
    for g, w in enumerate(POOL_WINDOWS):
        cols = slice(g * POOL_GROUP, (g + 1) * POOL_GROUP)
        u = jnp.dot(xb, win_ref[:, cols], preferred_element_type=F32)
        acc = u
        for j in range(1, w):
            acc = acc + st_ref[POOL_STATE - j, :, cols]
        d = acc * (1.0 / min(w, PAST_LEN + 1)) - u
        d = jnp.dot(d.astype(BF16), wgrp_ref[g], preferred_element_type=F32) * scale_ref[:, cols]
        gate = jnp.dot(xb, win_ref[:, D_MODEL + g * POOL_GROUP:D_MODEL + (g + 1) * POOL_GROUP],
                       preferred_element_type=F32)
        h_buf[:, cols] = (d * _silu(gate)).astype(BF16)
        for j in range(POOL_STATE - 1):
            nst_ref[j, :, cols] = st_ref[j + 1, :, cols]
        nst_ref[POOL_STATE - 1, :, cols] = u

    def set_rows(rs, cs, v):
        o_ref[rs, cs] = v

    _outproj_ln(h_buf, lambda rs, cs: x_ref[rs, cs], wout_ref, g_ref, b_ref,
                set_rows, lambda rs, cs: o_ref[rs, cs], rows)


def _pool_sample(x, st, win, wgrp, scale, wout, g, b):
    R, D = x.shape
    return pl.pallas_call(
        _pool_sample_kernel,
        grid=(1,),
        in_specs=[_resident(a.shape) for a in (x, st, win, wgrp, scale, wout, g, b)],
        out_specs=[pl.BlockSpec((R, D), lambda i: (0, 0)),
                   pl.BlockSpec(st.shape, lambda i: (0, 0, 0))],
        out_shape=[jax.ShapeDtypeStruct((R, D), F32), jax.ShapeDtypeStruct(st.shape, F32)],
        scratch_shapes=[pltpu.VMEM((R, D), BF16)],
        compiler_params=_params(1),
        name="pool_sample",
    )(x, st, win, wgrp, scale, wout, g, b)


def _project_kv(xb, cos, sa, sb, wkv_ref):
    kv = jnp.dot(xb, wkv_ref[...], preferred_element_type=F32)
    k_slabs = [_rope(kv[:, j * LANES:(j + 1) * LANES], cos, sa, sb) for j in range(KV_DIM // LANES)]
    return k_slabs, kv[:, KV_DIM:]


def _kv_prompt_kernel(x_ref, cos_ref, sa_ref, sb_ref, wkv_ref, knew_ref, vnew_ref, kdup_ref, vt_ref):
    k_slabs, v = _project_kv(x_ref[0].astype(BF16), cos_ref[...], sa_ref[...], sb_ref[...], wkv_ref)
    low = lax.broadcasted_iota(jnp.int32, (TK, LANES), 1) < HEAD_DIM
    for j, k in enumerate(k_slabs):
        swapped = pltpu.roll(k, HEAD_DIM, 1)
        kdup_ref[0, :, (2 * j) * LANES:(2 * j + 1) * LANES] = jnp.where(low, k, swapped).astype(BF16)
        kdup_ref[0, :, (2 * j + 1) * LANES:(2 * j + 2) * LANES] = jnp.where(low, swapped, k).astype(BF16)
    for i in range(TK // WINDOW):
        vt_ref[0, i] = v[i * WINDOW:(i + 1) * WINDOW, :].T.astype(BF16)

    @pl.when(pl.program_id(1) == pl.num_programs(1) - 1)
    def _():
        for j, k in enumerate(k_slabs):
            knew_ref[0, :, j * LANES:(j + 1) * LANES] = k[TK - WINDOW:, :]
        vnew_ref[0] = v[TK - WINDOW:, :]


def _kv_prompt(x, tables, wkv):
    B, S, D = x.shape
    tab = pl.BlockSpec((TK, LANES), lambda bi, t: (t, 0))
    last = pl.BlockSpec((1, WINDOW, KV_DIM), lambda bi, t: (bi, 0, 0))
    return pl.pallas_call(
        _kv_prompt_kernel,
        grid=(B, S // TK),
        in_specs=[pl.BlockSpec((1, TK, D), lambda bi, t: (bi, t, 0)), tab, tab, tab, _resident(wkv.shape)],
        out_specs=[last, last,
                   pl.BlockSpec((1, TK, N_KV_HEADS * LANES), lambda bi, t: (bi, t, 0)),
                   pl.BlockSpec((1, TK // WINDOW, KV_DIM, WINDOW), lambda bi, t: (bi, t, 0, 0))],
        out_shape=[jax.ShapeDtypeStruct((B, WINDOW, KV_DIM), F32), jax.ShapeDtypeStruct((B, WINDOW, KV_DIM), F32),
                   jax.ShapeDtypeStruct((B, S, N_KV_HEADS * LANES), BF16),
                   jax.ShapeDtypeStruct((B, S // WINDOW, KV_DIM, WINDOW), BF16)],
        compiler_params=_params(2),
        name="kv_prompt",
    )(x, *tables, wkv)


def _kv_sample_kernel(x_ref, cos_ref, sa_ref, sb_ref, wkv_ref, k_ref, v_ref):
    k_slabs, v = _project_kv(x_ref[...].astype(BF16), cos_ref[...], sa_ref[...], sb_ref[...], wkv_ref)
    for j, k in enumerate(k_slabs):
        k_ref[:, j * LANES:(j + 1) * LANES] = k
    v_ref[...] = v


def _kv_sample(x, tables, wkv):
    R, D = x.shape
    out = pl.BlockSpec((R, KV_DIM), lambda i: (0, 0))
    return pl.pallas_call(
        _kv_sample_kernel,
        grid=(1,),
        in_specs=[_resident(x.shape)] + [_resident(t.shape) for t in tables] + [_resident(wkv.shape)],
        out_specs=[out, out],
        out_shape=[jax.ShapeDtypeStruct((R, KV_DIM), F32), jax.ShapeDtypeStruct((R, KV_DIM), F32)],
        compiler_params=_params(1),
        name="kv_sample",
    )(x, *tables, wkv)


def _attn_prompt_kernel(x_ref, cos_ref, sa_ref, sb_ref, kdup_ref, vt_ref, win_ref, sink_ref, wout_ref,
                        g_ref, b_ref, o_ref, q_buf):
    h_buf = q_buf
    t = pl.program_id(1)
    xb = x_ref[0].astype(BF16)
    for n in range(N_CHUNKS):
        q = jnp.dot(xb, win_ref[:, n * COL_CHUNK:(n + 1) * COL_CHUNK], preferred_element_type=F32)
        for j in range(COL_CHUNK // LANES):
            qj = _rope(q[:, j * LANES:(j + 1) * LANES], cos_ref[...], sa_ref[...], sb_ref[...])
            c0 = n * COL_CHUNK + j * LANES
            q_buf[:, c0:c0 + LANES] = qj.astype(BF16)

    key = lax.broadcasted_iota(jnp.int32, (2 * WINDOW, WINDOW), 0)
    qry = lax.broadcasted_iota(jnp.int32, (2 * WINDOW, WINDOW), 1)
    band = (key > qry) & (key <= qry + WINDOW)
    low_half = lax.broadcasted_iota(jnp.int32, (WINDOW, LANES), 1) < HEAD_DIM
    ones_rows = jnp.ones((SUM_ROWS, 2 * WINDOW), BF16)

    def attend(qb, carry):
        blk = t * (TM // WINDOW) + qb
        prev_blk = jnp.maximum(blk - 1, 0)
        prev = pl.multiple_of(prev_blk * WINDOW, WINDOW)
        cur = pl.multiple_of(blk * WINDOW, WINDOW)
        vis = band & ((key >= WINDOW) | (blk > 0))
        rows = pl.ds(pl.multiple_of(qb * WINDOW, WINDOW), WINDOW)
        for kv in range(N_KV_HEADS):
            ks = slice(kv * LANES, (kv + 1) * LANES)
            vs = slice(kv * HEAD_DIM, (kv + 1) * HEAD_DIM)
            k2 = jnp.concatenate([kdup_ref[0, pl.ds(prev, WINDOW), ks], kdup_ref[0, pl.ds(cur, WINDOW), ks]], axis=0)
            v_aug = jnp.concatenate([vt_ref[0, prev_blk, vs, :], vt_ref[0, blk, vs, :]], axis=1)
            v_aug = jnp.concatenate([v_aug, ones_rows], axis=0)
            heads = [(kv * (GROUP // 2) + pair, par) for pair in range(GROUP // 2) for par in range(2)]
            q_all = []
            for slab, par in heads:
                q_slab = q_buf[rows, slab * LANES:(slab + 1) * LANES]
                q_all.append(jnp.where(low_half == (par == 0), q_slab, jnp.zeros_like(q_slab)))
            s_t = lax.dot_general(k2, jnp.concatenate(q_all, axis=0), (((1,), (1,)), ((), ())),
                                  preferred_element_type=F32)
            p_t, sink_terms = [], []
            for i, (slab, par) in enumerate(heads):
                s = jnp.where(vis, s_t[:, i * WINDOW:(i + 1) * WINDOW], NEG)
                sink = sink_ref[2 * slab + par]
                m = jnp.maximum(jnp.max(s, axis=0, keepdims=True), sink)
                p_t.append(jnp.exp(s - m).astype(BF16))
                sink_terms.append(jnp.exp(sink - m))
            o_t = jnp.dot(v_aug, jnp.concatenate(p_t, axis=1), preferred_element_type=F32)
            for pair in range(GROUP // 2):
                both = []
                for par in range(2):
                    i = 2 * pair + par
                    cs = slice(i * WINDOW, (i + 1) * WINDOW)
                    inv = 1.0 / (o_t[HEAD_DIM:HEAD_DIM + 1, cs] + sink_terms[i])
                    both.append(o_t[:HEAD_DIM, cs] * inv)
                slab = kv * (GROUP // 2) + pair
                o_ref[0, rows, slab * LANES:(slab + 1) * LANES] = jnp.concatenate(both, axis=0).T
        return carry

    lax.fori_loop(0, TM // WINDOW, attend, 0)

    for n in range(N_CHUNKS):
        cols = slice(n * COL_CHUNK, (n + 1) * COL_CHUNK)
        gate = jnp.dot(xb, win_ref[:, D_MODEL + n * COL_CHUNK:D_MODEL + (n + 1) * COL_CHUNK],
                       preferred_element_type=F32)
        h_buf[:, cols] = (o_ref[0, :, cols] * _silu(gate)).astype(BF16)

    def set_rows(rs, cs, v):
        o_ref[0, rs, cs] = v

    _outproj_ln(h_buf, lambda rs, cs: x_ref[0, rs, cs], wout_ref, g_ref, b_ref,
                set_rows, lambda rs, cs: o_ref[0, rs, cs], TM)


def _attn_prompt(x, tables, kdup, vt, win, sinks, wout, g, b):
    B, S, D = x.shape
    tab = pl.BlockSpec((TM, LANES), lambda bi, t: (t, 0))

    def per_batch(a):
        return pl.BlockSpec((1,) + a.shape[1:], lambda bi, t: (bi,) + (0,) * (a.ndim - 1),
                            pipeline_mode=pl.Buffered(1))

    return pl.pallas_call(
        _attn_prompt_kernel,
        grid=(B, S // TM),
        in_specs=[
            pl.BlockSpec((1, TM, D), lambda bi, t: (bi, t, 0)), tab, tab, tab, per_batch(kdup), per_batch(vt),
            _resident(win.shape),
            pl.BlockSpec(memory_space=pltpu.SMEM),
            _resident(wout.shape), _resident(g.shape), _resident(b.shape),
        ],
        out_specs=pl.BlockSpec((1, TM, D), lambda bi, t: (bi, t, 0)),
        out_shape=jax.ShapeDtypeStruct((B, S, D), F32),
        scratch_shapes=[pltpu.VMEM((TM, D), BF16)],
        compiler_params=_params(2),
        name="attn_prompt",
    )(x, *tables, kdup, vt, win, sinks, wout, g, b)


def _qgate_sample_kernel(x_ref, cos_ref, sa_ref, sb_ref, win_ref, q_ref, gate_ref):
    xb = x_ref[...].astype(BF16)
    for n in range(N_CHUNKS):
        q = jnp.dot(xb, win_ref[:, n * COL_CHUNK:(n + 1) * COL_CHUNK], preferred_element_type=F32)
        for j in range(COL_CHUNK // LANES):
            c0 = n * COL_CHUNK + j * LANES
            q_ref[:, c0:c0 + LANES] = _rope(q[:, j * LANES:(j + 1) * LANES],
                                            cos_ref[...], sa_ref[...], sb_ref[...])
        gate_ref[:, n * COL_CHUNK:(n + 1) * COL_CHUNK] = jnp.dot(
            xb, win_ref[:, D_MODEL + n * COL_CHUNK:D_MODEL + (n + 1) * COL_CHUNK],
            preferred_element_type=F32)


def _qgate_sample(x, tables, win):
    R, D = x.shape
    return pl.pallas_call(
        _qgate_sample_kernel,
        grid=(1,),
        in_specs=[_resident(x.shape)] + [_resident(t.shape) for t in tables] + [_resident(win.shape)],
        out_specs=[pl.BlockSpec((R, D), lambda i: (0, 0)), pl.BlockSpec((R, D), lambda i: (0, 0))],
        out_shape=[jax.ShapeDtypeStruct((R, D), F32), jax.ShapeDtypeStruct((R, D), F32)],
        compiler_params=_params(1),
        name="qgate_sample",
    )(x, *tables, win)


def _attn_sample_kernel(q_ref, kn_ref, vn_ref, ck_ref, cv_ref, sink_ref, o_ref, nk_ref, nv_ref):
    nb = q_ref.shape[0]
    head_of_lane = lax.broadcasted_iota(jnp.int32, (N_HEADS, D_MODEL), 1) // HEAD_DIM
    own_head = head_of_lane == lax.broadcasted_iota(jnp.int32, (N_HEADS, D_MODEL), 0)
    low_half = lax.broadcasted_iota(jnp.int32, (N_HEADS, LANES), 1) < HEAD_DIM
    last_row = lax.broadcasted_iota(jnp.int32, (WINDOW, KV_DIM), 0) == WINDOW - 1
    sink = sink_ref[...]
    heads_per_slab = LANES // HEAD_DIM
    slabs_per_group = GROUP // heads_per_slab

    def body(i, carry):
        newk = jnp.where(last_row, kn_ref[pl.ds(i, 1), :], pltpu.roll(ck_ref[i], WINDOW - 1, 0))
        newv = jnp.where(last_row, vn_ref[pl.ds(i, 1), :], pltpu.roll(cv_ref[i], WINDOW - 1, 0))
        nk_ref[i] = newk
        nv_ref[i] = newv
        qh = jnp.where(own_head, jnp.broadcast_to(q_ref[pl.ds(i, 1), :], (N_HEADS, D_MODEL)), 0.0)
        folded = []
        for kv in range(N_KV_HEADS):
            w = qh[:, kv * GROUP * HEAD_DIM:kv * GROUP * HEAD_DIM + LANES]
            for sl in range(1, slabs_per_group):
                c0 = kv * GROUP * HEAD_DIM + sl * LANES
                w = w + qh[:, c0:c0 + LANES]
            folded.append(w + pltpu.roll(w, HEAD_DIM, 1))
        qg = jnp.concatenate([jnp.where(low_half, folded[2 * j], folded[2 * j + 1])
                              for j in range(N_KV_HEADS // 2)], axis=1)
        s = lax.dot_general(qg.astype(BF16), newk.astype(BF16), (((1,), (1,)), ((), ())),
                            preferred_element_type=F32)
        m = jnp.maximum(jnp.max(s, axis=-1, keepdims=True), sink)
        p = jnp.exp(s - m)
        denom = jnp.sum(p, axis=-1, keepdims=True) + jnp.exp(sink - m)
        og = jnp.dot(p.astype(BF16), newv.astype(BF16), preferred_element_type=F32) / denom
        slabs = []
        for kv in range(N_KV_HEADS):
            xs = og[:, (kv // 2) * LANES:(kv // 2 + 1) * LANES]
            rolled = pltpu.roll(xs, HEAD_DIM, 1)
            both = jnp.where(low_half, xs, rolled) if kv % 2 == 0 else jnp.where(low_half, rolled, xs)
            slabs.extend([both] * slabs_per_group)
        full = jnp.concatenate(slabs, axis=1)
        o_ref[pl.ds(i, 1), :] = jnp.sum(jnp.where(own_head, full, 0.0), axis=0, keepdims=True)
        return carry

    lax.fori_loop(0, nb, body, 0)


def _attn_sample(q, kn, vn, ck, cv, sinks_col):
    R, D = q.shape
    nb = SAMPLE_ATTN_BATCH
    row2 = lambda w: pl.BlockSpec((nb, w), lambda i: (i, 0))
    cache = pl.BlockSpec((nb, WINDOW, KV_DIM), lambda i: (i, 0, 0))
    return pl.pallas_call(
        _attn_sample_kernel,
        grid=(R // nb,),
        in_specs=[row2(D), row2(KV_DIM), row2(KV_DIM), cache, cache, _resident(sinks_col.shape)],
        out_specs=[row2(D), cache, cache],
        out_shape=[jax.ShapeDtypeStruct((R, D), F32),
                   jax.ShapeDtypeStruct(ck.shape, F32), jax.ShapeDtypeStruct(cv.shape, F32)],
        compiler_params=_params(1),
        name="attn_sample",
    )(q, kn, vn, ck, cv, sinks_col)


def _gated_out_sample_kernel(x_ref, a_ref, gate_ref, wout_ref, g_ref, b_ref, o_ref, h_buf):
    rows = x_ref.shape[0]
    h_buf[...] = (a_ref[...] * _silu(gate_ref[...])).astype(BF16)

    def set_rows(rs, cs, v):
        o_ref[rs, cs] = v

    _outproj_ln(h_buf, lambda rs, cs: x_ref[rs, cs], wout_ref, g_ref, b_ref,
                set_rows, lambda rs, cs: o_ref[rs, cs], rows)


def _gated_out_sample(x, a, gate, wout, g, b):
    R, D = x.shape
    return pl.pallas_call(
        _gated_out_sample_kernel,
        grid=(1,),
        in_specs=[_resident(t.shape) for t in (x, a, gate, wout, g, b)],
        out_specs=pl.BlockSpec((R, D), lambda i: (0, 0)),
        out_shape=jax.ShapeDtypeStruct((R, D), F32),
        scratch_shapes=[pltpu.VMEM((R, D), BF16)],
        compiler_params=_params(1),
        name="gated_out_sample",
    )(x, a, gate, wout, g, b)


def _rope_tables(pos):
    half = ROT_DIM // 2
    inv_freq = ROPE_THETA ** (-jnp.arange(0, ROT_DIM, 2, dtype=F32) / ROT_DIM)
    ang = pos.astype(F32)[:, None] * inv_freq[None, :]
    cos, sin = jnp.cos(ang), jnp.sin(ang)
    n = pos.shape[0]
    rest = jnp.zeros((n, HEAD_DIM - ROT_DIM), F32)
    zero = jnp.zeros((n, half), F32)
    cos_h = jnp.concatenate([cos, cos, rest + 1.0], axis=1)
    sa_h = jnp.concatenate([-sin, zero, rest], axis=1)
    sb_h = jnp.concatenate([zero, sin, rest], axis=1)
    rep = LANES // HEAD_DIM
    return tuple(jnp.tile(a, (1, rep)) for a in (cos_h, sa_h, sb_h))


def kernel(x_prompt, x_sample, state_pool, cache_k, cache_v, w_in_a, w_grp_a, scale_a, w_out_a,
           w_kv, w_in_b, sinks_b, w_out_b, ln_g, ln_b):
    B, S, D = x_prompt.shape
    R = x_sample.shape[0]
    xp = x_prompt
    xs = x_sample.reshape(R, D)
    tab_p = _rope_tables(jnp.arange(S, dtype=jnp.int32))
    tab_s = _rope_tables(jnp.full((R,), PAST_LEN, jnp.int32))
    qtab_p = tuple(a * SM_SCALE for a in tab_p)
    qtab_s = tuple(a * SM_SCALE for a in tab_s)
    pool_p, pool_s = [], []
    for i in range(N_A_LAYERS):
        win, wgrp, wout = w_in_a[i].astype(BF16), w_grp_a[i].astype(BF16), w_out_a[i].astype(BF16)
        scale, g, b = scale_a[i][None, :], ln_g[i][None, :], ln_b[i][None, :]
        xp, sp = _pool_prompt(xp, win, wgrp, scale, wout, g, b)
        xs, ss = _pool_sample(xs, state_pool[i].transpose(1, 0, 2), win, wgrp, scale, wout, g, b)
        pool_p.append(sp[:, HALO - POOL_STATE:])
        pool_s.append(ss.transpose(1, 0, 2))
    wkv = w_kv.astype(BF16)
    new_k_p, new_v_p, kdup_p, vt_p = _kv_prompt(xp, tab_p, wkv)
    k_s, v_s = _kv_sample(xs, tab_s, wkv)
    ck = cache_k.reshape(R, WINDOW, KV_DIM)
    cv = cache_v.reshape(R, WINDOW, KV_DIM)
    for j in range(DEPTH - N_A_LAYERS):
        i = N_A_LAYERS + j
        win, wout = w_in_b[j].astype(BF16), w_out_b[j].astype(BF16)
        g, b = ln_g[i][None, :], ln_b[i][None, :]
        xp = _attn_prompt(xp, qtab_p, kdup_p, vt_p, win, sinks_b[j], wout, g, b)
        q_s, gate_s = _qgate_sample(xs, qtab_s, win)
        a_s, nk, nv = _attn_sample(q_s, k_s, v_s, ck, cv, sinks_b[j][:, None])
        xs = _gated_out_sample(xs, a_s, gate_s, wout, g, b)
    kv4 = (N_KV_HEADS, HEAD_DIM)
    return (xp, xs.reshape(R, 1, D), jnp.stack(pool_p, axis=0), jnp.stack(pool_s, axis=0),
            new_k_p.reshape(B, WINDOW, *kv4), new_v_p.reshape(B, WINDOW, *kv4),
            nk.reshape(R, WINDOW, *kv4), nv.reshape(R, WINDOW, *kv4))
```

```python
import functools

import jax
import jax.numpy as jnp
from jax import lax
from jax.experimental import pallas as pl
from jax.experimental.pallas import tpu as pltpu

F32 = jnp.float32
BF16 = jnp.bfloat16

D_MODEL = 2048
DEPTH = 4
PAST_LEN = 16384
N_A_LAYERS = DEPTH // 2
POOL_WINDOWS = (2, 4, 8, 16)
POOL_GROUP = D_MODEL // len(POOL_WINDOWS)
POOL_STATE = max(POOL_WINDOWS) - 1
HEAD_DIM = 64
N_HEADS = D_MODEL // HEAD_DIM
N_KV_HEADS = N_HEADS // 8
GROUP = N_HEADS // N_KV_HEADS
KV_DIM = N_KV_HEADS * HEAD_DIM
WINDOW = 128
ROT_DIM = HEAD_DIM // 4
ROPE_THETA = 500000.0
ALPHA = (2 * DEPTH) ** 0.25
LN_EPS = 1e-5
NEG = -1e30
SM_SCALE = HEAD_DIM ** -0.5

LANES = 128
HALO = 16
COL_CHUNK = 512
N_CHUNKS = D_MODEL // COL_CHUNK
TM = 256
TK = 512
LN_ROWS = 32
SAMPLE_ATTN_BATCH = 8
SUM_ROWS = 16
VMEM_LIMIT_BYTES = 56 * 1024 * 1024


def _params(n_axes):
    return pltpu.CompilerParams(dimension_semantics=("arbitrary",) * n_axes,
                                vmem_limit_bytes=VMEM_LIMIT_BYTES)


def _resident(shape):
    zeros = (0,) * len(shape)
    return pl.BlockSpec(shape, lambda *_: zeros, pipeline_mode=pl.Buffered(1))


def _layer(stacked, i):
    zeros = (0,) * (stacked.ndim - 1)
    return stacked, pl.BlockSpec((None,) + stacked.shape[1:], lambda *_: (i,) + zeros,
                                 pipeline_mode=pl.Buffered(1))


def _silu(g):
    return g / (1.0 + jnp.exp(-g))


def _rope(x, cos, sa, sb):
    return x * cos + pltpu.roll(x, LANES - ROT_DIM // 2, 1) * sa + pltpu.roll(x, ROT_DIM // 2, 1) * sb


def _outproj_ln(h_ref, x_rows, wout_ref, g_ref, b_ref, o_rows_set, o_rows_get, rows):
    for n in range(N_CHUNKS):
        cols = slice(n * COL_CHUNK, (n + 1) * COL_CHUNK)
        y = jnp.dot(h_ref[...], wout_ref[:, cols], preferred_element_type=F32)
        o_rows_set(slice(0, rows), cols, ALPHA * x_rows(slice(0, rows), cols) + y)
    step = min(LN_ROWS, rows)
    for r0 in range(0, rows, step):
        rs = slice(r0, r0 + step)
        r = o_rows_get(rs, slice(None))
        mu = jnp.mean(r, axis=-1, keepdims=True)
        c = r - mu
        var = jnp.mean(c * c, axis=-1, keepdims=True)
        o_rows_set(rs, slice(None), c * lax.rsqrt(var + LN_EPS) * g_ref[...] + b_ref[...])


def _pool_prompt_kernel(x_ref, win_ref, wgrp_ref, scale_ref, wout_ref, g_ref, b_ref,
                        o_ref, state_ref, u_buf, h_buf):
    t = pl.program_id(1)

    @pl.when(t == 0)
    def _():
        u_buf[0:HALO, :] = jnp.zeros((HALO, D_MODEL), F32)

    xb = x_ref[0].astype(BF16)
    row = lax.broadcasted_iota(jnp.int32, (TM, 1), 0) + t * TM
    for g, w in enumerate(POOL_WINDOWS):
        cols = slice(g * POOL_GROUP, (g + 1) * POOL_GROUP)
        u_buf[HALO:, cols] = jnp.dot(xb, win_ref[:, cols], preferred_element_type=F32)
        ext = u_buf[:, cols]
        s = ext
        shift = 1
        while shift < w:
            s = s + pltpu.roll(s, shift, 0)
            shift *= 2
        inv_cnt = 1.0 / jnp.minimum(w, row + 1).astype(F32)
        d = s[HALO:, :] * inv_cnt - ext[HALO:, :]
        d = jnp.dot(d.astype(BF16), wgrp_ref[g], preferred_element_type=F32) * scale_ref[:, cols]
        gate = jnp.dot(xb, win_ref[:, D_MODEL + g * POOL_GROUP:D_MODEL + (g + 1) * POOL_GROUP],
                       preferred_element_type=F32)
        h_buf[:, cols] = (d * _silu(gate)).astype(BF16)

    @pl.when(t == pl.num_programs(1) - 1)
    def _():
        state_ref[0] = u_buf[TM:TM + HALO, :]

    u_buf[0:HALO, :] = u_buf[TM:TM + HALO, :]

    def set_rows(rs, cs, v):
        o_ref[0, rs, cs] = v

    _outproj_ln(h_buf, lambda rs, cs: x_ref[0, rs, cs], wout_ref, g_ref, b_ref,
                set_rows, lambda rs, cs: o_ref[0, rs, cs], TM)


def _pool_prompt(x, params):
    B, S, D = x.shape
    return pl.pallas_call(
        _pool_prompt_kernel,
        grid=(B, S // TM),
        in_specs=[pl.BlockSpec((1, TM, D), lambda bi, t: (bi, t, 0))] + [spec for _, spec in params],
        out_specs=[
            pl.BlockSpec((1, TM, D), lambda bi, t: (bi, t, 0)),
            pl.BlockSpec((1, HALO, D), lambda bi, t: (bi, 0, 0)),
        ],
        out_shape=[jax.ShapeDtypeStruct((B, S, D), F32),
                   jax.ShapeDtypeStruct((B, HALO, D), F32)],
        scratch_shapes=[pltpu.VMEM((HALO + TM, D), F32), pltpu.VMEM((TM, D), BF16)],
        compiler_params=_params(2),
        name="pool_prompt",
    )(x, *[a for a, _ in params])


def _pool_sample_kernel(x_ref, st_ref, win_ref, wgrp_ref, scale_ref, wout_ref, g_ref, b_ref,
                        o_ref, nst_ref, h_buf):
    rows = x_ref.shape[0]
    xb = x_ref[...].astype(BF16)
    for g, w in enumerate(POOL_WINDOWS):
        cols = slice(g * POOL_GROUP, (g + 1) * POOL_GROUP)
        u = jnp.dot(xb, win_ref[:, cols], preferred_element_type=F32)
        acc = u
        for j in range(1, w):
            acc = acc + st_ref[POOL_STATE - j, :, cols]
        d = acc * (1.0 / min(w, PAST_LEN + 1)) - u
        d = jnp.dot(d.astype(BF16), wgrp_ref[g], preferred_element_type=F32) * scale_ref[:, cols]
        gate = jnp.dot(xb, win_ref[:, D_MODEL + g * POOL_GROUP:D_MODEL + (g + 1) * POOL_GROUP],
                       preferred_element_type=F32)
        h_buf[:, cols] = (d * _silu(gate)).astype(BF16)
        for j in range(POOL_STATE - 1):
            nst_ref[j, :, cols] = st_ref[j + 1, :, cols]
        nst_ref[POOL_STATE - 1, :, cols] = u

    def set_rows(rs, cs, v):
        o_ref[rs, cs] = v

    _outproj_ln(h_buf, lambda rs, cs: x_ref[rs, cs], wout_ref, g_ref, b_ref,
                set_rows, lambda rs, cs: o_ref[rs, cs], rows)


def _pool_sample(x, st, params):
    R, D = x.shape
    return pl.pallas_call(
        _pool_sample_kernel,
        grid=(1,),
        in_specs=[_resident(x.shape), _resident(st.shape)] + [spec for _, spec in params],
        out_specs=[pl.BlockSpec((R, D), lambda i: (0, 0)),
                   pl.BlockSpec(st.shape, lambda i: (0, 0, 0))],
        out_shape=[jax.ShapeDtypeStruct((R, D), F32), jax.ShapeDtypeStruct(st.shape, F32)],
        scratch_shapes=[pltpu.VMEM((R, D), BF16)],
        compiler_params=_params(1),
        name="pool_sample",
    )(x, st, *[a for a, _ in params])


def _project_kv(xb, cos, sa, sb, wkv_ref):
    kv = jnp.dot(xb, wkv_ref[...], preferred_element_type=F32)
    k_slabs = [_rope(kv[:, j * LANES:(j + 1) * LANES], cos, sa, sb) for j in range(KV_DIM // LANES)]
    return k_slabs, kv[:, KV_DIM:]


def _kv_prompt_kernel(x_ref, cos_ref, sa_ref, sb_ref, wkv_ref, knew_ref, vnew_ref, kdup_ref, vt_ref):
    k_slabs, v = _project_kv(x_ref[0].astype(BF16), cos_ref[...], sa_ref[...], sb_ref[...], wkv_ref)
    low = lax.broadcasted_iota(jnp.int32, (TK, LANES), 1) < HEAD_DIM
    for j, k in enumerate(k_slabs):
        swapped = pltpu.roll(k, HEAD_DIM, 1)
        kdup_ref[0, :, (2 * j) * LANES:(2 * j + 1) * LANES] = jnp.where(low, k, swapped).astype(BF16)
        kdup_ref[0, :, (2 * j + 1) * LANES:(2 * j + 2) * LANES] = jnp.where(low, swapped, k).astype(BF16)
    for i in range(TK // WINDOW):
        vt_ref[0, i] = v[i * WINDOW:(i + 1) * WINDOW, :].T.astype(BF16)

    @pl.when(pl.program_id(1) == pl.num_programs(1) - 1)
    def _():
        for j, k in enumerate(k_slabs):
            knew_ref[0, :, j * LANES:(j + 1) * LANES] = k[TK - WINDOW:, :]
        vnew_ref[0] = v[TK - WINDOW:, :]


def _kv_prompt(x, tables, wkv):
    B, S, D = x.shape
    tab = pl.BlockSpec((TK, LANES), lambda bi, t: (t, 0))
    last = pl.BlockSpec((1, WINDOW, KV_DIM), lambda bi, t: (bi, 0, 0))
    return pl.pallas_call(
        _kv_prompt_kernel,
        grid=(B, S // TK),
        in_specs=[pl.BlockSpec((1, TK, D), lambda bi, t: (bi, t, 0)), tab, tab, tab, _resident(wkv.shape)],
        out_specs=[last, last,
                   pl.BlockSpec((1, TK, N_KV_HEADS * LANES), lambda bi, t: (bi, t, 0)),
                   pl.BlockSpec((1, TK // WINDOW, KV_DIM, WINDOW), lambda bi, t: (bi, t, 0, 0))],
        out_shape=[jax.ShapeDtypeStruct((B, WINDOW, KV_DIM), F32), jax.ShapeDtypeStruct((B, WINDOW, KV_DIM), F32),
                   jax.ShapeDtypeStruct((B, S, N_KV_HEADS * LANES), BF16),
                   jax.ShapeDtypeStruct((B, S // WINDOW, KV_DIM, WINDOW), BF16)],
        compiler_params=_params(2),
        name="kv_prompt",
    )(x, *tables, wkv)


def _kv_sample_kernel(x_ref, cos_ref, sa_ref, sb_ref, wkv_ref, k_ref, v_ref):
    k_slabs, v = _project_kv(x_ref[...].astype(BF16), cos_ref[...], sa_ref[...], sb_ref[...], wkv_ref)
    for j, k in enumerate(k_slabs):
        k_ref[:, j * LANES:(j + 1) * LANES] = k
    v_ref[...] = v


def _kv_sample(x, tables, wkv):
    R, D = x.shape
    out = pl.BlockSpec((R, KV_DIM), lambda i: (0, 0))
    return pl.pallas_call(
        _kv_sample_kernel,
        grid=(1,),
        in_specs=[_resident(x.shape)] + [_resident(t.shape) for t in tables] + [_resident(wkv.shape)],
        out_specs=[out, out],
        out_shape=[jax.ShapeDtypeStruct((R, KV_DIM), F32), jax.ShapeDtypeStruct((R, KV_DIM), F32)],
        compiler_params=_params(1),
        name="kv_sample",
    )(x, *tables, wkv)


def _attn_prompt_kernel(x_ref, cos_ref, sa_ref, sb_ref, kdup_ref, vt_ref, win_ref, sink_ref, wout_ref,
                        g_ref, b_ref, o_ref, xb_buf, q_buf, h_buf):
    t = pl.program_id(1)
    xb_buf[...] = x_ref[0].astype(BF16)
    for n in range(N_CHUNKS):
        q = jnp.dot(xb_buf[...], win_ref[:, n * COL_CHUNK:(n + 1) * COL_CHUNK], preferred_element_type=F32)
        for j in range(COL_CHUNK // LANES):
            qj = _rope(q[:, j * LANES:(j + 1) * LANES], cos_ref[...], sa_ref[...], sb_ref[...])
            c0 = n * COL_CHUNK + j * LANES
            q_buf[:, c0:c0 + LANES] = qj.astype(BF16)

    key = lax.broadcasted_iota(jnp.int32, (2 * WINDOW, WINDOW), 0)
    qry = lax.broadcasted_iota(jnp.int32, (2 * WINDOW, WINDOW), 1)
    band = (key > qry) & (key <= qry + WINDOW)
    low_half = lax.broadcasted_iota(jnp.int32, (WINDOW, LANES), 1) < HEAD_DIM
    ones_rows = jnp.ones((SUM_ROWS, 2 * WINDOW), BF16)

    def attend(qb, carry):
        blk = t * (TM // WINDOW) + qb
        prev_blk = jnp.maximum(blk - 1, 0)
        prev = pl.multiple_of(prev_blk * WINDOW, WINDOW)
        cur = pl.multiple_of(blk * WINDOW, WINDOW)
        vis = band & ((key >= WINDOW) | (blk > 0))
        rows = pl.ds(pl.multiple_of(qb * WINDOW, WINDOW), WINDOW)
        for kv in range(N_KV_HEADS):
            ks = slice(kv * LANES, (kv + 1) * LANES)
            vs = slice(kv * HEAD_DIM, (kv + 1) * HEAD_DIM)
            k2 = jnp.concatenate([kdup_ref[0, pl.ds(prev, WINDOW), ks], kdup_ref[0, pl.ds(cur, WINDOW), ks]], axis=0)
            v_aug = jnp.concatenate([vt_ref[0, prev_blk, vs, :], vt_ref[0, blk, vs, :]], axis=1)
            v_aug = jnp.concatenate([v_aug, ones_rows], axis=0)
            heads = [(kv * (GROUP // 2) + pair, par) for pair in range(GROUP // 2) for par in range(2)]
            q_all = []
            for slab, par in heads:
                q_slab = q_buf[rows, slab * LANES:(slab + 1) * LANES]
                q_all.append(jnp.where(low_half == (par == 0), q_slab, jnp.zeros_like(q_slab)))
            s_t = lax.dot_general(k2, jnp.concatenate(q_all, axis=0), (((1,), (1,)), ((), ())),
                                  preferred_element_type=F32)
            p_t, sink_terms = [], []
            for i, (slab, par) in enumerate(heads):
                s = jnp.where(vis, s_t[:, i * WINDOW:(i + 1) * WINDOW], NEG)
                sink = sink_ref[2 * slab + par]
                m = jnp.maximum(jnp.max(s, axis=0, keepdims=True), sink)
                p_t.append(jnp.exp(s - m).astype(BF16))
                sink_terms.append(jnp.exp(sink - m))
            o_t = jnp.dot(v_aug, jnp.concatenate(p_t, axis=1), preferred_element_type=F32)
            g0 = D_MODEL + kv * GROUP * HEAD_DIM
            gate = jnp.dot(xb_buf[rows, :], win_ref[:, g0:g0 + GROUP * HEAD_DIM], preferred_element_type=F32)
            for pair in range(GROUP // 2):
                both = []
                for par in range(2):
                    i = 2 * pair + par
                    cs = slice(i * WINDOW, (i + 1) * WINDOW)
                    inv = 1.0 / (o_t[HEAD_DIM:HEAD_DIM + 1, cs] + sink_terms[i])
                    both.append(o_t[:HEAD_DIM, cs] * inv)
                slab = kv * (GROUP // 2) + pair
                attn = jnp.concatenate(both, axis=0).T
                h_buf[rows, slab * LANES:(slab + 1) * LANES] = (
                    attn * _silu(gate[:, pair * LANES:(pair + 1) * LANES])).astype(BF16)
        return carry

    lax.fori_loop(0, TM // WINDOW, attend, 0)

    def set_rows(rs, cs, v):
        o_ref[0, rs, cs] = v

    _outproj_ln(h_buf, lambda rs, cs: x_ref[0, rs, cs], wout_ref, g_ref, b_ref,
                set_rows, lambda rs, cs: o_ref[0, rs, cs], TM)


def _attn_prompt(x, tables, kdup, vt, win, sinks, wout, g, b):
    B, S, D = x.shape
    tab = pl.BlockSpec((TM, LANES), lambda bi, t: (t, 0))

    def per_batch(a):
        return pl.BlockSpec((1,) + a.shape[1:], lambda bi, t: (bi,) + (0,) * (a.ndim - 1),
                            pipeline_mode=pl.Buffered(1))

    return pl.pallas_call(
        _attn_prompt_kernel,
        grid=(B, S // TM),
        in_specs=[
            pl.BlockSpec((1, TM, D), lambda bi, t: (bi, t, 0)), tab, tab, tab, per_batch(kdup), per_batch(vt),
            win[1], pl.BlockSpec(memory_space=pltpu.SMEM), wout[1], g[1], b[1],
        ],
        out_specs=pl.BlockSpec((1, TM, D), lambda bi, t: (bi, t, 0)),
        out_shape=jax.ShapeDtypeStruct((B, S, D), F32),
        scratch_shapes=[pltpu.VMEM((TM, D), BF16), pltpu.VMEM((TM, D), BF16), pltpu.VMEM((TM, D), BF16)],
        compiler_params=_params(2),
        name="attn_prompt",
    )(x, *tables, kdup, vt, win[0], sinks, wout[0], g[0], b[0])


def _qgate_sample_kernel(x_ref, cos_ref, sa_ref, sb_ref, win_ref, q_ref, gate_ref):
    xb = x_ref[...].astype(BF16)
    for n in range(N_CHUNKS):
        q = jnp.dot(xb, win_ref[:, n * COL_CHUNK:(n + 1) * COL_CHUNK], preferred_element_type=F32)
        for j in range(COL_CHUNK // LANES):
            c0 = n * COL_CHUNK + j * LANES
            q_ref[:, c0:c0 + LANES] = _rope(q[:, j * LANES:(j + 1) * LANES],
                                            cos_ref[...], sa_ref[...], sb_ref[...])
        gate_ref[:, n * COL_CHUNK:(n + 1) * COL_CHUNK] = jnp.dot(
            xb, win_ref[:, D_MODEL + n * COL_CHUNK:D_MODEL + (n + 1) * COL_CHUNK],
            preferred_element_type=F32)


def _qgate_sample(x, tables, win):
    R, D = x.shape
    return pl.pallas_call(
        _qgate_sample_kernel,
        grid=(1,),
        in_specs=[_resident(x.shape)] + [_resident(t.shape) for t in tables] + [win[1]],
        out_specs=[pl.BlockSpec((R, D), lambda i: (0, 0)), pl.BlockSpec((R, D), lambda i: (0, 0))],
        out_shape=[jax.ShapeDtypeStruct((R, D), F32), jax.ShapeDtypeStruct((R, D), F32)],
        compiler_params=_params(1),
        name="qgate_sample",
    )(x, *tables, win[0])


def _attn_sample_kernel(q_ref, kn_ref, vn_ref, ck_ref, cv_ref, sink_ref, o_ref, nk_ref, nv_ref):
    nb = q_ref.shape[0]
    head_of_lane = lax.broadcasted_iota(jnp.int32, (N_HEADS, D_MODEL), 1) // HEAD_DIM
    own_head = head_of_lane == lax.broadcasted_iota(jnp.int32, (N_HEADS, D_MODEL), 0)
    low_half = lax.broadcasted_iota(jnp.int32, (N_HEADS, LANES), 1) < HEAD_DIM
    last_row = lax.broadcasted_iota(jnp.int32, (WINDOW, KV_DIM), 0) == WINDOW - 1
    sink = sink_ref[...]
    heads_per_slab = LANES // HEAD_DIM
    slabs_per_group = GROUP // heads_per_slab

    def body(i, carry):
        newk = jnp.where(last_row, kn_ref[pl.ds(i, 1), :], pltpu.roll(ck_ref[i], WINDOW - 1, 0))
        newv = jnp.where(last_row, vn_ref[pl.ds(i, 1), :], pltpu.roll(cv_ref[i], WINDOW - 1, 0))
        nk_ref[i] = newk
        nv_ref[i] = newv
        qh = jnp.where(own_head, jnp.broadcast_to(q_ref[pl.ds(i, 1), :], (N_HEADS, D_MODEL)), 0.0)
        folded = []
        for kv in range(N_KV_HEADS):
            w = qh[:, kv * GROUP * HEAD_DIM:kv * GROUP * HEAD_DIM + LANES]
            for sl in range(1, slabs_per_group):
                c0 = kv * GROUP * HEAD_DIM + sl * LANES
                w = w + qh[:, c0:c0 + LANES]
            folded.append(w + pltpu.roll(w, HEAD_DIM, 1))
        qg = jnp.concatenate([jnp.where(low_half, folded[2 * j], folded[2 * j + 1])
                              for j in range(N_KV_HEADS // 2)], axis=1)
        s = lax.dot_general(qg.astype(BF16), newk.astype(BF16), (((1,), (1,)), ((), ())),
                            preferred_element_type=F32)
        m = jnp.maximum(jnp.max(s, axis=-1, keepdims=True), sink)
        p = jnp.exp(s - m)
        denom = jnp.sum(p, axis=-1, keepdims=True) + jnp.exp(sink - m)
        og = jnp.dot(p.astype(BF16), newv.astype(BF16), preferred_element_type=F32) / denom
        slabs = []
        for kv in range(N_KV_HEADS):
            xs = og[:, (kv // 2) * LANES:(kv // 2 + 1) * LANES]
            rolled = pltpu.roll(xs, HEAD_DIM, 1)
            both = jnp.where(low_half, xs, rolled) if kv % 2 == 0 else jnp.where(low_half, rolled, xs)
            slabs.extend([both] * slabs_per_group)
        full = jnp.concatenate(slabs, axis=1)
        o_ref[pl.ds(i, 1), :] = jnp.sum(jnp.where(own_head, full, 0.0), axis=0, keepdims=True)
        return carry

    lax.fori_loop(0, nb, body, 0)


def _attn_sample(q, kn, vn, ck, cv, sinks_col):
    R, D = q.shape
    nb = SAMPLE_ATTN_BATCH
    row2 = lambda w: pl.BlockSpec((nb, w), lambda i: (i, 0))
    cache = pl.BlockSpec((nb, WINDOW, KV_DIM), lambda i: (i, 0, 0))
    return pl.pallas_call(
        _attn_sample_kernel,
        grid=(R // nb,),
        in_specs=[row2(D), row2(KV_DIM), row2(KV_DIM), cache, cache, _resident(sinks_col.shape)],
        out_specs=[row2(D), cache, cache],
        out_shape=[jax.ShapeDtypeStruct((R, D), F32),
                   jax.ShapeDtypeStruct(ck.shape, F32), jax.ShapeDtypeStruct(cv.shape, F32)],
        compiler_params=_params(1),
        name="attn_sample",
    )(q, kn, vn, ck, cv, sinks_col)


def _gated_out_sample_kernel(x_ref, a_ref, gate_ref, wout_ref, g_ref, b_ref, o_ref, h_buf):
    rows = x_ref.shape[0]
    h_buf[...] = (a_ref[...] * _silu(gate_ref[...])).astype(BF16)

    def set_rows(rs, cs, v):
        o_ref[rs, cs] = v

    _outproj_ln(h_buf, lambda rs, cs: x_ref[rs, cs], wout_ref, g_ref, b_ref,
                set_rows, lambda rs, cs: o_ref[rs, cs], rows)


def _gated_out_sample(x, a, gate, wout, g, b):
    R, D = x.shape
    return pl.pallas_call(
        _gated_out_sample_kernel,
        grid=(1,),
        in_specs=[_resident(t.shape) for t in (x, a, gate)] + [wout[1], g[1], b[1]],
        out_specs=pl.BlockSpec((R, D), lambda i: (0, 0)),
        out_shape=jax.ShapeDtypeStruct((R, D), F32),
        scratch_shapes=[pltpu.VMEM((R, D), BF16)],
        compiler_params=_params(1),
        name="gated_out_sample",
    )(x, a, gate, wout[0], g[0], b[0])


def _rope_tables(pos):
    half = ROT_DIM // 2
    inv_freq = ROPE_THETA ** (-jnp.arange(0, ROT_DIM, 2, dtype=F32) / ROT_DIM)
    ang = pos.astype(F32)[:, None] * inv_freq[None, :]
    cos, sin = jnp.cos(ang), jnp.sin(ang)
    n = pos.shape[0]
    rest = jnp.zeros((n, HEAD_DIM - ROT_DIM), F32)
    zero = jnp.zeros((n, half), F32)
    cos_h = jnp.concatenate([cos, cos, rest + 1.0], axis=1)
    sa_h = jnp.concatenate([-sin, zero, rest], axis=1)
    sb_h = jnp.concatenate([zero, sin, rest], axis=1)
    rep = LANES // HEAD_DIM
    return tuple(jnp.tile(a, (1, rep)) for a in (cos_h, sa_h, sb_h))


def kernel(x_prompt, x_sample, state_pool, cache_k, cache_v, w_in_a, w_grp_a, scale_a, w_out_a,
           w_kv, w_in_b, sinks_b, w_out_b, ln_g, ln_b):
    B, S, D = x_prompt.shape
    R = x_sample.shape[0]
    xp = x_prompt
    xs = x_sample.reshape(R, D)
    tab_p = _rope_tables(jnp.arange(S, dtype=jnp.int32))
    tab_s = _rope_tables(jnp.full((R,), PAST_LEN, jnp.int32))
    qtab_p = tuple(a * SM_SCALE for a in tab_p)
    qtab_s = tuple(a * SM_SCALE for a in tab_s)
    w_in_a, w_grp_a, w_out_a, w_in_b, w_out_b = (
        w.astype(BF16) for w in (w_in_a, w_grp_a, w_out_a, w_in_b, w_out_b))
    scale_a, ln_g, ln_b = (p.reshape(p.shape[0], 1, D) for p in (scale_a, ln_g, ln_b))
    pool_p, pool_s = [], []
    for i in range(N_A_LAYERS):
        params = [_layer(p, i) for p in (w_in_a, w_grp_a, scale_a, w_out_a, ln_g, ln_b)]
        xp, sp = _pool_prompt(xp, params)
        xs, ss = _pool_sample(xs, state_pool[i].transpose(1, 0, 2), params)
        pool_p.append(sp[:, HALO - POOL_STATE:])
        pool_s.append(ss.transpose(1, 0, 2))
    wkv = w_kv.astype(BF16)
    new_k_p, new_v_p, kdup_p, vt_p = _kv_prompt(xp, tab_p, wkv)
    k_s, v_s = _kv_sample(xs, tab_s, wkv)
    ck = cache_k.reshape(R, WINDOW, KV_DIM)
    cv = cache_v.reshape(R, WINDOW, KV_DIM)
    for j in range(DEPTH - N_A_LAYERS):
        i = N_A_LAYERS + j
        win, wout, g, b = _layer(w_in_b, j), _layer(w_out_b, j), _layer(ln_g, i), _layer(ln_b, i)
        xp = _attn_prompt(xp, qtab_p, kdup_p, vt_p, win, sinks_b[j], wout, g, b)
        q_s, gate_s = _qgate_sample(xs, qtab_s, win)
        a_s, nk, nv = _attn_sample(q_s, k_s, v_s, ck, cv, sinks_b[j][:, None])
        xs = _gated_out_sample(xs, a_s, gate_s, wout, g, b)
    kv4 = (N_KV_HEADS, HEAD_DIM)
    return (xp, xs.reshape(R, 1, D), jnp.stack(pool_p, axis=0), jnp.stack(pool_s, axis=0),
            new_k_p.reshape(B, WINDOW, *kv4), new_v_p.reshape(B, WINDOW, *kv4),
            nk.reshape(R, WINDOW, *kv4), nv.reshape(R, WINDOW, *kv4))
```

```python
import functools

import jax
import jax.numpy as jnp
from jax import lax
from jax.experimental import pallas as pl
from jax.experimental.pallas import tpu as pltpu

F32 = jnp.float32
BF16 = jnp.bfloat16

D_MODEL = 2048
DEPTH = 4
PAST_LEN = 16384
N_A_LAYERS = DEPTH // 2
POOL_WINDOWS = (2, 4, 8, 16)
POOL_GROUP = D_MODEL // len(POOL_WINDOWS)
POOL_STATE = max(POOL_WINDOWS) - 1
HEAD_DIM = 64
N_HEADS = D_MODEL // HEAD_DIM
N_KV_HEADS = N_HEADS // 8
GROUP = N_HEADS // N_KV_HEADS
KV_DIM = N_KV_HEADS * HEAD_DIM
WINDOW = 128
ROT_DIM = HEAD_DIM // 4
ROPE_THETA = 500000.0
ALPHA = (2 * DEPTH) ** 0.25
LN_EPS = 1e-5
NEG = -1e30
SM_SCALE = HEAD_DIM ** -0.5

LANES = 128
HALO = 16
COL_CHUNK = 512
N_CHUNKS = D_MODEL // COL_CHUNK
TM = 256
TK = 512
LN_ROWS = 32
SAMPLE_ATTN_BATCH = 8
SUM_ROWS = 16
VMEM_LIMIT_BYTES = 56 * 1024 * 1024


def _params(n_axes):
    return pltpu.CompilerParams(dimension_semantics=("arbitrary",) * n_axes,
                                vmem_limit_bytes=VMEM_LIMIT_BYTES)


def _resident(shape):
    zeros = (0,) * len(shape)
    return pl.BlockSpec(shape, lambda *_: zeros, pipeline_mode=pl.Buffered(1))


def _layer(stacked, i):
    zeros = (0,) * (stacked.ndim - 1)
    return stacked, pl.BlockSpec((None,) + stacked.shape[1:], lambda *_: (i,) + zeros,
                                 pipeline_mode=pl.Buffered(1))


def _silu(g):
    return g / (1.0 + jnp.exp(-g))


def _rope(x, cos, sa, sb):
    return x * cos + pltpu.roll(x, LANES - ROT_DIM // 2, 1) * sa + pltpu.roll(x, ROT_DIM // 2, 1) * sb


def _outproj_ln(h_ref, x_rows, wout_ref, g_ref, b_ref, o_rows_set, o_rows_get, rows):
    for n in range(N_CHUNKS):
        cols = slice(n * COL_CHUNK, (n + 1) * COL_CHUNK)
        y = jnp.dot(h_ref[...], wout_ref[:, cols], preferred_element_type=F32)
        o_rows_set(slice(0, rows), cols, ALPHA * x_rows(slice(0, rows), cols) + y)
    step = min(LN_ROWS, rows)
    for r0 in range(0, rows, step):
        rs = slice(r0, r0 + step)
        r = o_rows_get(rs, slice(None))
        mu = jnp.mean(r, axis=-1, keepdims=True)
        c = r - mu
        var = jnp.mean(c * c, axis=-1, keepdims=True)
        o_rows_set(rs, slice(None), c * lax.rsqrt(var + LN_EPS) * g_ref[...] + b_ref[...])


def _pool_prompt_kernel(x_ref, win_ref, wgrp_ref, scale_ref, wout_ref, g_ref, b_ref,
                        o_ref, state_ref, u_buf, h_buf):
    t = pl.program_id(1)

    @pl.when(t == 0)
    def _():
        u_buf[0:HALO, :] = jnp.zeros((HALO, D_MODEL), F32)

    xb = x_ref[0].astype(BF16)
    row = lax.broadcasted_iota(jnp.int32, (TM, 1), 0) + t * TM
    for g, w in enumerate(POOL_WINDOWS):
        cols = slice(g * POOL_GROUP, (g + 1) * POOL_GROUP)
        u_buf[HALO:, cols] = jnp.dot(xb, win_ref[:, cols], preferred_element_type=F32)
        ext = u_buf[:, cols]
        s = ext
        shift = 1
        while shift < w:
            s = s + pltpu.roll(s, shift, 0)
            shift *= 2
        inv_cnt = 1.0 / jnp.minimum(w, row + 1).astype(F32)
        d = s[HALO:, :] * inv_cnt - ext[HALO:, :]
        d = jnp.dot(d.astype(BF16), wgrp_ref[g], preferred_element_type=F32) * scale_ref[:, cols]
        gate = jnp.dot(xb, win_ref[:, D_MODEL + g * POOL_GROUP:D_MODEL + (g + 1) * POOL_GROUP],
                       preferred_element_type=F32)
        h_buf[:, cols] = (d * _silu(gate)).astype(BF16)

    @pl.when(t == pl.num_programs(1) - 1)
    def _():
        state_ref[0] = u_buf[TM:TM + HALO, :]

    u_buf[0:HALO, :] = u_buf[TM:TM + HALO, :]

    def set_rows(rs, cs, v):
        o_ref[0, rs, cs] = v

    _outproj_ln(h_buf, lambda rs, cs: x_ref[0, rs, cs], wout_ref, g_ref, b_ref,
                set_rows, lambda rs, cs: o_ref[0, rs, cs], TM)


def _pool_prompt(x, params):
    B, S, D = x.shape
    return pl.pallas_call(
        _pool_prompt_kernel,
        grid=(B, S // TM),
        in_specs=[pl.BlockSpec((1, TM, D), lambda bi, t: (bi, t, 0))] + [spec for _, spec in params],
        out_specs=[
            pl.BlockSpec((1, TM, D), lambda bi, t: (bi, t, 0)),
            pl.BlockSpec((1, HALO, D), lambda bi, t: (bi, 0, 0)),
        ],
        out_shape=[jax.ShapeDtypeStruct((B, S, D), F32),
                   jax.ShapeDtypeStruct((B, HALO, D), F32)],
        scratch_shapes=[pltpu.VMEM((HALO + TM, D), F32), pltpu.VMEM((TM, D), BF16)],
        compiler_params=_params(2),
        name="pool_prompt",
    )(x, *[a for a, _ in params])


def _pool_sample_kernel(x_ref, st_ref, win_ref, wgrp_ref, scale_ref, wout_ref, g_ref, b_ref,
                        o_ref, nst_ref, h_buf):
    rows = x_ref.shape[0]
    xb = x_ref[...].astype(BF16)
    for g, w in enumerate(POOL_WINDOWS):
        cols = slice(g * POOL_GROUP, (g + 1) * POOL_GROUP)
        u = jnp.dot(xb, win_ref[:, cols], preferred_element_type=F32)
        acc = u
        for j in range(1, w):
            acc = acc + st_ref[POOL_STATE - j, :, cols]
        d = acc * (1.0 / min(w, PAST_LEN + 1)) - u
        d = jnp.dot(d.astype(BF16), wgrp_ref[g], preferred_element_type=F32) * scale_ref[:, cols]
        gate = jnp.dot(xb, win_ref[:, D_MODEL + g * POOL_GROUP:D_MODEL + (g + 1) * POOL_GROUP],
                       preferred_element_type=F32)
        h_buf[:, cols] = (d * _silu(gate)).astype(BF16)
        for j in range(POOL_STATE - 1):
            nst_ref[j, :, cols] = st_ref[j + 1, :, cols]
        nst_ref[POOL_STATE - 1, :, cols] = u

    def set_rows(rs, cs, v):
        o_ref[rs, cs] = v

    _outproj_ln(h_buf, lambda rs, cs: x_ref[rs, cs], wout_ref, g_ref, b_ref,
                set_rows, lambda rs, cs: o_ref[rs, cs], rows)


def _pool_sample(x, st, params):
    R, D = x.shape
    return pl.pallas_call(
        _pool_sample_kernel,
        grid=(1,),
        in_specs=[_resident(x.shape), _resident(st.shape)] + [spec for _, spec in params],
        out_specs=[pl.BlockSpec((R, D), lambda i: (0, 0)),
                   pl.BlockSpec(st.shape, lambda i: (0, 0, 0))],
        out_shape=[jax.ShapeDtypeStruct((R, D), F32), jax.ShapeDtypeStruct(st.shape, F32)],
        scratch_shapes=[pltpu.VMEM((R, D), BF16)],
        compiler_params=_params(1),
        name="pool_sample",
    )(x, st, *[a for a, _ in params])


def _project_kv(xb, cos, sa, sb, wkv_ref):
    kv = jnp.dot(xb, wkv_ref[...], preferred_element_type=F32)
    k_slabs = [_rope(kv[:, j * LANES:(j + 1) * LANES], cos, sa, sb) for j in range(KV_DIM // LANES)]
    return k_slabs, kv[:, KV_DIM:]


def _kv_prompt_kernel(x_ref, cos_ref, sa_ref, sb_ref, wkv_ref, knew_ref, vnew_ref, kdup_ref, vt_ref):
    k_slabs, v = _project_kv(x_ref[0].astype(BF16), cos_ref[...], sa_ref[...], sb_ref[...], wkv_ref)
    low = lax.broadcasted_iota(jnp.int32, (TK, LANES), 1) < HEAD_DIM
    for j, k in enumerate(k_slabs):
        swapped = pltpu.roll(k, HEAD_DIM, 1)
        kdup_ref[0, :, (2 * j) * LANES:(2 * j + 1) * LANES] = jnp.where(low, k, swapped).astype(BF16)
        kdup_ref[0, :, (2 * j + 1) * LANES:(2 * j + 2) * LANES] = jnp.where(low, swapped, k).astype(BF16)
    for i in range(TK // WINDOW):
        vt_ref[0, i] = v[i * WINDOW:(i + 1) * WINDOW, :].T.astype(BF16)

    @pl.when(pl.program_id(1) == pl.num_programs(1) - 1)
    def _():
        for j, k in enumerate(k_slabs):
            knew_ref[0, :, j * LANES:(j + 1) * LANES] = k[TK - WINDOW:, :]
        vnew_ref[0] = v[TK - WINDOW:, :]


def _kv_prompt(x, tables, wkv):
    B, S, D = x.shape
    tab = pl.BlockSpec((TK, LANES), lambda bi, t: (t, 0))
    last = pl.BlockSpec((1, WINDOW, KV_DIM), lambda bi, t: (bi, 0, 0))
    return pl.pallas_call(
        _kv_prompt_kernel,
        grid=(B, S // TK),
        in_specs=[pl.BlockSpec((1, TK, D), lambda bi, t: (bi, t, 0)), tab, tab, tab, _resident(wkv.shape)],
        out_specs=[last, last,
                   pl.BlockSpec((1, TK, N_KV_HEADS * LANES), lambda bi, t: (bi, t, 0)),
                   pl.BlockSpec((1, TK // WINDOW, KV_DIM, WINDOW), lambda bi, t: (bi, t, 0, 0))],
        out_shape=[jax.ShapeDtypeStruct((B, WINDOW, KV_DIM), F32), jax.ShapeDtypeStruct((B, WINDOW, KV_DIM), F32),
                   jax.ShapeDtypeStruct((B, S, N_KV_HEADS * LANES), BF16),
                   jax.ShapeDtypeStruct((B, S // WINDOW, KV_DIM, WINDOW), BF16)],
        compiler_params=_params(2),
        name="kv_prompt",
    )(x, *tables, wkv)


def _kv_sample_kernel(x_ref, cos_ref, sa_ref, sb_ref, wkv_ref, k_ref, v_ref):
    k_slabs, v = _project_kv(x_ref[...].astype(BF16), cos_ref[...], sa_ref[...], sb_ref[...], wkv_ref)
    for j, k in enumerate(k_slabs):
        k_ref[:, j * LANES:(j + 1) * LANES] = k
    v_ref[...] = v


def _kv_sample(x, tables, wkv):
    R, D = x.shape
    out = pl.BlockSpec((R, KV_DIM), lambda i: (0, 0))
    return pl.pallas_call(
        _kv_sample_kernel,
        grid=(1,),
        in_specs=[_resident(x.shape)] + [_resident(t.shape) for t in tables] + [_resident(wkv.shape)],
        out_specs=[out, out],
        out_shape=[jax.ShapeDtypeStruct((R, KV_DIM), F32), jax.ShapeDtypeStruct((R, KV_DIM), F32)],
        compiler_params=_params(1),
        name="kv_sample",
    )(x, *tables, wkv)


def _attn_prompt_kernel(x_ref, cos_ref, sa_ref, sb_ref, kdup_ref, vt_ref, win_ref, sink_ref, wout_ref,
                        g_ref, b_ref, o_ref, xb_buf, q_buf, h_buf, g_buf):
    t = pl.program_id(1)
    group_cols = GROUP * HEAD_DIM
    xb_buf[...] = x_ref[0].astype(BF16)

    def project_q(kv):
        q = jnp.dot(xb_buf[...], win_ref[:, kv * group_cols:(kv + 1) * group_cols], preferred_element_type=F32)
        for j in range(group_cols // LANES):
            qj = _rope(q[:, j * LANES:(j + 1) * LANES], cos_ref[...], sa_ref[...], sb_ref[...])
            c0 = kv * group_cols + j * LANES
            q_buf[:, c0:c0 + LANES] = qj.astype(BF16)

    def project_gate(kv):
        cols = slice(kv * group_cols, (kv + 1) * group_cols)
        gate = jnp.dot(xb_buf[...], win_ref[:, D_MODEL + kv * group_cols:D_MODEL + (kv + 1) * group_cols],
                       preferred_element_type=F32)
        g_buf[:, cols] = _silu(gate)

    key = lax.broadcasted_iota(jnp.int32, (2 * WINDOW, WINDOW), 0)
    qry = lax.broadcasted_iota(jnp.int32, (2 * WINDOW, WINDOW), 1)
    band = (key > qry) & (key <= qry + WINDOW)
    low_half = lax.broadcasted_iota(jnp.int32, (WINDOW, LANES), 1) < HEAD_DIM
    ones_rows = jnp.ones((SUM_ROWS, 2 * WINDOW), BF16)

    def block_ids(qb):
        blk = t * (TM // WINDOW) + qb
        return blk, jnp.maximum(blk - 1, 0)

    def group_heads(kv):
        return [(kv * (GROUP // 2) + pair, par) for pair in range(GROUP // 2) for par in range(2)]

    def scores(qb, kv):
        blk, prev_blk = block_ids(qb)
        prev = pl.multiple_of(prev_blk * WINDOW, WINDOW)
        cur = pl.multiple_of(blk * WINDOW, WINDOW)
        rows = slice(qb * WINDOW, (qb + 1) * WINDOW)
        ks = slice(kv * LANES, (kv + 1) * LANES)
        k2 = jnp.concatenate([kdup_ref[0, pl.ds(prev, WINDOW), ks], kdup_ref[0, pl.ds(cur, WINDOW), ks]], axis=0)
        q_all = []
        for slab, par in group_heads(kv):
            q_slab = q_buf[rows, slab * LANES:(slab + 1) * LANES]
            q_all.append(jnp.where(low_half == (par == 0), q_slab, jnp.zeros_like(q_slab)))
        return lax.dot_general(k2, jnp.concatenate(q_all, axis=0), (((1,), (1,)), ((), ())),
                               preferred_element_type=F32)

    def finish(qb, kv, s_t):
        blk, prev_blk = block_ids(qb)
        vis = band & ((key >= WINDOW) | (blk > 0))
        rows = slice(qb * WINDOW, (qb + 1) * WINDOW)
        vs = slice(kv * HEAD_DIM, (kv + 1) * HEAD_DIM)
        v_aug = jnp.concatenate([vt_ref[0, prev_blk, vs, :], vt_ref[0, blk, vs, :]], axis=1)
        v_aug = jnp.concatenate([v_aug, ones_rows], axis=0)
        p_t, sink_terms = [], []
        for i, (slab, par) in enumerate(group_heads(kv)):
            s = jnp.where(vis, s_t[:, i * WINDOW:(i + 1) * WINDOW], NEG)
            sink = sink_ref[2 * slab + par]
            m = jnp.maximum(jnp.max(s, axis=0, keepdims=True), sink)
            p_t.append(jnp.exp(s - m).astype(BF16))
            sink_terms.append(jnp.exp(sink - m))
        o_t = jnp.dot(v_aug, jnp.concatenate(p_t, axis=1), preferred_element_type=F32)
        for pair in range(GROUP // 2):
            both = []
            for par in range(2):
                i = 2 * pair + par
                cs = slice(i * WINDOW, (i + 1) * WINDOW)
                inv = 1.0 / (o_t[HEAD_DIM:HEAD_DIM + 1, cs] + sink_terms[i])
                both.append(o_t[:HEAD_DIM, cs] * inv)
            slab = kv * (GROUP // 2) + pair
            attn = jnp.concatenate(both, axis=0).T
            cs = slice(slab * LANES, (slab + 1) * LANES)
            h_buf[rows, cs] = (attn * g_buf[rows, cs]).astype(BF16)

    project_q(0)
    for kv in range(N_KV_HEADS):
        s_t = [scores(qb, kv) for qb in range(TM // WINDOW)]
        if kv + 1 < N_KV_HEADS:
            project_q(kv + 1)
        project_gate(kv)
        for qb in range(TM // WINDOW):
            finish(qb, kv, s_t[qb])

    def set_rows(rs, cs, v):
        o_ref[0, rs, cs] = v

    _outproj_ln(h_buf, lambda rs, cs: x_ref[0, rs, cs], wout_ref, g_ref, b_ref,
                set_rows, lambda rs, cs: o_ref[0, rs, cs], TM)


def _attn_prompt(x, tables, kdup, vt, win, sinks, wout, g, b):
    B, S, D = x.shape
    tab = pl.BlockSpec((TM, LANES), lambda bi, t: (t, 0))

    def per_batch(a):
        return pl.BlockSpec((1,) + a.shape[1:], lambda bi, t: (bi,) + (0,) * (a.ndim - 1),
                            pipeline_mode=pl.Buffered(1))

    return pl.pallas_call(
        _attn_prompt_kernel,
        grid=(B, S // TM),
        in_specs=[
            pl.BlockSpec((1, TM, D), lambda bi, t: (bi, t, 0)), tab, tab, tab, per_batch(kdup), per_batch(vt),
            win[1], pl.BlockSpec(memory_space=pltpu.SMEM), wout[1], g[1], b[1],
        ],
        out_specs=pl.BlockSpec((1, TM, D), lambda bi, t: (bi, t, 0)),
        out_shape=jax.ShapeDtypeStruct((B, S, D), F32),
        scratch_shapes=[pltpu.VMEM((TM, D), BF16), pltpu.VMEM((TM, D), BF16), pltpu.VMEM((TM, D), BF16),
                        pltpu.VMEM((TM, D), F32)],
        compiler_params=_params(2),
        name="attn_prompt",
    )(x, *tables, kdup, vt, win[0], sinks, wout[0], g[0], b[0])


def _qgate_sample_kernel(x_ref, cos_ref, sa_ref, sb_ref, win_ref, q_ref, gate_ref):
    xb = x_ref[...].astype(BF16)
    for n in range(N_CHUNKS):
        q = jnp.dot(xb, win_ref[:, n * COL_CHUNK:(n + 1) * COL_CHUNK], preferred_element_type=F32)
        for j in range(COL_CHUNK // LANES):
            c0 = n * COL_CHUNK + j * LANES
            q_ref[:, c0:c0 + LANES] = _rope(q[:, j * LANES:(j + 1) * LANES],
                                            cos_ref[...], sa_ref[...], sb_ref[...])
        gate_ref[:, n * COL_CHUNK:(n + 1) * COL_CHUNK] = jnp.dot(
            xb, win_ref[:, D_MODEL + n * COL_CHUNK:D_MODEL + (n + 1) * COL_CHUNK],
            preferred_element_type=F32)


def _qgate_sample(x, tables, win):
    R, D = x.shape
    return pl.pallas_call(
        _qgate_sample_kernel,
        grid=(1,),
        in_specs=[_resident(x.shape)] + [_resident(t.shape) for t in tables] + [win[1]],
        out_specs=[pl.BlockSpec((R, D), lambda i: (0, 0)), pl.BlockSpec((R, D), lambda i: (0, 0))],
        out_shape=[jax.ShapeDtypeStruct((R, D), F32), jax.ShapeDtypeStruct((R, D), F32)],
        compiler_params=_params(1),
        name="qgate_sample",
    )(x, *tables, win[0])


def _attn_sample_kernel(q_ref, kn_ref, vn_ref, ck_ref, cv_ref, sink_ref, o_ref, nk_ref, nv_ref):
    nb = q_ref.shape[0]
    head_of_lane = lax.broadcasted_iota(jnp.int32, (N_HEADS, D_MODEL), 1) // HEAD_DIM
    own_head = head_of_lane == lax.broadcasted_iota(jnp.int32, (N_HEADS, D_MODEL), 0)
    low_half = lax.broadcasted_iota(jnp.int32, (N_HEADS, LANES), 1) < HEAD_DIM
    last_row = lax.broadcasted_iota(jnp.int32, (WINDOW, KV_DIM), 0) == WINDOW - 1
    sink = sink_ref[...]
    heads_per_slab = LANES // HEAD_DIM
    slabs_per_group = GROUP // heads_per_slab

    def body(i, carry):
        newk = jnp.where(last_row, kn_ref[pl.ds(i, 1), :], pltpu.roll(ck_ref[i], WINDOW - 1, 0))
        newv = jnp.where(last_row, vn_ref[pl.ds(i, 1), :], pltpu.roll(cv_ref[i], WINDOW - 1, 0))
        nk_ref[i] = newk
        nv_ref[i] = newv
        qh = jnp.where(own_head, jnp.broadcast_to(q_ref[pl.ds(i, 1), :], (N_HEADS, D_MODEL)), 0.0)
        folded = []
        for kv in range(N_KV_HEADS):
            w = qh[:, kv * GROUP * HEAD_DIM:kv * GROUP * HEAD_DIM + LANES]
            for sl in range(1, slabs_per_group):
                c0 = kv * GROUP * HEAD_DIM + sl * LANES
                w = w + qh[:, c0:c0 + LANES]
            folded.append(w + pltpu.roll(w, HEAD_DIM, 1))
        qg = jnp.concatenate([jnp.where(low_half, folded[2 * j], folded[2 * j + 1])
                              for j in range(N_KV_HEADS // 2)], axis=1)
        s = lax.dot_general(qg.astype(BF16), newk.astype(BF16), (((1,), (1,)), ((), ())),
                            preferred_element_type=F32)
        m = jnp.maximum(jnp.max(s, axis=-1, keepdims=True), sink)
        p = jnp.exp(s - m)
        denom = jnp.sum(p, axis=-1, keepdims=True) + jnp.exp(sink - m)
        og = jnp.dot(p.astype(BF16), newv.astype(BF16), preferred_element_type=F32) / denom
        slabs = []
        for kv in range(N_KV_HEADS):
            xs = og[:, (kv // 2) * LANES:(kv // 2 + 1) * LANES]
            rolled = pltpu.roll(xs, HEAD_DIM, 1)
            both = jnp.where(low_half, xs, rolled) if kv % 2 == 0 else jnp.where(low_half, rolled, xs)
            slabs.extend([both] * slabs_per_group)
        full = jnp.concatenate(slabs, axis=1)
        o_ref[pl.ds(i, 1), :] = jnp.sum(jnp.where(own_head, full, 0.0), axis=0, keepdims=True)
        return carry

    lax.fori_loop(0, nb, body, 0)


def _attn_sample(q, kn, vn, ck, cv, sinks_col):
    R, D = q.shape
    nb = SAMPLE_ATTN_BATCH
    row2 = lambda w: pl.BlockSpec((nb, w), lambda i: (i, 0))
    cache = pl.BlockSpec((nb, WINDOW, KV_DIM), lambda i: (i, 0, 0))
    return pl.pallas_call(
        _attn_sample_kernel,
        grid=(R // nb,),
        in_specs=[row2(D), row2(KV_DIM), row2(KV_DIM), cache, cache, _resident(sinks_col.shape)],
        out_specs=[row2(D), cache, cache],
        out_shape=[jax.ShapeDtypeStruct((R, D), F32),
                   jax.ShapeDtypeStruct(ck.shape, F32), jax.ShapeDtypeStruct(cv.shape, F32)],
        compiler_params=_params(1),
        name="attn_sample",
    )(q, kn, vn, ck, cv, sinks_col)


def _gated_out_sample_kernel(x_ref, a_ref, gate_ref, wout_ref, g_ref, b_ref, o_ref, h_buf):
    rows = x_ref.shape[0]
    h_buf[...] = (a_ref[...] * _silu(gate_ref[...])).astype(BF16)

    def set_rows(rs, cs, v):
        o_ref[rs, cs] = v

    _outproj_ln(h_buf, lambda rs, cs: x_ref[rs, cs], wout_ref, g_ref, b_ref,
                set_rows, lambda rs, cs: o_ref[rs, cs], rows)


def _gated_out_sample(x, a, gate, wout, g, b):
    R, D = x.shape
    return pl.pallas_call(
        _gated_out_sample_kernel,
        grid=(1,),
        in_specs=[_resident(t.shape) for t in (x, a, gate)] + [wout[1], g[1], b[1]],
        out_specs=pl.BlockSpec((R, D), lambda i: (0, 0)),
        out_shape=jax.ShapeDtypeStruct((R, D), F32),
        scratch_shapes=[pltpu.VMEM((R, D), BF16)],
        compiler_params=_params(1),
        name="gated_out_sample",
    )(x, a, gate, wout[0], g[0], b[0])


def _rope_tables(pos):
    half = ROT_DIM // 2
    inv_freq = ROPE_THETA ** (-jnp.arange(0, ROT_DIM, 2, dtype=F32) / ROT_DIM)
    ang = pos.astype(F32)[:, None] * inv_freq[None, :]
    cos, sin = jnp.cos(ang), jnp.sin(ang)
    n = pos.shape[0]
    rest = jnp.zeros((n, HEAD_DIM - ROT_DIM), F32)
    zero = jnp.zeros((n, half), F32)
    cos_h = jnp.concatenate([cos, cos, rest + 1.0], axis=1)
    sa_h = jnp.concatenate([-sin, zero, rest], axis=1)
    sb_h = jnp.concatenate([zero, sin, rest], axis=1)
    rep = LANES // HEAD_DIM
    return tuple(jnp.tile(a, (1, rep)) for a in (cos_h, sa_h, sb_h))


def kernel(x_prompt, x_sample, state_pool, cache_k, cache_v, w_in_a, w_grp_a, scale_a, w_out_a,
           w_kv, w_in_b, sinks_b, w_out_b, ln_g, ln_b):
    B, S, D = x_prompt.shape
    R = x_sample.shape[0]
    xp = x_prompt
    xs = x_sample.reshape(R, D)
    tab_p = _rope_tables(jnp.arange(S, dtype=jnp.int32))
    tab_s = _rope_tables(jnp.full((R,), PAST_LEN, jnp.int32))
    qtab_p = tuple(a * SM_SCALE for a in tab_p)
    qtab_s = tuple(a * SM_SCALE for a in tab_s)
    w_in_a, w_grp_a, w_out_a, w_in_b, w_out_b = (
        w.astype(BF16) for w in (w_in_a, w_grp_a, w_out_a, w_in_b, w_out_b))
    scale_a, ln_g, ln_b = (p.reshape(p.shape[0], 1, D) for p in (scale_a, ln_g, ln_b))
    pool_p, pool_s = [], []
    for i in range(N_A_LAYERS):
        params = [_layer(p, i) for p in (w_in_a, w_grp_a, scale_a, w_out_a, ln_g, ln_b)]
        xp, sp = _pool_prompt(xp, params)
        xs, ss = _pool_sample(xs, state_pool[i].transpose(1, 0, 2), params)
        pool_p.append(sp[:, HALO - POOL_STATE:])
        pool_s.append(ss.transpose(1, 0, 2))
    wkv = w_kv.astype(BF16)
    new_k_p, new_v_p, kdup_p, vt_p = _kv_prompt(xp, tab_p, wkv)
    k_s, v_s = _kv_sample(xs, tab_s, wkv)
    ck = cache_k.reshape(R, WINDOW, KV_DIM)
    cv = cache_v.reshape(R, WINDOW, KV_DIM)
    for j in range(DEPTH - N_A_LAYERS):
        i = N_A_LAYERS + j
        win, wout, g, b = _layer(w_in_b, j), _layer(w_out_b, j), _layer(ln_g, i), _layer(ln_b, i)
        xp = _attn_prompt(xp, qtab_p, kdup_p, vt_p, win, sinks_b[j], wout, g, b)
        q_s, gate_s = _qgate_sample(xs, qtab_s, win)
        a_s, nk, nv = _attn_sample(q_s, k_s, v_s, ck, cv, sinks_b[j][:, None])
        xs = _gated_out_sample(xs, a_s, gate_s, wout, g, b)
    kv4 = (N_KV_HEADS, HEAD_DIM)
    return (xp, xs.reshape(R, 1, D), jnp.stack(pool_p, axis=0), jnp.stack(pool_s, axis=0),
            new_k_p.reshape(B, WINDOW, *kv4), new_v_p.reshape(B, WINDOW, *kv4),
            nk.reshape(R, WINDOW, *kv4), nv.reshape(R, WINDOW, *kv4))
```

```python
import functools

import jax
import jax.numpy as jnp
from jax import lax
from jax.experimental import pallas as pl
from jax.experimental.pallas import tpu as pltpu

F32 = jnp.float32
BF16 = jnp.bfloat16

D_MODEL = 2048
DEPTH = 4
PAST_LEN = 16384
N_A_LAYERS = DEPTH // 2
POOL_WINDOWS = (2, 4, 8, 16)
POOL_GROUP = D_MODEL // len(POOL_WINDOWS)
POOL_STATE = max(POOL_WINDOWS) - 1
HEAD_DIM = 64
N_HEADS = D_MODEL // HEAD_DIM
N_KV_HEADS = N_HEADS // 8
GROUP = N_HEADS // N_KV_HEADS
KV_DIM = N_KV_HEADS * HEAD_DIM
WINDOW = 128
ROT_DIM = HEAD_DIM // 4
ROPE_THETA = 500000.0
ALPHA = (2 * DEPTH) ** 0.25
LN_EPS = 1e-5
NEG = -1e30
SM_SCALE = HEAD_DIM ** -0.5

LANES = 128
HALO = 16
COL_CHUNK = 512
N_CHUNKS = D_MODEL // COL_CHUNK
TM = 256
TK = 512
LN_ROWS = 32
SAMPLE_ATTN_BATCH = 8
SUM_ROWS = 16
VMEM_LIMIT_BYTES = 56 * 1024 * 1024


def _params(n_axes):
    return pltpu.CompilerParams(dimension_semantics=("arbitrary",) * n_axes,
                                vmem_limit_bytes=VMEM_LIMIT_BYTES)


def _resident(shape):
    zeros = (0,) * len(shape)
    return pl.BlockSpec(shape, lambda *_: zeros, pipeline_mode=pl.Buffered(1))


def _layer(stacked, i):
    zeros = (0,) * (stacked.ndim - 1)
    return stacked, pl.BlockSpec((None,) + stacked.shape[1:], lambda *_: (i,) + zeros,
                                 pipeline_mode=pl.Buffered(1))


def _silu(g):
    return g / (1.0 + jnp.exp(-g))


def _rope(x, cos, sa, sb):
    return x * cos + pltpu.roll(x, LANES - ROT_DIM // 2, 1) * sa + pltpu.roll(x, ROT_DIM // 2, 1) * sb


def _outproj_ln(h_ref, x_rows, wout_ref, g_ref, b_ref, o_rows_set, o_rows_get, rows):
    for n in range(N_CHUNKS):
        cols = slice(n * COL_CHUNK, (n + 1) * COL_CHUNK)
        y = jnp.dot(h_ref[...], wout_ref[:, cols], preferred_element_type=F32)
        o_rows_set(slice(0, rows), cols, ALPHA * x_rows(slice(0, rows), cols) + y)
    step = min(LN_ROWS, rows)
    for r0 in range(0, rows, step):
        rs = slice(r0, r0 + step)
        r = o_rows_get(rs, slice(None))
        mu = jnp.mean(r, axis=-1, keepdims=True)
        c = r - mu
        var = jnp.mean(c * c, axis=-1, keepdims=True)
        o_rows_set(rs, slice(None), c * lax.rsqrt(var + LN_EPS) * g_ref[...] + b_ref[...])


def _pool_prompt_kernel(x_ref, win_ref, wgrp_ref, scale_ref, wout_ref, g_ref, b_ref,
                        o_ref, state_ref, xb_buf, u_buf, h_buf):
    t = pl.program_id(1)

    @pl.when(t == 0)
    def _():
        u_buf[0:HALO, :] = jnp.zeros((HALO, D_MODEL), F32)

    xb_buf[...] = x_ref[0].astype(BF16)
    row = lax.broadcasted_iota(jnp.int32, (TM, 1), 0) + t * TM

    def project_u(g):
        cols = slice(g * POOL_GROUP, (g + 1) * POOL_GROUP)
        u_buf[HALO:, cols] = jnp.dot(xb_buf[...], win_ref[:, cols], preferred_element_type=F32)

    project_u(0)
    for g, w in enumerate(POOL_WINDOWS):
        cols = slice(g * POOL_GROUP, (g + 1) * POOL_GROUP)
        if g + 1 < len(POOL_WINDOWS):
            project_u(g + 1)
        ext = u_buf[:, cols]
        s = ext
        shift = 1
        while shift < w:
            s = s + pltpu.roll(s, shift, 0)
            shift *= 2
        inv_cnt = 1.0 / jnp.minimum(w, row + 1).astype(F32)
        d = s[HALO:, :] * inv_cnt - ext[HALO:, :]
        gate = jnp.dot(xb_buf[...], win_ref[:, D_MODEL + g * POOL_GROUP:D_MODEL + (g + 1) * POOL_GROUP],
                       preferred_element_type=F32)
        d = jnp.dot(d.astype(BF16), wgrp_ref[g], preferred_element_type=F32) * scale_ref[:, cols]
        h_buf[:, cols] = (d * _silu(gate)).astype(BF16)

    state_ref[0] = u_buf[TM:TM + HALO, :]
    u_buf[0:HALO, :] = u_buf[TM:TM + HALO, :]

    def set_rows(rs, cs, v):
        o_ref[0, rs, cs] = v

    _outproj_ln(h_buf, lambda rs, cs: x_ref[0, rs, cs], wout_ref, g_ref, b_ref,
                set_rows, lambda rs, cs: o_ref[0, rs, cs], TM)


def _pool_prompt(x, params):
    B, S, D = x.shape
    return pl.pallas_call(
        _pool_prompt_kernel,
        grid=(B, S // TM),
        in_specs=[pl.BlockSpec((1, TM, D), lambda bi, t: (bi, t, 0))] + [spec for _, spec in params],
        out_specs=[
            pl.BlockSpec((1, TM, D), lambda bi, t: (bi, t, 0)),
            pl.BlockSpec((1, HALO, D), lambda bi, t: (bi, 0, 0)),
        ],
        out_shape=[jax.ShapeDtypeStruct((B, S, D), F32),
                   jax.ShapeDtypeStruct((B, HALO, D), F32)],
        scratch_shapes=[pltpu.VMEM((TM, D), BF16), pltpu.VMEM((HALO + TM, D), F32), pltpu.VMEM((TM, D), BF16)],
        compiler_params=_params(2),
        name="pool_prompt",
    )(x, *[a for a, _ in params])


def _pool_sample_kernel(x_ref, st_ref, win_ref, wgrp_ref, scale_ref, wout_ref, g_ref, b_ref,
                        o_ref, nst_ref, h_buf):
    rows = x_ref.shape[0]
    xb = x_ref[...].astype(BF16)
    for g, w in enumerate(POOL_WINDOWS):
        cols = slice(g * POOL_GROUP, (g + 1) * POOL_GROUP)
        u = jnp.dot(xb, win_ref[:, cols], preferred_element_type=F32)
        acc = u
        for j in range(1, w):
            acc = acc + st_ref[POOL_STATE - j, :, cols]
        d = acc * (1.0 / min(w, PAST_LEN + 1)) - u
        d = jnp.dot(d.astype(BF16), wgrp_ref[g], preferred_element_type=F32) * scale_ref[:, cols]
        gate = jnp.dot(xb, win_ref[:, D_MODEL + g * POOL_GROUP:D_MODEL + (g + 1) * POOL_GROUP],
                       preferred_element_type=F32)
        h_buf[:, cols] = (d * _silu(gate)).astype(BF16)
        for j in range(POOL_STATE - 1):
            nst_ref[j, :, cols] = st_ref[j + 1, :, cols]
        nst_ref[POOL_STATE - 1, :, cols] = u

    def set_rows(rs, cs, v):
        o_ref[rs, cs] = v

    _outproj_ln(h_buf, lambda rs, cs: x_ref[rs, cs], wout_ref, g_ref, b_ref,
                set_rows, lambda rs, cs: o_ref[rs, cs], rows)


def _pool_sample(x, st, params):
    R, D = x.shape
    return pl.pallas_call(
        _pool_sample_kernel,
        grid=(1,),
        in_specs=[_resident(x.shape), _resident(st.shape)] + [spec for _, spec in params],
        out_specs=[pl.BlockSpec((R, D), lambda i: (0, 0)),
                   pl.BlockSpec(st.shape, lambda i: (0, 0, 0))],
        out_shape=[jax.ShapeDtypeStruct((R, D), F32), jax.ShapeDtypeStruct(st.shape, F32)],
        scratch_shapes=[pltpu.VMEM((R, D), BF16)],
        compiler_params=_params(1),
        name="pool_sample",
    )(x, st, *[a for a, _ in params])


def _project_kv(xb, cos, sa, sb, wkv_ref):
    kv = jnp.dot(xb, wkv_ref[...], preferred_element_type=F32)
    k_slabs = [_rope(kv[:, j * LANES:(j + 1) * LANES], cos, sa, sb) for j in range(KV_DIM // LANES)]
    return k_slabs, kv[:, KV_DIM:]


def _kv_prompt_kernel(x_ref, cos_ref, sa_ref, sb_ref, wkv_ref, knew_ref, vnew_ref, kdup_ref, vt_ref):
    k_slabs, v = _project_kv(x_ref[0].astype(BF16), cos_ref[...], sa_ref[...], sb_ref[...], wkv_ref)
    low = lax.broadcasted_iota(jnp.int32, (TK, LANES), 1) < HEAD_DIM
    for j, k in enumerate(k_slabs):
        swapped = pltpu.roll(k, HEAD_DIM, 1)
        kdup_ref[0, :, (2 * j) * LANES:(2 * j + 1) * LANES] = jnp.where(low, k, swapped).astype(BF16)
        kdup_ref[0, :, (2 * j + 1) * LANES:(2 * j + 2) * LANES] = jnp.where(low, swapped, k).astype(BF16)
    for i in range(TK // WINDOW):
        vt_ref[0, i] = v[i * WINDOW:(i + 1) * WINDOW, :].T.astype(BF16)

    @pl.when(pl.program_id(1) == pl.num_programs(1) - 1)
    def _():
        for j, k in enumerate(k_slabs):
            knew_ref[0, :, j * LANES:(j + 1) * LANES] = k[TK - WINDOW:, :]
        vnew_ref[0] = v[TK - WINDOW:, :]


def _kv_prompt(x, tables, wkv):
    B, S, D = x.shape
    tab = pl.BlockSpec((TK, LANES), lambda bi, t: (t, 0))
    last = pl.BlockSpec((1, WINDOW, KV_DIM), lambda bi, t: (bi, 0, 0))
    return pl.pallas_call(
        _kv_prompt_kernel,
        grid=(B, S // TK),
        in_specs=[pl.BlockSpec((1, TK, D), lambda bi, t: (bi, t, 0)), tab, tab, tab, _resident(wkv.shape)],
        out_specs=[last, last,
                   pl.BlockSpec((1, TK, N_KV_HEADS * LANES), lambda bi, t: (bi, t, 0)),
                   pl.BlockSpec((1, TK // WINDOW, KV_DIM, WINDOW), lambda bi, t: (bi, t, 0, 0))],
        out_shape=[jax.ShapeDtypeStruct((B, WINDOW, KV_DIM), F32), jax.ShapeDtypeStruct((B, WINDOW, KV_DIM), F32),
                   jax.ShapeDtypeStruct((B, S, N_KV_HEADS * LANES), BF16),
                   jax.ShapeDtypeStruct((B, S // WINDOW, KV_DIM, WINDOW), BF16)],
        compiler_params=_params(2),
        name="kv_prompt",
    )(x, *tables, wkv)


def _kv_sample_kernel(x_ref, cos_ref, sa_ref, sb_ref, wkv_ref, k_ref, v_ref):
    k_slabs, v = _project_kv(x_ref[...].astype(BF16), cos_ref[...], sa_ref[...], sb_ref[...], wkv_ref)
    for j, k in enumerate(k_slabs):
        k_ref[:, j * LANES:(j + 1) * LANES] = k
    v_ref[...] = v


def _kv_sample(x, tables, wkv):
    R, D = x.shape
    out = pl.BlockSpec((R, KV_DIM), lambda i: (0, 0))
    return pl.pallas_call(
        _kv_sample_kernel,
        grid=(1,),
        in_specs=[_resident(x.shape)] + [_resident(t.shape) for t in tables] + [_resident(wkv.shape)],
        out_specs=[out, out],
        out_shape=[jax.ShapeDtypeStruct((R, KV_DIM), F32), jax.ShapeDtypeStruct((R, KV_DIM), F32)],
        compiler_params=_params(1),
        name="kv_sample",
    )(x, *tables, wkv)


def _attn_prompt_kernel(x_ref, cos_ref, sa_ref, sb_ref, kdup_ref, vt_ref, win_ref, sink_ref, wout_ref,
                        g_ref, b_ref, o_ref, xb_buf, q_buf, h_buf, g_buf):
    t = pl.program_id(1)
    group_cols = GROUP * HEAD_DIM
    xb_buf[...] = x_ref[0].astype(BF16)

    def project_q(kv):
        q = jnp.dot(xb_buf[...], win_ref[:, kv * group_cols:(kv + 1) * group_cols], preferred_element_type=F32)
        for j in range(group_cols // LANES):
            qj = _rope(q[:, j * LANES:(j + 1) * LANES], cos_ref[...], sa_ref[...], sb_ref[...])
            c0 = kv * group_cols + j * LANES
            q_buf[:, c0:c0 + LANES] = qj.astype(BF16)

    def project_gate(kv):
        cols = slice(kv * group_cols, (kv + 1) * group_cols)
        gate = jnp.dot(xb_buf[...], win_ref[:, D_MODEL + kv * group_cols:D_MODEL + (kv + 1) * group_cols],
                       preferred_element_type=F32)
        g_buf[:, cols] = _silu(gate)

    key = lax.broadcasted_iota(jnp.int32, (2 * WINDOW, WINDOW), 0)
    qry = lax.broadcasted_iota(jnp.int32, (2 * WINDOW, WINDOW), 1)
    band = (key > qry) & (key <= qry + WINDOW)
    low_half = lax.broadcasted_iota(jnp.int32, (WINDOW, LANES), 1) < HEAD_DIM
    ones_rows = jnp.ones((SUM_ROWS, 2 * WINDOW), BF16)

    def block_ids(qb):
        blk = t * (TM // WINDOW) + qb
        return blk, jnp.maximum(blk - 1, 0)

    def group_heads(kv):
        return [(kv * (GROUP // 2) + pair, par) for pair in range(GROUP // 2) for par in range(2)]

    def scores(qb, kv):
        blk, prev_blk = block_ids(qb)
        prev = pl.multiple_of(prev_blk * WINDOW, WINDOW)
        cur = pl.multiple_of(blk * WINDOW, WINDOW)
        rows = slice(qb * WINDOW, (qb + 1) * WINDOW)
        ks = slice(kv * LANES, (kv + 1) * LANES)
        k2 = jnp.concatenate([kdup_ref[0, pl.ds(prev, WINDOW), ks], kdup_ref[0, pl.ds(cur, WINDOW), ks]], axis=0)
        q_all = []
        for slab, par in group_heads(kv):
            q_slab = q_buf[rows, slab * LANES:(slab + 1) * LANES]
            q_all.append(jnp.where(low_half == (par == 0), q_slab, jnp.zeros_like(q_slab)))
        return lax.dot_general(k2, jnp.concatenate(q_all, axis=0), (((1,), (1,)), ((), ())),
                               preferred_element_type=F32)

    def finish(qb, kv, s_t):
        blk, prev_blk = block_ids(qb)
        vis = band & ((key >= WINDOW) | (blk > 0))
        rows = slice(qb * WINDOW, (qb + 1) * WINDOW)
        vs = slice(kv * HEAD_DIM, (kv + 1) * HEAD_DIM)
        v_aug = jnp.concatenate([vt_ref[0, prev_blk, vs, :], vt_ref[0, blk, vs, :]], axis=1)
        v_aug = jnp.concatenate([v_aug, ones_rows], axis=0)
        p_t, sink_terms = [], []
        for i, (slab, par) in enumerate(group_heads(kv)):
            s = jnp.where(vis, s_t[:, i * WINDOW:(i + 1) * WINDOW], NEG)
            sink = sink_ref[2 * slab + par]
            m = jnp.maximum(jnp.max(s, axis=0, keepdims=True), sink)
            p_t.append(jnp.exp(s - m).astype(BF16))
            sink_terms.append(jnp.exp(sink - m))
        o_t = jnp.dot(v_aug, jnp.concatenate(p_t, axis=1), preferred_element_type=F32)
        for pair in range(GROUP // 2):
            both = []
            for par in range(2):
                i = 2 * pair + par
                cs = slice(i * WINDOW, (i + 1) * WINDOW)
                inv = 1.0 / (o_t[HEAD_DIM:HEAD_DIM + 1, cs] + sink_terms[i])
                both.append(o_t[:HEAD_DIM, cs] * inv)
            slab = kv * (GROUP // 2) + pair
            attn = jnp.concatenate(both, axis=0).T
            cs = slice(slab * LANES, (slab + 1) * LANES)
            h_buf[rows, cs] = (attn * g_buf[rows, cs]).astype(BF16)

    project_q(0)
    for kv in range(N_KV_HEADS):
        s_t = [scores(qb, kv) for qb in range(TM // WINDOW)]
        if kv + 1 < N_KV_HEADS:
            project_q(kv + 1)
        project_gate(kv)
        for qb in range(TM // WINDOW):
            finish(qb, kv, s_t[qb])

    def set_rows(rs, cs, v):
        o_ref[0, rs, cs] = v

    _outproj_ln(h_buf, lambda rs, cs: x_ref[0, rs, cs], wout_ref, g_ref, b_ref,
                set_rows, lambda rs, cs: o_ref[0, rs, cs], TM)


def _attn_prompt(x, tables, kdup, vt, win, sinks, wout, g, b):
    B, S, D = x.shape
    tab = pl.BlockSpec((TM, LANES), lambda bi, t: (t, 0))

    def per_batch(a):
        return pl.BlockSpec((1,) + a.shape[1:], lambda bi, t: (bi,) + (0,) * (a.ndim - 1),
                            pipeline_mode=pl.Buffered(1))

    return pl.pallas_call(
        _attn_prompt_kernel,
        grid=(B, S // TM),
        in_specs=[
            pl.BlockSpec((1, TM, D), lambda bi, t: (bi, t, 0)), tab, tab, tab, per_batch(kdup), per_batch(vt),
            win[1], pl.BlockSpec(memory_space=pltpu.SMEM), wout[1], g[1], b[1],
        ],
        out_specs=pl.BlockSpec((1, TM, D), lambda bi, t: (bi, t, 0)),
        out_shape=jax.ShapeDtypeStruct((B, S, D), F32),
        scratch_shapes=[pltpu.VMEM((TM, D), BF16), pltpu.VMEM((TM, D), BF16), pltpu.VMEM((TM, D), BF16),
                        pltpu.VMEM((TM, D), F32)],
        compiler_params=_params(2),
        name="attn_prompt",
    )(x, *tables, kdup, vt, win[0], sinks, wout[0], g[0], b[0])


def _qgate_sample_kernel(x_ref, cos_ref, sa_ref, sb_ref, win_ref, q_ref, gate_ref):
    xb = x_ref[...].astype(BF16)
    for n in range(N_CHUNKS):
        q = jnp.dot(xb, win_ref[:, n * COL_CHUNK:(n + 1) * COL_CHUNK], preferred_element_type=F32)
        for j in range(COL_CHUNK // LANES):
            c0 = n * COL_CHUNK + j * LANES
            q_ref[:, c0:c0 + LANES] = _rope(q[:, j * LANES:(j + 1) * LANES],
                                            cos_ref[...], sa_ref[...], sb_ref[...])
        gate_ref[:, n * COL_CHUNK:(n + 1) * COL_CHUNK] = jnp.dot(
            xb, win_ref[:, D_MODEL + n * COL_CHUNK:D_MODEL + (n + 1) * COL_CHUNK],
            preferred_element_type=F32)


def _qgate_sample(x, tables, win):
    R, D = x.shape
    return pl.pallas_call(
        _qgate_sample_kernel,
        grid=(1,),
        in_specs=[_resident(x.shape)] + [_resident(t.shape) for t in tables] + [win[1]],
        out_specs=[pl.BlockSpec((R, D), lambda i: (0, 0)), pl.BlockSpec((R, D), lambda i: (0, 0))],
        out_shape=[jax.ShapeDtypeStruct((R, D), F32), jax.ShapeDtypeStruct((R, D), F32)],
        compiler_params=_params(1),
        name="qgate_sample",
    )(x, *tables, win[0])


def _attn_sample_kernel(q_ref, kn_ref, vn_ref, ck_ref, cv_ref, sink_ref, o_ref, nk_ref, nv_ref):
    nb = q_ref.shape[0]
    head_of_lane = lax.broadcasted_iota(jnp.int32, (N_HEADS, D_MODEL), 1) // HEAD_DIM
    own_head = head_of_lane == lax.broadcasted_iota(jnp.int32, (N_HEADS, D_MODEL), 0)
    low_half = lax.broadcasted_iota(jnp.int32, (N_HEADS, LANES), 1) < HEAD_DIM
    last_row = lax.broadcasted_iota(jnp.int32, (WINDOW, KV_DIM), 0) == WINDOW - 1
    sink = sink_ref[...]
    heads_per_slab = LANES // HEAD_DIM
    slabs_per_group = GROUP // heads_per_slab

    def body(i, carry):
        newk = jnp.where(last_row, kn_ref[pl.ds(i, 1), :], pltpu.roll(ck_ref[i], WINDOW - 1, 0))
        newv = jnp.where(last_row, vn_ref[pl.ds(i, 1), :], pltpu.roll(cv_ref[i], WINDOW - 1, 0))
        nk_ref[i] = newk
        nv_ref[i] = newv
        qh = jnp.where(own_head, jnp.broadcast_to(q_ref[pl.ds(i, 1), :], (N_HEADS, D_MODEL)), 0.0)
        folded = []
        for kv in range(N_KV_HEADS):
            w = qh[:, kv * GROUP * HEAD_DIM:kv * GROUP * HEAD_DIM + LANES]
            for sl in range(1, slabs_per_group):
                c0 = kv * GROUP * HEAD_DIM + sl * LANES
                w = w + qh[:, c0:c0 + LANES]
            folded.append(w + pltpu.roll(w, HEAD_DIM, 1))
        qg = jnp.concatenate([jnp.where(low_half, folded[2 * j], folded[2 * j + 1])
                              for j in range(N_KV_HEADS // 2)], axis=1)
        s = lax.dot_general(qg.astype(BF16), newk.astype(BF16), (((1,), (1,)), ((), ())),
                            preferred_element_type=F32)
        m = jnp.maximum(jnp.max(s, axis=-1, keepdims=True), sink)
        p = jnp.exp(s - m)
        denom = jnp.sum(p, axis=-1, keepdims=True) + jnp.exp(sink - m)
        og = jnp.dot(p.astype(BF16), newv.astype(BF16), preferred_element_type=F32) / denom
        slabs = []
        for kv in range(N_KV_HEADS):
            xs = og[:, (kv // 2) * LANES:(kv // 2 + 1) * LANES]
            rolled = pltpu.roll(xs, HEAD_DIM, 1)
            both = jnp.where(low_half, xs, rolled) if kv % 2 == 0 else jnp.where(low_half, rolled, xs)
            slabs.extend([both] * slabs_per_group)
        full = jnp.concatenate(slabs, axis=1)
        o_ref[pl.ds(i, 1), :] = jnp.sum(jnp.where(own_head, full, 0.0), axis=0, keepdims=True)
        return carry

    lax.fori_loop(0, nb, body, 0)


def _attn_sample(q, kn, vn, ck, cv, sinks_col):
    R, D = q.shape
    nb = SAMPLE_ATTN_BATCH
    row2 = lambda w: pl.BlockSpec((nb, w), lambda i: (i, 0))
    cache = pl.BlockSpec((nb, WINDOW, KV_DIM), lambda i: (i, 0, 0))
    return pl.pallas_call(
        _attn_sample_kernel,
        grid=(R // nb,),
        in_specs=[row2(D), row2(KV_DIM), row2(KV_DIM), cache, cache, _resident(sinks_col.shape)],
        out_specs=[row2(D), cache, cache],
        out_shape=[jax.ShapeDtypeStruct((R, D), F32),
                   jax.ShapeDtypeStruct(ck.shape, F32), jax.ShapeDtypeStruct(cv.shape, F32)],
        compiler_params=_params(1),
        name="attn_sample",
    )(q, kn, vn, ck, cv, sinks_col)


def _gated_out_sample_kernel(x_ref, a_ref, gate_ref, wout_ref, g_ref, b_ref, o_ref, h_buf):
    rows = x_ref.shape[0]
    h_buf[...] = (a_ref[...] * _silu(gate_ref[...])).astype(BF16)

    def set_rows(rs, cs, v):
        o_ref[rs, cs] = v

    _outproj_ln(h_buf, lambda rs, cs: x_ref[rs, cs], wout_ref, g_ref, b_ref,
                set_rows, lambda rs, cs: o_ref[rs, cs], rows)


def _gated_out_sample(x, a, gate, wout, g, b):
    R, D = x.shape
    return pl.pallas_call(
        _gated_out_sample_kernel,
        grid=(1,),
        in_specs=[_resident(t.shape) for t in (x, a, gate)] + [wout[1], g[1], b[1]],
        out_specs=pl.BlockSpec((R, D), lambda i: (0, 0)),
        out_shape=jax.ShapeDtypeStruct((R, D), F32),
        scratch_shapes=[pltpu.VMEM((R, D), BF16)],
        compiler_params=_params(1),
        name="gated_out_sample",
    )(x, a, gate, wout[0], g[0], b[0])


def _rope_tables(pos):
    half = ROT_DIM // 2
    inv_freq = ROPE_THETA ** (-jnp.arange(0, ROT_DIM, 2, dtype=F32) / ROT_DIM)
    ang = pos.astype(F32)[:, None] * inv_freq[None, :]
    cos, sin = jnp.cos(ang), jnp.sin(ang)
    n = pos.shape[0]
    rest = jnp.zeros((n, HEAD_DIM - ROT_DIM), F32)
    zero = jnp.zeros((n, half), F32)
    cos_h = jnp.concatenate([cos, cos, rest + 1.0], axis=1)
    sa_h = jnp.concatenate([-sin, zero, rest], axis=1)
    sb_h = jnp.concatenate([zero, sin, rest], axis=1)
    rep = LANES // HEAD_DIM
    return tuple(jnp.tile(a, (1, rep)) for a in (cos_h, sa_h, sb_h))


def kernel(x_prompt, x_sample, state_pool, cache_k, cache_v, w_in_a, w_grp_a, scale_a, w_out_a,
           w_kv, w_in_b, sinks_b, w_out_b, ln_g, ln_b):
    B, S, D = x_prompt.shape
    R = x_sample.shape[0]
    xp = x_prompt
    xs = x_sample.reshape(R, D)
    tab_p = _rope_tables(jnp.arange(S, dtype=jnp.int32))
    tab_s = _rope_tables(jnp.full((R,), PAST_LEN, jnp.int32))
    qtab_p = tuple(a * SM_SCALE for a in tab_p)
    qtab_s = tuple(a * SM_SCALE for a in tab_s)
    w_in_a, w_grp_a, w_out_a, w_in_b, w_out_b = (
        w.astype(BF16) for w in (w_in_a, w_grp_a, w_out_a, w_in_b, w_out_b))
    scale_a, ln_g, ln_b = (p.reshape(p.shape[0], 1, D) for p in (scale_a, ln_g, ln_b))
    pool_p, pool_s = [], []
    for i in range(N_A_LAYERS):
        params = [_layer(p, i) for p in (w_in_a, w_grp_a, scale_a, w_out_a, ln_g, ln_b)]
        xp, sp = _pool_prompt(xp, params)
        xs, ss = _pool_sample(xs, state_pool[i].transpose(1, 0, 2), params)
        pool_p.append(sp[:, HALO - POOL_STATE:])
        pool_s.append(ss.transpose(1, 0, 2))
    wkv = w_kv.astype(BF16)
    new_k_p, new_v_p, kdup_p, vt_p = _kv_prompt(xp, tab_p, wkv)
    k_s, v_s = _kv_sample(xs, tab_s, wkv)
    ck = cache_k.reshape(R, WINDOW, KV_DIM)
    cv = cache_v.reshape(R, WINDOW, KV_DIM)
    for j in range(DEPTH - N_A_LAYERS):
        i = N_A_LAYERS + j
        win, wout, g, b = _layer(w_in_b, j), _layer(w_out_b, j), _layer(ln_g, i), _layer(ln_b, i)
        xp = _attn_prompt(xp, qtab_p, kdup_p, vt_p, win, sinks_b[j], wout, g, b)
        q_s, gate_s = _qgate_sample(xs, qtab_s, win)
        a_s, nk, nv = _attn_sample(q_s, k_s, v_s, ck, cv, sinks_b[j][:, None])
        xs = _gated_out_sample(xs, a_s, gate_s, wout, g, b)
    kv4 = (N_KV_HEADS, HEAD_DIM)
    return (xp, xs.reshape(R, 1, D), jnp.stack(pool_p, axis=0), jnp.stack(pool_s, axis=0),
            new_k_p.reshape(B, WINDOW, *kv4), new_v_p.reshape(B, WINDOW, *kv4),
            nk.reshape(R, WINDOW, *kv4), nv.reshape(R, WINDOW, *kv4))
```

```python
import functools

import jax
import jax.numpy as jnp
from jax import lax
from jax.experimental import pallas as pl
from jax.experimental.pallas import tpu as pltpu

F32 = jnp.float32
BF16 = jnp.bfloat16

D_MODEL = 2048
DEPTH = 4
PAST_LEN = 16384
N_A_LAYERS = DEPTH // 2
POOL_WINDOWS = (2, 4, 8, 16)
POOL_GROUP = D_MODEL // len(POOL_WINDOWS)
POOL_STATE = max(POOL_WINDOWS) - 1
HEAD_DIM = 64
N_HEADS = D_MODEL // HEAD_DIM
N_KV_HEADS = N_HEADS // 8
GROUP = N_HEADS // N_KV_HEADS
KV_DIM = N_KV_HEADS * HEAD_DIM
WINDOW = 128
ROT_DIM = HEAD_DIM // 4
ROPE_THETA = 500000.0
ALPHA = (2 * DEPTH) ** 0.25
LN_EPS = 1e-5
NEG = -1e30
SM_SCALE = HEAD_DIM ** -0.5

LANES = 128
SUBLANES = 8
HALO = 16
COL_CHUNK = 512
N_CHUNKS = D_MODEL // COL_CHUNK
TM = 256
TK = 512
LN_ROWS = 32
SAMPLE_ATTN_BATCH = 8
SUM_ROWS = 16
VMEM_LIMIT_BYTES = 56 * 1024 * 1024


def _params(n_axes):
    return pltpu.CompilerParams(dimension_semantics=("arbitrary",) * n_axes,
                                vmem_limit_bytes=VMEM_LIMIT_BYTES)


def _resident(shape):
    zeros = (0,) * len(shape)
    return pl.BlockSpec(shape, lambda *_: zeros, pipeline_mode=pl.Buffered(1))


def _layer(stacked, i):
    zeros = (0,) * (stacked.ndim - 1)
    return stacked, pl.BlockSpec((None,) + stacked.shape[1:], lambda *_: (i,) + zeros,
                                 pipeline_mode=pl.Buffered(1))


def _silu(g):
    return g / (1.0 + jnp.exp(-g))


def _rope(x, cos, sa, sb):
    return x * cos + pltpu.roll(x, LANES - ROT_DIM // 2, 1) * sa + pltpu.roll(x, ROT_DIM // 2, 1) * sb


def _outproj_residual(h_ref, x_rows, wout_ref, r_rows_set, rows):
    for n in range(N_CHUNKS):
        cols = slice(n * COL_CHUNK, (n + 1) * COL_CHUNK)
        y = jnp.dot(h_ref[0:rows, :], wout_ref[:, cols], preferred_element_type=F32)
        r_rows_set(slice(0, rows), cols, ALPHA * x_rows(slice(0, rows), cols) + y)


def _zero_after(v):
    bits = pltpu.bitcast(v, jnp.int32)
    half = jnp.full(bits.shape, 16, jnp.int32)
    return lax.shift_right_logical(lax.shift_right_logical(bits, half), half).astype(F32)


def _layer_norm(r_rows_get, o_rows_set, g_ref, b_ref, rows):
    step = min(LN_ROWS, rows)
    anchors = []
    for r0 in range(0, rows, step):
        rs = slice(r0, r0 + step)
        r = r_rows_get(rs, slice(None))
        mu = jnp.mean(r, axis=-1, keepdims=True)
        c = r - mu
        var = jnp.mean(c * c, axis=-1, keepdims=True)
        out = c * lax.rsqrt(var + LN_EPS) * g_ref[...] + b_ref[...]
        o_rows_set(rs, slice(None), out)
        folded = sum(out[i:i + SUBLANES, j:j + LANES]
                     for i in range(0, step, SUBLANES) for j in range(0, out.shape[1], LANES))
        anchors.append(_zero_after(folded))
    return anchors


def _outproj_ln(h_ref, x_rows, wout_ref, g_ref, b_ref, o_rows_set, o_rows_get, rows):
    _outproj_residual(h_ref, x_rows, wout_ref, o_rows_set, rows)
    _layer_norm(o_rows_get, o_rows_set, g_ref, b_ref, rows)


def _pool_layer_kernel(layer, tiles_per_batch,
                       x_ref, xs_ref, st_hbm, win_ref, wgrp_ref, scale_ref, wout_ref, g_ref, b_ref,
                       o_ref, state_ref, ys_ref, nst_hbm,
                       xb_buf, u_buf, h_buf, r_buf, st_buf, us_buf, sems):
    step = pl.program_id(0)
    n_prompt = pl.num_programs(0) - 1
    rows_s = xs_ref.shape[0]

    def mix(xb, d, g):
        cols = slice(g * POOL_GROUP, (g + 1) * POOL_GROUP)
        gate = jnp.dot(xb, win_ref[:, D_MODEL + g * POOL_GROUP:D_MODEL + (g + 1) * POOL_GROUP],
                       preferred_element_type=F32)
        d = jnp.dot(d.astype(BF16), wgrp_ref[g], preferred_element_type=F32) * scale_ref[:, cols]
        return (d * _silu(gate)).astype(BF16)

    def set_out(rs, cs, v):
        o_ref[0, rs, cs] = v

    def set_r(rs, cs, v):
        r_buf[rs, cs] = v

    def norm_previous_tile():
        return _layer_norm(lambda rs, cs: r_buf[rs, cs], set_out, g_ref, b_ref, TM)

    @pl.when(step == 0)
    def _():
        r_buf[...] = jnp.zeros(r_buf.shape, F32)

    @pl.when(step < n_prompt)
    def _():
        t = step % tiles_per_batch

        @pl.when(t == 0)
        def _():
            u_buf[0:HALO, :] = jnp.zeros((HALO, D_MODEL), F32)

        xb_buf[...] = x_ref[0].astype(BF16)
        row = lax.broadcasted_iota(jnp.int32, (TM, 1), 0) + t * TM

        def project_u(g):
            cols = slice(g * POOL_GROUP, (g + 1) * POOL_GROUP)
            u_buf[HALO:, cols] = jnp.dot(xb_buf[...], win_ref[:, cols], preferred_element_type=F32)

        project_u(0)
        anchors = norm_previous_tile()
        per_group = len(anchors) // len(POOL_WINDOWS)
        for g, w in enumerate(POOL_WINDOWS):
            cols = slice(g * POOL_GROUP, (g + 1) * POOL_GROUP)
            if g + 1 < len(POOL_WINDOWS):
                project_u(g + 1)
            ext = u_buf[:, cols]
            s = ext
            shift = 1
            while shift < w:
                s = s + pltpu.roll(s, shift, 0)
                shift *= 2
            inv_cnt = 1.0 / jnp.minimum(w, row + 1).astype(F32)
            anchor = sum(anchors[g * per_group:(g + 1) * per_group])
            anchor = jnp.tile(anchor, (TM // anchor.shape[0], POOL_GROUP // anchor.shape[1]))
            h_buf[:, cols] = mix(xb_buf[...], s[HALO:, :] * inv_cnt - ext[HALO:, :] + anchor, g)

        state_ref[0] = u_buf[TM:TM + HALO, :]
        u_buf[0:HALO, :] = u_buf[TM:TM + HALO, :]
        _outproj_residual(h_buf, lambda rs, cs: x_ref[0, rs, cs], wout_ref, set_r, TM)

    @pl.when(step == n_prompt)
    def _():
        load = pltpu.make_async_copy(st_hbm.at[layer], st_buf, sems.at[0])
        shift_old = pltpu.make_async_copy(st_hbm.at[layer, pl.ds(1, POOL_STATE - 1)],
                                          nst_hbm.at[pl.ds(0, POOL_STATE - 1)], sems.at[1])
        append_new = pltpu.make_async_copy(us_buf, nst_hbm.at[POOL_STATE - 1], sems.at[2])
        load.start()
        shift_old.start()
        norm_previous_tile()
        xb = xs_ref[...].astype(BF16)
        load.wait()
        for g, w in enumerate(POOL_WINDOWS):
            cols = slice(g * POOL_GROUP, (g + 1) * POOL_GROUP)
            u = jnp.dot(xb, win_ref[:, cols], preferred_element_type=F32)
            us_buf[:, cols] = u
            acc = u
            for j in range(1, w):
                acc = acc + st_buf[POOL_STATE - j, :, cols]
            h_buf[0:rows_s, cols] = mix(xb, acc * (1.0 / min(w, PAST_LEN + 1)) - u, g)
        append_new.start()

        def set_ys(rs, cs, v):
            ys_ref[rs, cs] = v

        _outproj_ln(h_buf, lambda rs, cs: xs_ref[rs, cs], wout_ref, g_ref, b_ref,
                    set_ys, lambda rs, cs: ys_ref[rs, cs], rows_s)
        shift_old.wait()
        append_new.wait()


def _pool_layer(x, xs, state_rows, layer, params):
    B, S, D = x.shape
    R = xs.shape[0]
    tiles_per_batch = S // TM
    n_prompt = B * tiles_per_batch

    def tile(step):
        step = jnp.clip(step, 0, n_prompt - 1)
        return step // tiles_per_batch, step % tiles_per_batch

    return pl.pallas_call(
        functools.partial(_pool_layer_kernel, layer, tiles_per_batch),
        grid=(n_prompt + 1,),
        in_specs=[pl.BlockSpec((1, TM, D), lambda i: (*tile(i), 0)), _resident(xs.shape),
                  pl.BlockSpec(memory_space=pl.ANY)] + [spec for _, spec in params],
        out_specs=[
            pl.BlockSpec((1, TM, D), lambda i: (*tile(i - 1), 0)),
            pl.BlockSpec((1, HALO, D), lambda i: (tile(i)[0], 0, 0)),
            pl.BlockSpec((R, D), lambda i: (0, 0)),
            pl.BlockSpec(memory_space=pl.ANY),
        ],
        out_shape=[jax.ShapeDtypeStruct((B, S, D), F32), jax.ShapeDtypeStruct((B, HALO, D), F32),
                   jax.ShapeDtypeStruct((R, D), F32), jax.ShapeDtypeStruct((POOL_STATE, R, D), F32)],
        scratch_shapes=[pltpu.VMEM((TM, D), BF16), pltpu.VMEM((HALO + TM, D), F32), pltpu.VMEM((TM, D), BF16),
                        pltpu.VMEM((TM, D), F32), pltpu.VMEM((POOL_STATE, R, D), F32), pltpu.VMEM((R, D), F32),
                        pltpu.SemaphoreType.DMA((3,))],
        compiler_params=_params(1),
        name="pool_layer",
    )(x, xs, state_rows, *[a for a, _ in params])


def _project_kv(xb, cos, sa, sb, wkv_ref):
    kv = jnp.dot(xb, wkv_ref[...], preferred_element_type=F32)
    k_slabs = [_rope(kv[:, j * LANES:(j + 1) * LANES], cos, sa, sb) for j in range(KV_DIM // LANES)]
    return k_slabs, kv[:, KV_DIM:]


def _kv_prompt_kernel(x_ref, cos_ref, sa_ref, sb_ref, wkv_ref, knew_ref, vnew_ref, kdup_ref, vt_ref):
    k_slabs, v = _project_kv(x_ref[0].astype(BF16), cos_ref[...], sa_ref[...], sb_ref[...], wkv_ref)
    low = lax.broadcasted_iota(jnp.int32, (TK, LANES), 1) < HEAD_DIM
    for j, k in enumerate(k_slabs):
        swapped = pltpu.roll(k, HEAD_DIM, 1)
        kdup_ref[0, :, (2 * j) * LANES:(2 * j + 1) * LANES] = jnp.where(low, k, swapped).astype(BF16)
        kdup_ref[0, :, (2 * j + 1) * LANES:(2 * j + 2) * LANES] = jnp.where(low, swapped, k).astype(BF16)
    for i in range(TK // WINDOW):
        vt_ref[0, i] = v[i * WINDOW:(i + 1) * WINDOW, :].T.astype(BF16)

    @pl.when(pl.program_id(1) == pl.num_programs(1) - 1)
    def _():
        for j, k in enumerate(k_slabs):
            knew_ref[0, :, j * LANES:(j + 1) * LANES] = k[TK - WINDOW:, :]
        vnew_ref[0] = v[TK - WINDOW:, :]


def _kv_prompt(x, tables, wkv):
    B, S, D = x.shape
    tab = pl.BlockSpec((TK, LANES), lambda bi, t: (t, 0))
    last = pl.BlockSpec((1, WINDOW, KV_DIM), lambda bi, t: (bi, 0, 0))
    return pl.pallas_call(
        _kv_prompt_kernel,
        grid=(B, S // TK),
        in_specs=[pl.BlockSpec((1, TK, D), lambda bi, t: (bi, t, 0)), tab, tab, tab, _resident(wkv.shape)],
        out_specs=[last, last,
                   pl.BlockSpec((1, TK, N_KV_HEADS * LANES), lambda bi, t: (bi, t, 0)),
                   pl.BlockSpec((1, TK // WINDOW, KV_DIM, WINDOW), lambda bi, t: (bi, t, 0, 0))],
        out_shape=[jax.ShapeDtypeStruct((B, WINDOW, KV_DIM), F32), jax.ShapeDtypeStruct((B, WINDOW, KV_DIM), F32),
                   jax.ShapeDtypeStruct((B, S, N_KV_HEADS * LANES), BF16),
                   jax.ShapeDtypeStruct((B, S // WINDOW, KV_DIM, WINDOW), BF16)],
        compiler_params=_params(2),
        name="kv_prompt",
    )(x, *tables, wkv)


def _kv_sample_kernel(x_ref, cos_ref, sa_ref, sb_ref, wkv_ref, k_ref, v_ref):
    k_slabs, v = _project_kv(x_ref[...].astype(BF16), cos_ref[...], sa_ref[...], sb_ref[...], wkv_ref)
    for j, k in enumerate(k_slabs):
        k_ref[:, j * LANES:(j + 1) * LANES] = k
    v_ref[...] = v


def _kv_sample(x, tables, wkv):
    R, D = x.shape
    out = pl.BlockSpec((R, KV_DIM), lambda i: (0, 0))
    return pl.pallas_call(
        _kv_sample_kernel,
        grid=(1,),
        in_specs=[_resident(x.shape)] + [_resident(t.shape) for t in tables] + [_resident(wkv.shape)],
        out_specs=[out, out],
        out_shape=[jax.ShapeDtypeStruct((R, KV_DIM), F32), jax.ShapeDtypeStruct((R, KV_DIM), F32)],
        compiler_params=_params(1),
        name="kv_sample",
    )(x, *tables, wkv)


def _attn_prompt_kernel(x_ref, cos_ref, sa_ref, sb_ref, kdup_ref, vt_ref, win_ref, sink_ref, wout_ref,
                        g_ref, b_ref, o_ref, xb_buf, q_buf, h_buf, g_buf):
    t = pl.program_id(1)
    group_cols = GROUP * HEAD_DIM
    xb_buf[...] = x_ref[0].astype(BF16)

    def project_q(kv):
        q = jnp.dot(xb_buf[...], win_ref[:, kv * group_cols:(kv + 1) * group_cols], preferred_element_type=F32)
        for j in range(group_cols // LANES):
            qj = _rope(q[:, j * LANES:(j + 1) * LANES], cos_ref[...], sa_ref[...], sb_ref[...])
            c0 = kv * group_cols + j * LANES
            q_buf[:, c0:c0 + LANES] = qj.astype(BF16)

    def project_gate(kv):
        cols = slice(kv * group_cols, (kv + 1) * group_cols)
        gate = jnp.dot(xb_buf[...], win_ref[:, D_MODEL + kv * group_cols:D_MODEL + (kv + 1) * group_cols],
                       preferred_element_type=F32)
        g_buf[:, cols] = _silu(gate)

    key = lax.broadcasted_iota(jnp.int32, (2 * WINDOW, WINDOW), 0)
    qry = lax.broadcasted_iota(jnp.int32, (2 * WINDOW, WINDOW), 1)
    band = (key > qry) & (key <= qry + WINDOW)
    low_half = lax.broadcasted_iota(jnp.int32, (WINDOW, LANES), 1) < HEAD_DIM
    ones_rows = jnp.ones((SUM_ROWS, 2 * WINDOW), BF16)

    def block_ids(qb):
        blk = t * (TM // WINDOW) + qb
        return blk, jnp.maximum(blk - 1, 0)

    def group_heads(kv):
        return [(kv * (GROUP // 2) + pair, par) for pair in range(GROUP // 2) for par in range(2)]

    def scores(qb, kv):
        blk, prev_blk = block_ids(qb)
        prev = pl.multiple_of(prev_blk * WINDOW, WINDOW)
        cur = pl.multiple_of(blk * WINDOW, WINDOW)
        rows = slice(qb * WINDOW, (qb + 1) * WINDOW)
        ks = slice(kv * LANES, (kv + 1) * LANES)
        k2 = jnp.concatenate([kdup_ref[0, pl.ds(prev, WINDOW), ks], kdup_ref[0, pl.ds(cur, WINDOW), ks]], axis=0)
        q_all = []
        for slab, par in group_heads(kv):
            q_slab = q_buf[rows, slab * LANES:(slab + 1) * LANES]
            q_all.append(jnp.where(low_half == (par == 0), q_slab, jnp.zeros_like(q_slab)))
        return lax.dot_general(k2, jnp.concatenate(q_all, axis=0), (((1,), (1,)), ((), ())),
                               preferred_element_type=F32)

    def finish(qb, kv, s_t):
        blk, prev_blk = block_ids(qb)
        vis = band & ((key >= WINDOW) | (blk > 0))
        rows = slice(qb * WINDOW, (qb + 1) * WINDOW)
        vs = slice(kv * HEAD_DIM, (kv + 1) * HEAD_DIM)
        v_aug = jnp.concatenate([vt_ref[0, prev_blk, vs, :], vt_ref[0, blk, vs, :]], axis=1)
        v_aug = jnp.concatenate([v_aug, ones_rows], axis=0)
        p_t, sink_terms = [], []
        for i, (slab, par) in enumerate(group_heads(kv)):
            s = jnp.where(vis, s_t[:, i * WINDOW:(i + 1) * WINDOW], NEG)
            sink = sink_ref[2 * slab + par]
            m = jnp.maximum(jnp.max(s, axis=0, keepdims=True), sink)
            p_t.append(jnp.exp(s - m).astype(BF16))
            sink_terms.append(jnp.exp(sink - m))
        o_t = jnp.dot(v_aug, jnp.concatenate(p_t, axis=1), preferred_element_type=F32)
        for pair in range(GROUP // 2):
            both = []
            for par in range(2):
                i = 2 * pair + par
                cs = slice(i * WINDOW, (i + 1) * WINDOW)
                inv = 1.0 / (o_t[HEAD_DIM:HEAD_DIM + 1, cs] + sink_terms[i])
                both.append(o_t[:HEAD_DIM, cs] * inv)
            slab = kv * (GROUP // 2) + pair
            attn = jnp.concatenate(both, axis=0).T
            cs = slice(slab * LANES, (slab + 1) * LANES)
            h_buf[rows, cs] = (attn * g_buf[rows, cs]).astype(BF16)

    project_q(0)
    for kv in range(N_KV_HEADS):
        s_t = [scores(qb, kv) for qb in range(TM // WINDOW)]
        if kv + 1 < N_KV_HEADS:
            project_q(kv + 1)
        project_gate(kv)
        for qb in range(TM // WINDOW):
            finish(qb, kv, s_t[qb])

    def set_rows(rs, cs, v):
        o_ref[0, rs, cs] = v

    _outproj_ln(h_buf, lambda rs, cs: x_ref[0, rs, cs], wout_ref, g_ref, b_ref,
                set_rows, lambda rs, cs: o_ref[0, rs, cs], TM)


def _attn_prompt(x, tables, kdup, vt, win, sinks, wout, g, b):
    B, S, D = x.shape
    tab = pl.BlockSpec((TM, LANES), lambda bi, t: (t, 0))

    def per_batch(a):
        return pl.BlockSpec((1,) + a.shape[1:], lambda bi, t: (bi,) + (0,) * (a.ndim - 1),
                            pipeline_mode=pl.Buffered(1))

    return pl.pallas_call(
        _attn_prompt_kernel,
        grid=(B, S // TM),
        in_specs=[
            pl.BlockSpec((1, TM, D), lambda bi, t: (bi, t, 0)), tab, tab, tab, per_batch(kdup), per_batch(vt),
            win[1], pl.BlockSpec(memory_space=pltpu.SMEM), wout[1], g[1], b[1],
        ],
        out_specs=pl.BlockSpec((1, TM, D), lambda bi, t: (bi, t, 0)),
        out_shape=jax.ShapeDtypeStruct((B, S, D), F32),
        scratch_shapes=[pltpu.VMEM((TM, D), BF16), pltpu.VMEM((TM, D), BF16), pltpu.VMEM((TM, D), BF16),
                        pltpu.VMEM((TM, D), F32)],
        compiler_params=_params(2),
        name="attn_prompt",
    )(x, *tables, kdup, vt, win[0], sinks, wout[0], g[0], b[0])


def _qgate_sample_kernel(x_ref, cos_ref, sa_ref, sb_ref, win_ref, q_ref, gate_ref):
    xb = x_ref[...].astype(BF16)
    for n in range(N_CHUNKS):
        q = jnp.dot(xb, win_ref[:, n * COL_CHUNK:(n + 1) * COL_CHUNK], preferred_element_type=F32)
        for j in range(COL_CHUNK // LANES):
            c0 = n * COL_CHUNK + j * LANES
            q_ref[:, c0:c0 + LANES] = _rope(q[:, j * LANES:(j + 1) * LANES],
                                            cos_ref[...], sa_ref[...], sb_ref[...])
        gate_ref[:, n * COL_CHUNK:(n + 1) * COL_CHUNK] = jnp.dot(
            xb, win_ref[:, D_MODEL + n * COL_CHUNK:D_MODEL + (n + 1) * COL_CHUNK],
            preferred_element_type=F32)


def _qgate_sample(x, tables, win):
    R, D = x.shape
    return pl.pallas_call(
        _qgate_sample_kernel,
        grid=(1,),
        in_specs=[_resident(x.shape)] + [_resident(t.shape) for t in tables] + [win[1]],
        out_specs=[pl.BlockSpec((R, D), lambda i: (0, 0)), pl.BlockSpec((R, D), lambda i: (0, 0))],
        out_shape=[jax.ShapeDtypeStruct((R, D), F32), jax.ShapeDtypeStruct((R, D), F32)],
        compiler_params=_params(1),
        name="qgate_sample",
    )(x, *tables, win[0])


def _attn_sample_kernel(q_ref, kn_ref, vn_ref, ck_ref, cv_ref, sink_ref, o_ref, nk_ref, nv_ref):
    nb = q_ref.shape[0]
    head_of_lane = lax.broadcasted_iota(jnp.int32, (N_HEADS, D_MODEL), 1) // HEAD_DIM
    own_head = head_of_lane == lax.broadcasted_iota(jnp.int32, (N_HEADS, D_MODEL), 0)
    low_half = lax.broadcasted_iota(jnp.int32, (N_HEADS, LANES), 1) < HEAD_DIM
    last_row = lax.broadcasted_iota(jnp.int32, (WINDOW, KV_DIM), 0) == WINDOW - 1
    sink = sink_ref[...]
    heads_per_slab = LANES // HEAD_DIM
    slabs_per_group = GROUP // heads_per_slab

    def body(i, carry):
        newk = jnp.where(last_row, kn_ref[pl.ds(i, 1), :], pltpu.roll(ck_ref[i], WINDOW - 1, 0))
        newv = jnp.where(last_row, vn_ref[pl.ds(i, 1), :], pltpu.roll(cv_ref[i], WINDOW - 1, 0))
        nk_ref[i] = newk
        nv_ref[i] = newv
        qh = jnp.where(own_head, jnp.broadcast_to(q_ref[pl.ds(i, 1), :], (N_HEADS, D_MODEL)), 0.0)
        folded = []
        for kv in range(N_KV_HEADS):
            w = qh[:, kv * GROUP * HEAD_DIM:kv * GROUP * HEAD_DIM + LANES]
            for sl in range(1, slabs_per_group):
                c0 = kv * GROUP * HEAD_DIM + sl * LANES
                w = w + qh[:, c0:c0 + LANES]
            folded.append(w + pltpu.roll(w, HEAD_DIM, 1))
        qg = jnp.concatenate([jnp.where(low_half, folded[2 * j], folded[2 * j + 1])
                              for j in range(N_KV_HEADS // 2)], axis=1)
        s = lax.dot_general(qg.astype(BF16), newk.astype(BF16), (((1,), (1,)), ((), ())),
                            preferred_element_type=F32)
        m = jnp.maximum(jnp.max(s, axis=-1, keepdims=True), sink)
        p = jnp.exp(s - m)
        denom = jnp.sum(p, axis=-1, keepdims=True) + jnp.exp(sink - m)
        og = jnp.dot(p.astype(BF16), newv.astype(BF16), preferred_element_type=F32) / denom
        slabs = []
        for kv in range(N_KV_HEADS):
            xs = og[:, (kv // 2) * LANES:(kv // 2 + 1) * LANES]
            rolled = pltpu.roll(xs, HEAD_DIM, 1)
            both = jnp.where(low_half, xs, rolled) if kv % 2 == 0 else jnp.where(low_half, rolled, xs)
            slabs.extend([both] * slabs_per_group)
        full = jnp.concatenate(slabs, axis=1)
        o_ref[pl.ds(i, 1), :] = jnp.sum(jnp.where(own_head, full, 0.0), axis=0, keepdims=True)
        return carry

    lax.fori_loop(0, nb, body, 0)


def _attn_sample(q, kn, vn, ck, cv, sinks_col):
    R, D = q.shape
    nb = SAMPLE_ATTN_BATCH
    row2 = lambda w: pl.BlockSpec((nb, w), lambda i: (i, 0))
    cache = pl.BlockSpec((nb, WINDOW, KV_DIM), lambda i: (i, 0, 0))
    return pl.pallas_call(
        _attn_sample_kernel,
        grid=(R // nb,),
        in_specs=[row2(D), row2(KV_DIM), row2(KV_DIM), cache, cache, _resident(sinks_col.shape)],
        out_specs=[row2(D), cache, cache],
        out_shape=[jax.ShapeDtypeStruct((R, D), F32),
                   jax.ShapeDtypeStruct(ck.shape, F32), jax.ShapeDtypeStruct(cv.shape, F32)],
        compiler_params=_params(1),
        name="attn_sample",
    )(q, kn, vn, ck, cv, sinks_col)


def _gated_out_sample_kernel(x_ref, a_ref, gate_ref, wout_ref, g_ref, b_ref, o_ref, h_buf):
    rows = x_ref.shape[0]
    h_buf[...] = (a_ref[...] * _silu(gate_ref[...])).astype(BF16)

    def set_rows(rs, cs, v):
        o_ref[rs, cs] = v

    _outproj_ln(h_buf, lambda rs, cs: x_ref[rs, cs], wout_ref, g_ref, b_ref,
                set_rows, lambda rs, cs: o_ref[rs, cs], rows)


def _gated_out_sample(x, a, gate, wout, g, b):
    R, D = x.shape
    return pl.pallas_call(
        _gated_out_sample_kernel,
        grid=(1,),
        in_specs=[_resident(t.shape) for t in (x, a, gate)] + [wout[1], g[1], b[1]],
        out_specs=pl.BlockSpec((R, D), lambda i: (0, 0)),
        out_shape=jax.ShapeDtypeStruct((R, D), F32),
        scratch_shapes=[pltpu.VMEM((R, D), BF16)],
        compiler_params=_params(1),
        name="gated_out_sample",
    )(x, a, gate, wout[0], g[0], b[0])


def _rope_tables(pos):
    half = ROT_DIM // 2
    inv_freq = ROPE_THETA ** (-jnp.arange(0, ROT_DIM, 2, dtype=F32) / ROT_DIM)
    ang = pos.astype(F32)[:, None] * inv_freq[None, :]
    cos, sin = jnp.cos(ang), jnp.sin(ang)
    n = pos.shape[0]
    rest = jnp.zeros((n, HEAD_DIM - ROT_DIM), F32)
    zero = jnp.zeros((n, half), F32)
    cos_h = jnp.concatenate([cos, cos, rest + 1.0], axis=1)
    sa_h = jnp.concatenate([-sin, zero, rest], axis=1)
    sb_h = jnp.concatenate([zero, sin, rest], axis=1)
    rep = LANES // HEAD_DIM
    return tuple(jnp.tile(a, (1, rep)) for a in (cos_h, sa_h, sb_h))


def kernel(x_prompt, x_sample, state_pool, cache_k, cache_v, w_in_a, w_grp_a, scale_a, w_out_a,
           w_kv, w_in_b, sinks_b, w_out_b, ln_g, ln_b):
    B, S, D = x_prompt.shape
    R = x_sample.shape[0]
    xp = x_prompt
    xs = x_sample.reshape(R, D)
    tab_p = _rope_tables(jnp.arange(S, dtype=jnp.int32))
    tab_s = _rope_tables(jnp.full((R,), PAST_LEN, jnp.int32))
    qtab_p = tuple(a * SM_SCALE for a in tab_p)
    qtab_s = tuple(a * SM_SCALE for a in tab_s)
    w_in_a, w_grp_a, w_out_a, w_in_b, w_out_b = (
        w.astype(BF16) for w in (w_in_a, w_grp_a, w_out_a, w_in_b, w_out_b))
    scale_a, ln_g, ln_b = (p.reshape(p.shape[0], 1, D) for p in (scale_a, ln_g, ln_b))
    state_rows = state_pool.transpose(0, 2, 1, 3)
    pool_p, pool_s = [], []
    for i in range(N_A_LAYERS):
        params = [_layer(p, i) for p in (w_in_a, w_grp_a, scale_a, w_out_a, ln_g, ln_b)]
        xp, sp, xs, ss = _pool_layer(xp, xs, state_rows, i, params)
        pool_p.append(sp[:, HALO - POOL_STATE:])
        pool_s.append(ss)
    wkv = w_kv.astype(BF16)
    new_k_p, new_v_p, kdup_p, vt_p = _kv_prompt(xp, tab_p, wkv)
    k_s, v_s = _kv_sample(xs, tab_s, wkv)
    ck = cache_k.reshape(R, WINDOW, KV_DIM)
    cv = cache_v.reshape(R, WINDOW, KV_DIM)
    for j in range(DEPTH - N_A_LAYERS):
        i = N_A_LAYERS + j
        win, wout, g, b = _layer(w_in_b, j), _layer(w_out_b, j), _layer(ln_g, i), _layer(ln_b, i)
        xp = _attn_prompt(xp, qtab_p, kdup_p, vt_p, win, sinks_b[j], wout, g, b)
        q_s, gate_s = _qgate_sample(xs, qtab_s, win)
        a_s, nk, nv = _attn_sample(q_s, k_s, v_s, ck, cv, sinks_b[j][:, None])
        xs = _gated_out_sample(xs, a_s, gate_s, wout, g, b)
    kv4 = (N_KV_HEADS, HEAD_DIM)
    return (xp, xs.reshape(R, 1, D), jnp.stack(pool_p, axis=0), jnp.stack(pool_s, axis=0).transpose(0, 2, 1, 3),
            new_k_p.reshape(B, WINDOW, *kv4), new_v_p.reshape(B, WINDOW, *kv4),
            nk.reshape(R, WINDOW, *kv4), nv.reshape(R, WINDOW, *kv4))
```

```python
import functools

import jax
import jax.numpy as jnp
from jax import lax
from jax.experimental import pallas as pl
from jax.experimental.pallas import tpu as pltpu

F32 = jnp.float32
BF16 = jnp.bfloat16

D_MODEL = 2048
DEPTH = 4
PAST_LEN = 16384
N_A_LAYERS = DEPTH // 2
POOL_WINDOWS = (2, 4, 8, 16)
POOL_GROUP = D_MODEL // len(POOL_WINDOWS)
POOL_STATE = max(POOL_WINDOWS) - 1
HEAD_DIM = 64
N_HEADS = D_MODEL // HEAD_DIM
N_KV_HEADS = N_HEADS // 8
GROUP = N_HEADS // N_KV_HEADS
KV_DIM = N_KV_HEADS * HEAD_DIM
WINDOW = 128
ROT_DIM = HEAD_DIM // 4
ROPE_THETA = 500000.0
ALPHA = (2 * DEPTH) ** 0.25
LN_EPS = 1e-5
NEG = -1e30
SM_SCALE = HEAD_DIM ** -0.5

LANES = 128
SUBLANES = 8
HALO = 16
COL_CHUNK = 512
N_CHUNKS = D_MODEL // COL_CHUNK
TM = 256
TK = 512
LN_ROWS = 32
SAMPLE_ATTN_BATCH = 8
SUM_ROWS = 16
VMEM_LIMIT_BYTES = 56 * 1024 * 1024


def _params(n_axes):
    return pltpu.CompilerParams(dimension_semantics=("arbitrary",) * n_axes,
                                vmem_limit_bytes=VMEM_LIMIT_BYTES)


def _resident(shape):
    zeros = (0,) * len(shape)
    return pl.BlockSpec(shape, lambda *_: zeros, pipeline_mode=pl.Buffered(1))


def _layer(stacked, i):
    zeros = (0,) * (stacked.ndim - 1)
    return stacked, pl.BlockSpec((None,) + stacked.shape[1:], lambda *_: (i,) + zeros,
                                 pipeline_mode=pl.Buffered(1))


def _silu(g):
    return g / (1.0 + jnp.exp(-g))


def _rope(x, cos, sa, sb):
    return x * cos + pltpu.roll(x, LANES - ROT_DIM // 2, 1) * sa + pltpu.roll(x, ROT_DIM // 2, 1) * sb


def _outproj_residual(h_ref, x_rows, wout_ref, r_rows_set, rows):
    for n in range(N_CHUNKS):
        cols = slice(n * COL_CHUNK, (n + 1) * COL_CHUNK)
        y = jnp.dot(h_ref[0:rows, :], wout_ref[:, cols], preferred_element_type=F32)
        r_rows_set(slice(0, rows), cols, ALPHA * x_rows(slice(0, rows), cols) + y)


def _zero_after(v):
    bits = pltpu.bitcast(v, jnp.int32)
    half = jnp.full(bits.shape, 16, jnp.int32)
    return lax.shift_right_logical(lax.shift_right_logical(bits, half), half).astype(F32)


def _layer_norm(r_rows_get, o_rows_set, g_ref, b_ref, rows):
    step = min(LN_ROWS, rows)
    anchors = []
    for r0 in range(0, rows, step):
        rs = slice(r0, r0 + step)
        r = r_rows_get(rs, slice(None))
        mu = jnp.mean(r, axis=-1, keepdims=True)
        c = r - mu
        var = jnp.mean(c * c, axis=-1, keepdims=True)
        out = c * lax.rsqrt(var + LN_EPS) * g_ref[...] + b_ref[...]
        o_rows_set(rs, slice(None), out)
        folded = sum(out[i:i + SUBLANES, j:j + LANES]
                     for i in range(0, step, SUBLANES) for j in range(0, out.shape[1], LANES))
        anchors.append(_zero_after(folded))
    return anchors


def _outproj_ln(h_ref, x_rows, wout_ref, g_ref, b_ref, o_rows_set, o_rows_get, rows):
    _outproj_residual(h_ref, x_rows, wout_ref, o_rows_set, rows)
    _layer_norm(o_rows_get, o_rows_set, g_ref, b_ref, rows)


def _pool_layer_kernel(layer, tiles_per_batch,
                       x_ref, xs_ref, st_hbm, win_ref, wgrp_ref, scale_ref, wout_ref, g_ref, b_ref,
                       o_ref, state_ref, ys_ref, nst_hbm,
                       xb_buf, u_buf, h_buf, r_buf, st_buf, us_buf, sems):
    step = pl.program_id(0)
    n_prompt = pl.num_programs(0) - 1
    rows_s = xs_ref.shape[0]

    def mix(xb, d, g):
        cols = slice(g * POOL_GROUP, (g + 1) * POOL_GROUP)
        gate = jnp.dot(xb, win_ref[:, D_MODEL + g * POOL_GROUP:D_MODEL + (g + 1) * POOL_GROUP],
                       preferred_element_type=F32)
        d = jnp.dot(d.astype(BF16), wgrp_ref[g], preferred_element_type=F32) * scale_ref[:, cols]
        return (d * _silu(gate)).astype(BF16)

    def set_out(rs, cs, v):
        o_ref[0, rs, cs] = v

    def set_r(rs, cs, v):
        r_buf[rs, cs] = v

    def norm_previous_tile():
        return _layer_norm(lambda rs, cs: r_buf[rs, cs], set_out, g_ref, b_ref, TM)

    @pl.when(step == 0)
    def _():
        r_buf[...] = jnp.zeros(r_buf.shape, F32)

    @pl.when(step < n_prompt)
    def _():
        t = step % tiles_per_batch

        @pl.when(t == 0)
        def _():
            u_buf[0:HALO, :] = jnp.zeros((HALO, D_MODEL), F32)

        xb_buf[...] = x_ref[0].astype(BF16)
        row = lax.broadcasted_iota(jnp.int32, (TM, 1), 0) + t * TM

        def project_u(g):
            cols = slice(g * POOL_GROUP, (g + 1) * POOL_GROUP)
            u_buf[HALO:, cols] = jnp.dot(xb_buf[...], win_ref[:, cols], preferred_element_type=F32)

        project_u(0)
        for g, w in enumerate(POOL_WINDOWS):
            cols = slice(g * POOL_GROUP, (g + 1) * POOL_GROUP)
            if g + 1 < len(POOL_WINDOWS):
                project_u(g + 1)
            ext = u_buf[:, cols]
            s = ext
            shift = 1
            while shift < w:
                s = s + pltpu.roll(s, shift, 0)
                shift *= 2
            inv_cnt = 1.0 / jnp.minimum(w, row + 1).astype(F32)
            h_buf[:, cols] = mix(xb_buf[...], s[HALO:, :] * inv_cnt - ext[HALO:, :], g)

        state_ref[0] = u_buf[TM:TM + HALO, :]
        u_buf[0:HALO, :] = u_buf[TM:TM + HALO, :]
        _outproj_residual(h_buf, lambda rs, cs: x_ref[0, rs, cs], wout_ref, set_r, TM)
        norm_previous_tile()

    @pl.when(step == n_prompt)
    def _():
        load = pltpu.make_async_copy(st_hbm.at[layer], st_buf, sems.at[0])
        shift_old = pltpu.make_async_copy(st_hbm.at[layer, pl.ds(1, POOL_STATE - 1)],
                                          nst_hbm.at[pl.ds(0, POOL_STATE - 1)], sems.at[1])
        append_new = pltpu.make_async_copy(us_buf, nst_hbm.at[POOL_STATE - 1], sems.at[2])
        load.start()
        shift_old.start()
        xb = xs_ref[...].astype(BF16)
        load.wait()
        for g, w in enumerate(POOL_WINDOWS):
            cols = slice(g * POOL_GROUP, (g + 1) * POOL_GROUP)
            u = jnp.dot(xb, win_ref[:, cols], preferred_element_type=F32)
            us_buf[:, cols] = u
            acc = u
            for j in range(1, w):
                acc = acc + st_buf[POOL_STATE - j, :, cols]
            h_buf[0:rows_s, cols] = mix(xb, acc * (1.0 / min(w, PAST_LEN + 1)) - u, g)
        append_new.start()

        def set_ys(rs, cs, v):
            ys_ref[rs, cs] = v

        _outproj_ln(h_buf, lambda rs, cs: xs_ref[rs, cs], wout_ref, g_ref, b_ref,
                    set_ys, lambda rs, cs: ys_ref[rs, cs], rows_s)
        shift_old.wait()
        append_new.wait()


def _pool_layer(x, xs, state_rows, layer, params):
    B, S, D = x.shape
    R = xs.shape[0]
    tiles_per_batch = S // TM
    n_prompt = B * tiles_per_batch

    def tile(step):
        step = jnp.clip(step, 0, n_prompt - 1)
        return step // tiles_per_batch, step % tiles_per_batch

    return pl.pallas_call(
        functools.partial(_pool_layer_kernel, layer, tiles_per_batch),
        grid=(n_prompt + 1,),
        in_specs=[pl.BlockSpec((1, TM, D), lambda i: (*tile(i), 0)), _resident(xs.shape),
                  pl.BlockSpec(memory_space=pl.ANY)] + [spec for _, spec in params],
        out_specs=[
            pl.BlockSpec((1, TM, D), lambda i: (*tile(i), 0)),
            pl.BlockSpec((1, HALO, D), lambda i: (tile(i)[0], 0, 0)),
            pl.BlockSpec((R, D), lambda i: (0, 0)),
            pl.BlockSpec(memory_space=pl.ANY),
        ],
        out_shape=[jax.ShapeDtypeStruct((B, S, D), F32), jax.ShapeDtypeStruct((B, HALO, D), F32),
                   jax.ShapeDtypeStruct((R, D), F32), jax.ShapeDtypeStruct((POOL_STATE, R, D), F32)],
        scratch_shapes=[pltpu.VMEM((TM, D), BF16), pltpu.VMEM((HALO + TM, D), F32), pltpu.VMEM((TM, D), BF16),
                        pltpu.VMEM((TM, D), F32), pltpu.VMEM((POOL_STATE, R, D), F32), pltpu.VMEM((R, D), F32),
                        pltpu.SemaphoreType.DMA((3,))],
        compiler_params=_params(1),
        name="pool_layer",
    )(x, xs, state_rows, *[a for a, _ in params])


def _project_kv(xb, cos, sa, sb, wkv_ref):
    kv = jnp.dot(xb, wkv_ref[...], preferred_element_type=F32)
    k_slabs = [_rope(kv[:, j * LANES:(j + 1) * LANES], cos, sa, sb) for j in range(KV_DIM // LANES)]
    return k_slabs, kv[:, KV_DIM:]


def _kv_prompt_kernel(x_ref, cos_ref, sa_ref, sb_ref, wkv_ref, knew_ref, vnew_ref, kdup_ref, vt_ref):
    k_slabs, v = _project_kv(x_ref[0].astype(BF16), cos_ref[...], sa_ref[...], sb_ref[...], wkv_ref)
    low = lax.broadcasted_iota(jnp.int32, (TK, LANES), 1) < HEAD_DIM
    for j, k in enumerate(k_slabs):
        swapped = pltpu.roll(k, HEAD_DIM, 1)
        kdup_ref[0, :, (2 * j) * LANES:(2 * j + 1) * LANES] = jnp.where(low, k, swapped).astype(BF16)
        kdup_ref[0, :, (2 * j + 1) * LANES:(2 * j + 2) * LANES] = jnp.where(low, swapped, k).astype(BF16)
    for i in range(TK // WINDOW):
        vt_ref[0, i] = v[i * WINDOW:(i + 1) * WINDOW, :].T.astype(BF16)

    @pl.when(pl.program_id(1) == pl.num_programs(1) - 1)
    def _():
        for j, k in enumerate(k_slabs):
            knew_ref[0, :, j * LANES:(j + 1) * LANES] = k[TK - WINDOW:, :]
        vnew_ref[0] = v[TK - WINDOW:, :]


def _kv_prompt(x, tables, wkv):
    B, S, D = x.shape
    tab = pl.BlockSpec((TK, LANES), lambda bi, t: (t, 0))
    last = pl.BlockSpec((1, WINDOW, KV_DIM), lambda bi, t: (bi, 0, 0))
    return pl.pallas_call(
        _kv_prompt_kernel,
        grid=(B, S // TK),
        in_specs=[pl.BlockSpec((1, TK, D), lambda bi, t: (bi, t, 0)), tab, tab, tab, _resident(wkv.shape)],
        out_specs=[last, last,
                   pl.BlockSpec((1, TK, N_KV_HEADS * LANES), lambda bi, t: (bi, t, 0)),
                   pl.BlockSpec((1, TK // WINDOW, KV_DIM, WINDOW), lambda bi, t: (bi, t, 0, 0))],
        out_shape=[jax.ShapeDtypeStruct((B, WINDOW, KV_DIM), F32), jax.ShapeDtypeStruct((B, WINDOW, KV_DIM), F32),
                   jax.ShapeDtypeStruct((B, S, N_KV_HEADS * LANES), BF16),
                   jax.ShapeDtypeStruct((B, S // WINDOW, KV_DIM, WINDOW), BF16)],
        compiler_params=_params(2),
        name="kv_prompt",
    )(x, *tables, wkv)


def _kv_sample_kernel(x_ref, cos_ref, sa_ref, sb_ref, wkv_ref, k_ref, v_ref):
    k_slabs, v = _project_kv(x_ref[...].astype(BF16), cos_ref[...], sa_ref[...], sb_ref[...], wkv_ref)
    for j, k in enumerate(k_slabs):
        k_ref[:, j * LANES:(j + 1) * LANES] = k
    v_ref[...] = v


def _kv_sample(x, tables, wkv):
    R, D = x.shape
    out = pl.BlockSpec((R, KV_DIM), lambda i: (0, 0))
    return pl.pallas_call(
        _kv_sample_kernel,
        grid=(1,),
        in_specs=[_resident(x.shape)] + [_resident(t.shape) for t in tables] + [_resident(wkv.shape)],
        out_specs=[out, out],
        out_shape=[jax.ShapeDtypeStruct((R, KV_DIM), F32), jax.ShapeDtypeStruct((R, KV_DIM), F32)],
        compiler_params=_params(1),
        name="kv_sample",
    )(x, *tables, wkv)


def _attn_prompt_kernel(x_ref, cos_ref, sa_ref, sb_ref, kdup_ref, vt_ref, win_ref, sink_ref, wout_ref,
                        g_ref, b_ref, o_ref, xb_buf, q_buf, h_buf, g_buf):
    t = pl.program_id(1)
    group_cols = GROUP * HEAD_DIM
    xb_buf[...] = x_ref[0].astype(BF16)

    def project_q(kv):
        q = jnp.dot(xb_buf[...], win_ref[:, kv * group_cols:(kv + 1) * group_cols], preferred_element_type=F32)
        for j in range(group_cols // LANES):
            qj = _rope(q[:, j * LANES:(j + 1) * LANES], cos_ref[...], sa_ref[...], sb_ref[...])
            c0 = kv * group_cols + j * LANES
            q_buf[:, c0:c0 + LANES] = qj.astype(BF16)

    def project_gate(kv):
        cols = slice(kv * group_cols, (kv + 1) * group_cols)
        gate = jnp.dot(xb_buf[...], win_ref[:, D_MODEL + kv * group_cols:D_MODEL + (kv + 1) * group_cols],
                       preferred_element_type=F32)
        g_buf[:, cols] = _silu(gate)

    key = lax.broadcasted_iota(jnp.int32, (2 * WINDOW, WINDOW), 0)
    qry = lax.broadcasted_iota(jnp.int32, (2 * WINDOW, WINDOW), 1)
    band = (key > qry) & (key <= qry + WINDOW)
    low_half = lax.broadcasted_iota(jnp.int32, (WINDOW, LANES), 1) < HEAD_DIM
    ones_rows = jnp.ones((SUM_ROWS, 2 * WINDOW), BF16)

    def block_ids(qb):
        blk = t * (TM // WINDOW) + qb
        return blk, jnp.maximum(blk - 1, 0)

    def group_heads(kv):
        return [(kv * (GROUP // 2) + pair, par) for pair in range(GROUP // 2) for par in range(2)]

    def scores(qb, kv):
        blk, prev_blk = block_ids(qb)
        prev = pl.multiple_of(prev_blk * WINDOW, WINDOW)
        cur = pl.multiple_of(blk * WINDOW, WINDOW)
        rows = slice(qb * WINDOW, (qb + 1) * WINDOW)
        ks = slice(kv * LANES, (kv + 1) * LANES)
        k2 = jnp.concatenate([kdup_ref[0, pl.ds(prev, WINDOW), ks], kdup_ref[0, pl.ds(cur, WINDOW), ks]], axis=0)
        q_all = []
        for slab, par in group_heads(kv):
            q_slab = q_buf[rows, slab * LANES:(slab + 1) * LANES]
            q_all.append(jnp.where(low_half == (par == 0), q_slab, jnp.zeros_like(q_slab)))
        return lax.dot_general(k2, jnp.concatenate(q_all, axis=0), (((1,), (1,)), ((), ())),
                               preferred_element_type=F32)

    def finish(qb, kv, s_t):
        blk, prev_blk = block_ids(qb)
        vis = band & ((key >= WINDOW) | (blk > 0))
        rows = slice(qb * WINDOW, (qb + 1) * WINDOW)
        vs = slice(kv * HEAD_DIM, (kv + 1) * HEAD_DIM)
        v_aug = jnp.concatenate([vt_ref[0, prev_blk, vs, :], vt_ref[0, blk, vs, :]], axis=1)
        v_aug = jnp.concatenate([v_aug, ones_rows], axis=0)
        p_t, sink_terms = [], []
        for i, (slab, par) in enumerate(group_heads(kv)):
            s = jnp.where(vis, s_t[:, i * WINDOW:(i + 1) * WINDOW], NEG)
            sink = sink_ref[2 * slab + par]
            m = jnp.maximum(jnp.max(s, axis=0, keepdims=True), sink)
            p_t.append(jnp.exp(s - m).astype(BF16))
            sink_terms.append(jnp.exp(sink - m))
        o_t = jnp.dot(v_aug, jnp.concatenate(p_t, axis=1), preferred_element_type=F32)
        for pair in range(GROUP // 2):
            both = []
            for par in range(2):
                i = 2 * pair + par
                cs = slice(i * WINDOW, (i + 1) * WINDOW)
                inv = 1.0 / (o_t[HEAD_DIM:HEAD_DIM + 1, cs] + sink_terms[i])
                both.append(o_t[:HEAD_DIM, cs] * inv)
            slab = kv * (GROUP // 2) + pair
            attn = jnp.concatenate(both, axis=0).T
            cs = slice(slab * LANES, (slab + 1) * LANES)
            h_buf[rows, cs] = (attn * g_buf[rows, cs]).astype(BF16)

    project_q(0)
    for kv in range(N_KV_HEADS):
        s_t = [scores(qb, kv) for qb in range(TM // WINDOW)]
        if kv + 1 < N_KV_HEADS:
            project_q(kv + 1)
        project_gate(kv)
        for qb in range(TM // WINDOW):
            finish(qb, kv, s_t[qb])

    def set_rows(rs, cs, v):
        o_ref[0, rs, cs] = v

    _outproj_ln(h_buf, lambda rs, cs: x_ref[0, rs, cs], wout_ref, g_ref, b_ref,
                set_rows, lambda rs, cs: o_ref[0, rs, cs], TM)


def _attn_prompt(x, tables, kdup, vt, win, sinks, wout, g, b):
    B, S, D = x.shape
    tab = pl.BlockSpec((TM, LANES), lambda bi, t: (t, 0))

    def per_batch(a):
        return pl.BlockSpec((1,) + a.shape[1:], lambda bi, t: (bi,) + (0,) * (a.ndim - 1),
                            pipeline_mode=pl.Buffered(1))

    return pl.pallas_call(
        _attn_prompt_kernel,
        grid=(B, S // TM),
        in_specs=[
            pl.BlockSpec((1, TM, D), lambda bi, t: (bi, t, 0)), tab, tab, tab, per_batch(kdup), per_batch(vt),
            win[1], pl.BlockSpec(memory_space=pltpu.SMEM), wout[1], g[1], b[1],
        ],
        out_specs=pl.BlockSpec((1, TM, D), lambda bi, t: (bi, t, 0)),
        out_shape=jax.ShapeDtypeStruct((B, S, D), F32),
        scratch_shapes=[pltpu.VMEM((TM, D), BF16), pltpu.VMEM((TM, D), BF16), pltpu.VMEM((TM, D), BF16),
                        pltpu.VMEM((TM, D), F32)],
        compiler_params=_params(2),
        name="attn_prompt",
    )(x, *tables, kdup, vt, win[0], sinks, wout[0], g[0], b[0])


def _qgate_sample_kernel(x_ref, cos_ref, sa_ref, sb_ref, win_ref, q_ref, gate_ref):
    xb = x_ref[...].astype(BF16)
    for n in range(N_CHUNKS):
        q = jnp.dot(xb, win_ref[:, n * COL_CHUNK:(n + 1) * COL_CHUNK], preferred_element_type=F32)
        for j in range(COL_CHUNK // LANES):
            c0 = n * COL_CHUNK + j * LANES
            q_ref[:, c0:c0 + LANES] = _rope(q[:, j * LANES:(j + 1) * LANES],
                                            cos_ref[...], sa_ref[...], sb_ref[...])
        gate_ref[:, n * COL_CHUNK:(n + 1) * COL_CHUNK] = jnp.dot(
            xb, win_ref[:, D_MODEL + n * COL_CHUNK:D_MODEL + (n + 1) * COL_CHUNK],
            preferred_element_type=F32)


def _qgate_sample(x, tables, win):
    R, D = x.shape
    return pl.pallas_call(
        _qgate_sample_kernel,
        grid=(1,),
        in_specs=[_resident(x.shape)] + [_resident(t.shape) for t in tables] + [win[1]],
        out_specs=[pl.BlockSpec((R, D), lambda i: (0, 0)), pl.BlockSpec((R, D), lambda i: (0, 0))],
        out_shape=[jax.ShapeDtypeStruct((R, D), F32), jax.ShapeDtypeStruct((R, D), F32)],
        compiler_params=_params(1),
        name="qgate_sample",
    )(x, *tables, win[0])


def _attn_sample_kernel(q_ref, kn_ref, vn_ref, ck_ref, cv_ref, sink_ref, o_ref, nk_ref, nv_ref):
    nb = q_ref.shape[0]
    head_of_lane = lax.broadcasted_iota(jnp.int32, (N_HEADS, D_MODEL), 1) // HEAD_DIM
    own_head = head_of_lane == lax.broadcasted_iota(jnp.int32, (N_HEADS, D_MODEL), 0)
    low_half = lax.broadcasted_iota(jnp.int32, (N_HEADS, LANES), 1) < HEAD_DIM
    last_row = lax.broadcasted_iota(jnp.int32, (WINDOW, KV_DIM), 0) == WINDOW - 1
    sink = sink_ref[...]
    heads_per_slab = LANES // HEAD_DIM
    slabs_per_group = GROUP // heads_per_slab

    def body(i, carry):
        newk = jnp.where(last_row, kn_ref[pl.ds(i, 1), :], pltpu.roll(ck_ref[i], WINDOW - 1, 0))
        newv = jnp.where(last_row, vn_ref[pl.ds(i, 1), :], pltpu.roll(cv_ref[i], WINDOW - 1, 0))
        nk_ref[i] = newk
        nv_ref[i] = newv
        qh = jnp.where(own_head, jnp.broadcast_to(q_ref[pl.ds(i, 1), :], (N_HEADS, D_MODEL)), 0.0)
        folded = []
        for kv in range(N_KV_HEADS):
            w = qh[:, kv * GROUP * HEAD_DIM:kv * GROUP * HEAD_DIM + LANES]
            for sl in range(1, slabs_per_group):
                c0 = kv * GROUP * HEAD_DIM + sl * LANES
                w = w + qh[:, c0:c0 + LANES]
            folded.append(w + pltpu.roll(w, HEAD_DIM, 1))
        qg = jnp.concatenate([jnp.where(low_half, folded[2 * j], folded[2 * j + 1])
                              for j in range(N_KV_HEADS // 2)], axis=1)
        s = lax.dot_general(qg.astype(BF16), newk.astype(BF16), (((1,), (1,)), ((), ())),
                            preferred_element_type=F32)
        m = jnp.maximum(jnp.max(s, axis=-1, keepdims=True), sink)
        p = jnp.exp(s - m)
        denom = jnp.sum(p, axis=-1, keepdims=True) + jnp.exp(sink - m)
        og = jnp.dot(p.astype(BF16), newv.astype(BF16), preferred_element_type=F32) / denom
        slabs = []
        for kv in range(N_KV_HEADS):
            xs = og[:, (kv // 2) * LANES:(kv // 2 + 1) * LANES]
            rolled = pltpu.roll(xs, HEAD_DIM, 1)
            both = jnp.where(low_half, xs, rolled) if kv % 2 == 0 else jnp.where(low_half, rolled, xs)
            slabs.extend([both] * slabs_per_group)
        full = jnp.concatenate(slabs, axis=1)
        o_ref[pl.ds(i, 1), :] = jnp.sum(jnp.where(own_head, full, 0.0), axis=0, keepdims=True)
        return carry

    lax.fori_loop(0, nb, body, 0)


def _attn_sample(q, kn, vn, ck, cv, sinks_col):
    R, D = q.shape
    nb = SAMPLE_ATTN_BATCH
    row2 = lambda w: pl.BlockSpec((nb, w), lambda i: (i, 0))
    cache = pl.BlockSpec((nb, WINDOW, KV_DIM), lambda i: (i, 0, 0))
    return pl.pallas_call(
        _attn_sample_kernel,
        grid=(R // nb,),
        in_specs=[row2(D), row2(KV_DIM), row2(KV_DIM), cache, cache, _resident(sinks_col.shape)],
        out_specs=[row2(D), cache, cache],
        out_shape=[jax.ShapeDtypeStruct((R, D), F32),
                   jax.ShapeDtypeStruct(ck.shape, F32), jax.ShapeDtypeStruct(cv.shape, F32)],
        compiler_params=_params(1),
        name="attn_sample",
    )(q, kn, vn, ck, cv, sinks_col)


def _gated_out_sample_kernel(x_ref, a_ref, gate_ref, wout_ref, g_ref, b_ref, o_ref, h_buf):
    rows = x_ref.shape[0]
    h_buf[...] = (a_ref[...] * _silu(gate_ref[...])).astype(BF16)

    def set_rows(rs, cs, v):
        o_ref[rs, cs] = v

    _outproj_ln(h_buf, lambda rs, cs: x_ref[rs, cs], wout_ref, g_ref, b_ref,
                set_rows, lambda rs, cs: o_ref[rs, cs], rows)


def _gated_out_sample(x, a, gate, wout, g, b):
    R, D = x.shape
    return pl.pallas_call(
        _gated_out_sample_kernel,
        grid=(1,),
        in_specs=[_resident(t.shape) for t in (x, a, gate)] + [wout[1], g[1], b[1]],
        out_specs=pl.BlockSpec((R, D), lambda i: (0, 0)),
        out_shape=jax.ShapeDtypeStruct((R, D), F32),
        scratch_shapes=[pltpu.VMEM((R, D), BF16)],
        compiler_params=_params(1),
        name="gated_out_sample",
    )(x, a, gate, wout[0], g[0], b[0])


def _rope_tables(pos):
    half = ROT_DIM // 2
    inv_freq = ROPE_THETA ** (-jnp.arange(0, ROT_DIM, 2, dtype=F32) / ROT_DIM)
    ang = pos.astype(F32)[:, None] * inv_freq[None, :]
    cos, sin = jnp.cos(ang), jnp.sin(ang)
    n = pos.shape[0]
    rest = jnp.zeros((n, HEAD_DIM - ROT_DIM), F32)
    zero = jnp.zeros((n, half), F32)
    cos_h = jnp.concatenate([cos, cos, rest + 1.0], axis=1)
    sa_h = jnp.concatenate([-sin, zero, rest], axis=1)
    sb_h = jnp.concatenate([zero, sin, rest], axis=1)
    rep = LANES // HEAD_DIM
    return tuple(jnp.tile(a, (1, rep)) for a in (cos_h, sa_h, sb_h))


def kernel(x_prompt, x_sample, state_pool, cache_k, cache_v, w_in_a, w_grp_a, scale_a, w_out_a,
           w_kv, w_in_b, sinks_b, w_out_b, ln_g, ln_b):
    B, S, D = x_prompt.shape
    R = x_sample.shape[0]
    xp = x_prompt
    xs = x_sample.reshape(R, D)
    tab_p = _rope_tables(jnp.arange(S, dtype=jnp.int32))
    tab_s = _rope_tables(jnp.full((R,), PAST_LEN, jnp.int32))
    qtab_p = tuple(a * SM_SCALE for a in tab_p)
    qtab_s = tuple(a * SM_SCALE for a in tab_s)
    w_in_a, w_grp_a, w_out_a, w_in_b, w_out_b = (
        w.astype(BF16) for w in (w_in_a, w_grp_a, w_out_a, w_in_b, w_out_b))
    scale_a, ln_g, ln_b = (p.reshape(p.shape[0], 1, D) for p in (scale_a, ln_g, ln_b))
    state_rows = state_pool.transpose(0, 2, 1, 3)
    pool_p, pool_s = [], []
    for i in range(N_A_LAYERS):
        params = [_layer(p, i) for p in (w_in_a, w_grp_a, scale_a, w_out_a, ln_g, ln_b)]
        xp, sp, xs, ss = _pool_layer(xp, xs, state_rows, i, params)
        pool_p.append(sp[:, HALO - POOL_STATE:])
        pool_s.append(ss)
    wkv = w_kv.astype(BF16)
    new_k_p, new_v_p, kdup_p, vt_p = _kv_prompt(xp, tab_p, wkv)
    k_s, v_s = _kv_sample(xs, tab_s, wkv)
    ck = cache_k.reshape(R, WINDOW, KV_DIM)
    cv = cache_v.reshape(R, WINDOW, KV_DIM)
    for j in range(DEPTH - N_A_LAYERS):
        i = N_A_LAYERS + j
        win, wout, g, b = _layer(w_in_b, j), _layer(w_out_b, j), _layer(ln_g, i), _layer(ln_b, i)
        xp = _attn_prompt(xp, qtab_p, kdup_p, vt_p, win, sinks_b[j], wout, g, b)
        q_s, gate_s = _qgate_sample(xs, qtab_s, win)
        a_s, nk, nv = _attn_sample(q_s, k_s, v_s, ck, cv, sinks_b[j][:, None])
        xs = _gated_out_sample(xs, a_s, gate_s, wout, g, b)
    kv4 = (N_KV_HEADS, HEAD_DIM)
    return (xp, xs.reshape(R, 1, D), jnp.stack(pool_p, axis=0), jnp.stack(pool_s, axis=0).transpose(0, 2, 1, 3),
            new_k_p.reshape(B, WINDOW, *kv4), new_v_p.reshape(B, WINDOW, *kv4),
            nk.reshape(R, WINDOW, *kv4), nv.reshape(R, WINDOW, *kv4))
```

```python
import functools

import jax
import jax.numpy as jnp
from jax import lax
from jax.experimental import pallas as pl
from jax.experimental.pallas import tpu as pltpu

F32 = jnp.float32
BF16 = jnp.bfloat16

D_MODEL = 2048
DEPTH = 4
PAST_LEN = 16384
N_A_LAYERS = DEPTH // 2
POOL_WINDOWS = (2, 4, 8, 16)
POOL_GROUP = D_MODEL // len(POOL_WINDOWS)
POOL_STATE = max(POOL_WINDOWS) - 1
HEAD_DIM = 64
N_HEADS = D_MODEL // HEAD_DIM
N_KV_HEADS = N_HEADS // 8
GROUP = N_HEADS // N_KV_HEADS
KV_DIM = N_KV_HEADS * HEAD_DIM
WINDOW = 128
ROT_DIM = HEAD_DIM // 4
ROPE_THETA = 500000.0
ALPHA = (2 * DEPTH) ** 0.25
LN_EPS = 1e-5
NEG = -1e30
SM_SCALE = HEAD_DIM ** -0.5

LANES = 128
SUBLANES = 8
HALO = 16
COL_CHUNK = 512
N_CHUNKS = D_MODEL // COL_CHUNK
TM = 256
TK = 512
LN_ROWS = 32
SAMPLE_ATTN_BATCH = 8
SUM_ROWS = 16
VMEM_LIMIT_BYTES = 56 * 1024 * 1024


def _params(n_axes):
    return pltpu.CompilerParams(dimension_semantics=("arbitrary",) * n_axes,
                                vmem_limit_bytes=VMEM_LIMIT_BYTES)


def _resident(shape):
    zeros = (0,) * len(shape)
    return pl.BlockSpec(shape, lambda *_: zeros, pipeline_mode=pl.Buffered(1))


def _layer(stacked, i):
    zeros = (0,) * (stacked.ndim - 1)
    return stacked, pl.BlockSpec((None,) + stacked.shape[1:], lambda *_: (i,) + zeros,
                                 pipeline_mode=pl.Buffered(1))


def _silu(g):
    return g / (1.0 + jnp.exp(-g))


def _rope(x, cos, sa, sb):
    return x * cos + pltpu.roll(x, LANES - ROT_DIM // 2, 1) * sa + pltpu.roll(x, ROT_DIM // 2, 1) * sb


def _outproj_residual(h_ref, x_rows, wout_ref, r_rows_set, rows):
    for n in range(N_CHUNKS):
        cols = slice(n * COL_CHUNK, (n + 1) * COL_CHUNK)
        y = jnp.dot(h_ref[0:rows, :], wout_ref[:, cols], preferred_element_type=F32)
        r_rows_set(slice(0, rows), cols, ALPHA * x_rows(slice(0, rows), cols) + y)


def _zero_after(v):
    bits = pltpu.bitcast(v, jnp.int32)
    half = jnp.full(bits.shape, 16, jnp.int32)
    return lax.shift_right_logical(lax.shift_right_logical(bits, half), half).astype(F32)


def _layer_norm(r_rows_get, o_rows_set, g_ref, b_ref, rows):
    step = min(LN_ROWS, rows)
    anchors = []
    for r0 in range(0, rows, step):
        rs = slice(r0, r0 + step)
        r = r_rows_get(rs, slice(None))
        mu = jnp.mean(r, axis=-1, keepdims=True)
        c = r - mu
        var = jnp.mean(c * c, axis=-1, keepdims=True)
        out = c * lax.rsqrt(var + LN_EPS) * g_ref[...] + b_ref[...]
        o_rows_set(rs, slice(None), out)
        folded = sum(out[i:i + SUBLANES, j:j + LANES]
                     for i in range(0, step, SUBLANES) for j in range(0, out.shape[1], LANES))
        anchors.append(_zero_after(folded))
    return anchors


def _outproj_ln(h_ref, x_rows, wout_ref, g_ref, b_ref, o_rows_set, o_rows_get, rows):
    _outproj_residual(h_ref, x_rows, wout_ref, o_rows_set, rows)
    _layer_norm(o_rows_get, o_rows_set, g_ref, b_ref, rows)


def _pool_layer_kernel(layer, tiles_per_batch,
                       x_ref, xs_ref, st_hbm, win_ref, wgrp_ref, scale_ref, wout_ref, g_ref, b_ref,
                       o_ref, state_ref, ys_ref, nst_hbm,
                       xb_buf, u_buf, h_buf, r_buf, st_buf, us_buf, sems):
    step = pl.program_id(0)
    n_prompt = pl.num_programs(0) - 1
    rows_s = xs_ref.shape[0]

    def mix(xb, d, g):
        cols = slice(g * POOL_GROUP, (g + 1) * POOL_GROUP)
        gate = jnp.dot(xb, win_ref[:, D_MODEL + g * POOL_GROUP:D_MODEL + (g + 1) * POOL_GROUP],
                       preferred_element_type=F32)
        d = jnp.dot(d.astype(BF16), wgrp_ref[g], preferred_element_type=F32) * scale_ref[:, cols]
        return (d * _silu(gate)).astype(BF16)

    def set_out(rs, cs, v):
        o_ref[0, rs, cs] = v

    def set_r(rs, cs, v):
        r_buf[rs, cs] = v

    def norm_previous_tile():
        return _layer_norm(lambda rs, cs: r_buf[rs, cs], set_out, g_ref, b_ref, TM)

    @pl.when(step == 0)
    def _():
        r_buf[...] = jnp.zeros(r_buf.shape, F32)

    t = step % tiles_per_batch

    @pl.when(t == 0)
    def _():
        u_buf[0:HALO, :] = jnp.zeros((HALO, D_MODEL), F32)

    xb_buf[...] = x_ref[0].astype(BF16)
    row = lax.broadcasted_iota(jnp.int32, (TM, 1), 0) + t * TM

    def project_u(g):
        cols = slice(g * POOL_GROUP, (g + 1) * POOL_GROUP)
        u_buf[HALO:, cols] = jnp.dot(xb_buf[...], win_ref[:, cols], preferred_element_type=F32)

    project_u(0)
    anchors = norm_previous_tile()
    per_group = len(anchors) // len(POOL_WINDOWS)
    for g, w in enumerate(POOL_WINDOWS):
        cols = slice(g * POOL_GROUP, (g + 1) * POOL_GROUP)
        if g + 1 < len(POOL_WINDOWS):
            project_u(g + 1)
        ext = u_buf[:, cols]
        s = ext
        shift = 1
        while shift < w:
            s = s + pltpu.roll(s, shift, 0)
            shift *= 2
        inv_cnt = 1.0 / jnp.minimum(w, row + 1).astype(F32)
        anchor = sum(anchors[g * per_group:(g + 1) * per_group])
        anchor = jnp.tile(anchor, (TM // anchor.shape[0], POOL_GROUP // anchor.shape[1]))
        h_buf[:, cols] = mix(xb_buf[...], s[HALO:, :] * inv_cnt - ext[HALO:, :] + anchor, g)

    state_ref[0] = u_buf[TM:TM + HALO, :]
    u_buf[0:HALO, :] = u_buf[TM:TM + HALO, :]
    _outproj_residual(h_buf, lambda rs, cs: x_ref[0, rs, cs], wout_ref, set_r, TM)

    @pl.when(step == n_prompt)
    def _():
        load = pltpu.make_async_copy(st_hbm.at[layer], st_buf, sems.at[0])
        shift_old = pltpu.make_async_copy(st_buf.at[pl.ds(1, POOL_STATE - 1)],
                                          nst_hbm.at[pl.ds(0, POOL_STATE - 1)], sems.at[1])
        append_new = pltpu.make_async_copy(us_buf, nst_hbm.at[POOL_STATE - 1], sems.at[2])
        load.start()
        xb = xs_ref[...].astype(BF16)
        load.wait()
        shift_old.start()
        for g, w in enumerate(POOL_WINDOWS):
            cols = slice(g * POOL_GROUP, (g + 1) * POOL_GROUP)
            u = jnp.dot(xb, win_ref[:, cols], preferred_element_type=F32)
            us_buf[:, cols] = u
            acc = u
            for j in range(1, w):
                acc = acc + st_buf[POOL_STATE - j, :, cols]
            h_buf[0:rows_s, cols] = mix(xb, acc * (1.0 / min(w, PAST_LEN + 1)) - u, g)
        append_new.start()

        def set_ys(rs, cs, v):
            ys_ref[rs, cs] = v

        _outproj_ln(h_buf, lambda rs, cs: xs_ref[rs, cs], wout_ref, g_ref, b_ref,
                    set_ys, lambda rs, cs: ys_ref[rs, cs], rows_s)
        shift_old.wait()
        append_new.wait()


def _pool_layer(x, xs, state_rows, layer, params):
    B, S, D = x.shape
    R = xs.shape[0]
    tiles_per_batch = S // TM
    n_prompt = B * tiles_per_batch

    def tile(step):
        step = jnp.clip(step, 0, n_prompt - 1)
        return step // tiles_per_batch, step % tiles_per_batch

    return pl.pallas_call(
        functools.partial(_pool_layer_kernel, layer, tiles_per_batch),
        grid=(n_prompt + 1,),
        in_specs=[pl.BlockSpec((1, TM, D), lambda i: (*tile(i), 0)), _resident(xs.shape),
                  pl.BlockSpec(memory_space=pl.ANY)] + [spec for _, spec in params],
        out_specs=[
            pl.BlockSpec((1, TM, D), lambda i: (*tile(i - 1), 0)),
            pl.BlockSpec((1, HALO, D), lambda i: (tile(i)[0], 0, 0)),
            pl.BlockSpec((R, D), lambda i: (0, 0)),
            pl.BlockSpec(memory_space=pl.ANY),
        ],
        out_shape=[jax.ShapeDtypeStruct((B, S, D), F32), jax.ShapeDtypeStruct((B, HALO, D), F32),
                   jax.ShapeDtypeStruct((R, D), F32), jax.ShapeDtypeStruct((POOL_STATE, R, D), F32)],
        scratch_shapes=[pltpu.VMEM((TM, D), BF16), pltpu.VMEM((HALO + TM, D), F32), pltpu.VMEM((TM, D), BF16),
                        pltpu.VMEM((TM, D), F32), pltpu.VMEM((POOL_STATE, R, D), F32), pltpu.VMEM((R, D), F32),
                        pltpu.SemaphoreType.DMA((3,))],
        compiler_params=_params(1),
        name="pool_layer",
    )(x, xs, state_rows, *[a for a, _ in params])


def _project_kv(xb, cos, sa, sb, wkv_ref):
    kv = jnp.dot(xb, wkv_ref[...], preferred_element_type=F32)
    k_slabs = [_rope(kv[:, j * LANES:(j + 1) * LANES], cos, sa, sb) for j in range(KV_DIM // LANES)]
    return k_slabs, kv[:, KV_DIM:]


def _kv_prompt_kernel(x_ref, cos_ref, sa_ref, sb_ref, wkv_ref, knew_ref, vnew_ref, kdup_ref, vt_ref):
    k_slabs, v = _project_kv(x_ref[0].astype(BF16), cos_ref[...], sa_ref[...], sb_ref[...], wkv_ref)
    low = lax.broadcasted_iota(jnp.int32, (TK, LANES), 1) < HEAD_DIM
    for j, k in enumerate(k_slabs):
        swapped = pltpu.roll(k, HEAD_DIM, 1)
        kdup_ref[0, :, (2 * j) * LANES:(2 * j + 1) * LANES] = jnp.where(low, k, swapped).astype(BF16)
        kdup_ref[0, :, (2 * j + 1) * LANES:(2 * j + 2) * LANES] = jnp.where(low, swapped, k).astype(BF16)
    for i in range(TK // WINDOW):
        vt_ref[0, i] = v[i * WINDOW:(i + 1) * WINDOW, :].T.astype(BF16)

    @pl.when(pl.program_id(1) == pl.num_programs(1) - 1)
    def _():
        for j, k in enumerate(k_slabs):
            knew_ref[0, :, j * LANES:(j + 1) * LANES] = k[TK - WINDOW:, :]
        vnew_ref[0] = v[TK - WINDOW:, :]


def _kv_prompt(x, tables, wkv):
    B, S, D = x.shape
    tab = pl.BlockSpec((TK, LANES), lambda bi, t: (t, 0))
    last = pl.BlockSpec((1, WINDOW, KV_DIM), lambda bi, t: (bi, 0, 0))
    return pl.pallas_call(
        _kv_prompt_kernel,
        grid=(B, S // TK),
        in_specs=[pl.BlockSpec((1, TK, D), lambda bi, t: (bi, t, 0)), tab, tab, tab, _resident(wkv.shape)],
        out_specs=[last, last,
                   pl.BlockSpec((1, TK, N_KV_HEADS * LANES), lambda bi, t: (bi, t, 0)),
                   pl.BlockSpec((1, TK // WINDOW, KV_DIM, WINDOW), lambda bi, t: (bi, t, 0, 0))],
        out_shape=[jax.ShapeDtypeStruct((B, WINDOW, KV_DIM), F32), jax.ShapeDtypeStruct((B, WINDOW, KV_DIM), F32),
                   jax.ShapeDtypeStruct((B, S, N_KV_HEADS * LANES), BF16),
                   jax.ShapeDtypeStruct((B, S // WINDOW, KV_DIM, WINDOW), BF16)],
        compiler_params=_params(2),
        name="kv_prompt",
    )(x, *tables, wkv)


def _kv_sample_kernel(x_ref, cos_ref, sa_ref, sb_ref, wkv_ref, k_ref, v_ref):
    k_slabs, v = _project_kv(x_ref[...].astype(BF16), cos_ref[...], sa_ref[...], sb_ref[...], wkv_ref)
    for j, k in enumerate(k_slabs):
        k_ref[:, j * LANES:(j + 1) * LANES] = k
    v_ref[...] = v


def _kv_sample(x, tables, wkv):
    R, D = x.shape
    out = pl.BlockSpec((R, KV_DIM), lambda i: (0, 0))
    return pl.pallas_call(
        _kv_sample_kernel,
        grid=(1,),
        in_specs=[_resident(x.shape)] + [_resident(t.shape) for t in tables] + [_resident(wkv.shape)],
        out_specs=[out, out],
        out_shape=[jax.ShapeDtypeStruct((R, KV_DIM), F32), jax.ShapeDtypeStruct((R, KV_DIM), F32)],
        compiler_params=_params(1),
        name="kv_sample",
    )(x, *tables, wkv)


def _attn_prompt_kernel(x_ref, cos_ref, sa_ref, sb_ref, kdup_ref, vt_ref, win_ref, sink_ref, wout_ref,
                        g_ref, b_ref, o_ref, xb_buf, q_buf, h_buf, g_buf):
    t = pl.program_id(1)
    group_cols = GROUP * HEAD_DIM
    xb_buf[...] = x_ref[0].astype(BF16)

    def project_q(kv):
        q = jnp.dot(xb_buf[...], win_ref[:, kv * group_cols:(kv + 1) * group_cols], preferred_element_type=F32)
        for j in range(group_cols // LANES):
            qj = _rope(q[:, j * LANES:(j + 1) * LANES], cos_ref[...], sa_ref[...], sb_ref[...])
            c0 = kv * group_cols + j * LANES
            q_buf[:, c0:c0 + LANES] = qj.astype(BF16)

    def project_gate(kv):
        cols = slice(kv * group_cols, (kv + 1) * group_cols)
        gate = jnp.dot(xb_buf[...], win_ref[:, D_MODEL + kv * group_cols:D_MODEL + (kv + 1) * group_cols],
                       preferred_element_type=F32)
        g_buf[:, cols] = _silu(gate)

    key = lax.broadcasted_iota(jnp.int32, (2 * WINDOW, WINDOW), 0)
    qry = lax.broadcasted_iota(jnp.int32, (2 * WINDOW, WINDOW), 1)
    band = (key > qry) & (key <= qry + WINDOW)
    low_half = lax.broadcasted_iota(jnp.int32, (WINDOW, LANES), 1) < HEAD_DIM
    ones_rows = jnp.ones((SUM_ROWS, 2 * WINDOW), BF16)

    def block_ids(qb):
        blk = t * (TM // WINDOW) + qb
        return blk, jnp.maximum(blk - 1, 0)

    def group_heads(kv):
        return [(kv * (GROUP // 2) + pair, par) for pair in range(GROUP // 2) for par in range(2)]

    def scores(qb, kv):
        blk, prev_blk = block_ids(qb)
        prev = pl.multiple_of(prev_blk * WINDOW, WINDOW)
        cur = pl.multiple_of(blk * WINDOW, WINDOW)
        rows = slice(qb * WINDOW, (qb + 1) * WINDOW)
        ks = slice(kv * LANES, (kv + 1) * LANES)
        k2 = jnp.concatenate([kdup_ref[0, pl.ds(prev, WINDOW), ks], kdup_ref[0, pl.ds(cur, WINDOW), ks]], axis=0)
        q_all = []
        for slab, par in group_heads(kv):
            q_slab = q_buf[rows, slab * LANES:(slab + 1) * LANES]
            q_all.append(jnp.where(low_half == (par == 0), q_slab, jnp.zeros_like(q_slab)))
        return lax.dot_general(k2, jnp.concatenate(q_all, axis=0), (((1,), (1,)), ((), ())),
                               preferred_element_type=F32)

    def finish(qb, kv, s_t):
        blk, prev_blk = block_ids(qb)
        vis = band & ((key >= WINDOW) | (blk > 0))
        rows = slice(qb * WINDOW, (qb + 1) * WINDOW)
        vs = slice(kv * HEAD_DIM, (kv + 1) * HEAD_DIM)
        v_aug = jnp.concatenate([vt_ref[0, prev_blk, vs, :], vt_ref[0, blk, vs, :]], axis=1)
        v_aug = jnp.concatenate([v_aug, ones_rows], axis=0)
        p_t, sink_terms = [], []
        for i, (slab, par) in enumerate(group_heads(kv)):
            s = jnp.where(vis, s_t[:, i * WINDOW:(i + 1) * WINDOW], NEG)
            sink = sink_ref[2 * slab + par]
            m = jnp.maximum(jnp.max(s, axis=0, keepdims=True), sink)
            p_t.append(jnp.exp(s - m).astype(BF16))
            sink_terms.append(jnp.exp(sink - m))
        o_t = jnp.dot(v_aug, jnp.concatenate(p_t, axis=1), preferred_element_type=F32)
        for pair in range(GROUP // 2):
            both = []
            for par in range(2):
                i = 2 * pair + par
                cs = slice(i * WINDOW, (i + 1) * WINDOW)
                inv = 1.0 / (o_t[HEAD_DIM:HEAD_DIM + 1, cs] + sink_terms[i])
                both.append(o_t[:HEAD_DIM, cs] * inv)
            slab = kv * (GROUP // 2) + pair
            attn = jnp.concatenate(both, axis=0).T
            cs = slice(slab * LANES, (slab + 1) * LANES)
            h_buf[rows, cs] = (attn * g_buf[rows, cs]).astype(BF16)

    project_q(0)
    for kv in range(N_KV_HEADS):
        s_t = [scores(qb, kv) for qb in range(TM // WINDOW)]
        if kv + 1 < N_KV_HEADS:
            project_q(kv + 1)
        project_gate(kv)
        for qb in range(TM // WINDOW):
            finish(qb, kv, s_t[qb])

    def set_rows(rs, cs, v):
        o_ref[0, rs, cs] = v

    _outproj_ln(h_buf, lambda rs, cs: x_ref[0, rs, cs], wout_ref, g_ref, b_ref,
                set_rows, lambda rs, cs: o_ref[0, rs, cs], TM)


def _attn_prompt(x, tables, kdup, vt, win, sinks, wout, g, b):
    B, S, D = x.shape
    tab = pl.BlockSpec((TM, LANES), lambda bi, t: (t, 0))

    def per_batch(a):
        return pl.BlockSpec((1,) + a.shape[1:], lambda bi, t: (bi,) + (0,) * (a.ndim - 1),
                            pipeline_mode=pl.Buffered(1))

    return pl.pallas_call(
        _attn_prompt_kernel,
        grid=(B, S // TM),
        in_specs=[
            pl.BlockSpec((1, TM, D), lambda bi, t: (bi, t, 0)), tab, tab, tab, per_batch(kdup), per_batch(vt),
            win[1], pl.BlockSpec(memory_space=pltpu.SMEM), wout[1], g[1], b[1],
        ],
        out_specs=pl.BlockSpec((1, TM, D), lambda bi, t: (bi, t, 0)),
        out_shape=jax.ShapeDtypeStruct((B, S, D), F32),
        scratch_shapes=[pltpu.VMEM((TM, D), BF16), pltpu.VMEM((TM, D), BF16), pltpu.VMEM((TM, D), BF16),
                        pltpu.VMEM((TM, D), F32)],
        compiler_params=_params(2),
        name="attn_prompt",
    )(x, *tables, kdup, vt, win[0], sinks, wout[0], g[0], b[0])


def _qgate_sample_kernel(x_ref, cos_ref, sa_ref, sb_ref, win_ref, q_ref, gate_ref):
    xb = x_ref[...].astype(BF16)
    for n in range(N_CHUNKS):
        q = jnp.dot(xb, win_ref[:, n * COL_CHUNK:(n + 1) * COL_CHUNK], preferred_element_type=F32)
        for j in range(COL_CHUNK // LANES):
            c0 = n * COL_CHUNK + j * LANES
            q_ref[:, c0:c0 + LANES] = _rope(q[:, j * LANES:(j + 1) * LANES],
                                            cos_ref[...], sa_ref[...], sb_ref[...])
        gate_ref[:, n * COL_CHUNK:(n + 1) * COL_CHUNK] = jnp.dot(
            xb, win_ref[:, D_MODEL + n * COL_CHUNK:D_MODEL + (n + 1) * COL_CHUNK],
            preferred_element_type=F32)


def _qgate_sample(x, tables, win):
    R, D = x.shape
    return pl.pallas_call(
        _qgate_sample_kernel,
        grid=(1,),
        in_specs=[_resident(x.shape)] + [_resident(t.shape) for t in tables] + [win[1]],
        out_specs=[pl.BlockSpec((R, D), lambda i: (0, 0)), pl.BlockSpec((R, D), lambda i: (0, 0))],
        out_shape=[jax.ShapeDtypeStruct((R, D), F32), jax.ShapeDtypeStruct((R, D), F32)],
        compiler_params=_params(1),
        name="qgate_sample",
    )(x, *tables, win[0])


def _attn_sample_kernel(q_ref, kn_ref, vn_ref, ck_ref, cv_ref, sink_ref, o_ref, nk_ref, nv_ref):
    nb = q_ref.shape[0]
    head_of_lane = lax.broadcasted_iota(jnp.int32, (N_HEADS, D_MODEL), 1) // HEAD_DIM
    own_head = head_of_lane == lax.broadcasted_iota(jnp.int32, (N_HEADS, D_MODEL), 0)
    low_half = lax.broadcasted_iota(jnp.int32, (N_HEADS, LANES), 1) < HEAD_DIM
    last_row = lax.broadcasted_iota(jnp.int32, (WINDOW, KV_DIM), 0) == WINDOW - 1
    sink = sink_ref[...]
    heads_per_slab = LANES // HEAD_DIM
    slabs_per_group = GROUP // heads_per_slab

    def body(i, carry):
        newk = jnp.where(last_row, kn_ref[pl.ds(i, 1), :], pltpu.roll(ck_ref[i], WINDOW - 1, 0))
        newv = jnp.where(last_row, vn_ref[pl.ds(i, 1), :], pltpu.roll(cv_ref[i], WINDOW - 1, 0))
        nk_ref[i] = newk
        nv_ref[i] = newv
        qh = jnp.where(own_head, jnp.broadcast_to(q_ref[pl.ds(i, 1), :], (N_HEADS, D_MODEL)), 0.0)
        folded = []
        for kv in range(N_KV_HEADS):
            w = qh[:, kv * GROUP * HEAD_DIM:kv * GROUP * HEAD_DIM + LANES]
            for sl in range(1, slabs_per_group):
                c0 = kv * GROUP * HEAD_DIM + sl * LANES
                w = w + qh[:, c0:c0 + LANES]
            folded.append(w + pltpu.roll(w, HEAD_DIM, 1))
        qg = jnp.concatenate([jnp.where(low_half, folded[2 * j], folded[2 * j + 1])
                              for j in range(N_KV_HEADS // 2)], axis=1)
        s = lax.dot_general(qg.astype(BF16), newk.astype(BF16), (((1,), (1,)), ((), ())),
                            preferred_element_type=F32)
        m = jnp.maximum(jnp.max(s, axis=-1, keepdims=True), sink)
        p = jnp.exp(s - m)
        denom = jnp.sum(p, axis=-1, keepdims=True) + jnp.exp(sink - m)
        og = jnp.dot(p.astype(BF16), newv.astype(BF16), preferred_element_type=F32) / denom
        slabs = []
        for kv in range(N_KV_HEADS):
            xs = og[:, (kv // 2) * LANES:(kv // 2 + 1) * LANES]
            rolled = pltpu.roll(xs, HEAD_DIM, 1)
            both = jnp.where(low_half, xs, rolled) if kv % 2 == 0 else jnp.where(low_half, rolled, xs)
            slabs.extend([both] * slabs_per_group)
        full = jnp.concatenate(slabs, axis=1)
        o_ref[pl.ds(i, 1), :] = jnp.sum(jnp.where(own_head, full, 0.0), axis=0, keepdims=True)
        return carry

    lax.fori_loop(0, nb, body, 0)


def _attn_sample(q, kn, vn, ck, cv, sinks_col):
    R, D = q.shape
    nb = SAMPLE_ATTN_BATCH
    row2 = lambda w: pl.BlockSpec((nb, w), lambda i: (i, 0))
    cache = pl.BlockSpec((nb, WINDOW, KV_DIM), lambda i: (i, 0, 0))
    return pl.pallas_call(
        _attn_sample_kernel,
        grid=(R // nb,),
        in_specs=[row2(D), row2(KV_DIM), row2(KV_DIM), cache, cache, _resident(sinks_col.shape)],
        out_specs=[row2(D), cache, cache],
        out_shape=[jax.ShapeDtypeStruct((R, D), F32),
                   jax.ShapeDtypeStruct(ck.shape, F32), jax.ShapeDtypeStruct(cv.shape, F32)],
        compiler_params=_params(1),
        name="attn_sample",
    )(q, kn, vn, ck, cv, sinks_col)


def _gated_out_sample_kernel(x_ref, a_ref, gate_ref, wout_ref, g_ref, b_ref, o_ref, h_buf):
    rows = x_ref.shape[0]
    h_buf[...] = (a_ref[...] * _silu(gate_ref[...])).astype(BF16)

    def set_rows(rs, cs, v):
        o_ref[rs, cs] = v

    _outproj_ln(h_buf, lambda rs, cs: x_ref[rs, cs], wout_ref, g_ref, b_ref,
                set_rows, lambda rs, cs: o_ref[rs, cs], rows)


def _gated_out_sample(x, a, gate, wout, g, b):
    R, D = x.shape
    return pl.pallas_call(
        _gated_out_sample_kernel,
        grid=(1,),
        in_specs=[_resident(t.shape) for t in (x, a, gate)] + [wout[1], g[1], b[1]],
        out_specs=pl.BlockSpec((R, D), lambda i: (0, 0)),
        out_shape=jax.ShapeDtypeStruct((R, D), F32),
        scratch_shapes=[pltpu.VMEM((R, D), BF16)],
        compiler_params=_params(1),
        name="gated_out_sample",
    )(x, a, gate, wout[0], g[0], b[0])


def _rope_tables(pos):
    half = ROT_DIM // 2
    inv_freq = ROPE_THETA ** (-jnp.arange(0, ROT_DIM, 2, dtype=F32) / ROT_DIM)
    ang = pos.astype(F32)[:, None] * inv_freq[None, :]
    cos, sin = jnp.cos(ang), jnp.sin(ang)
    n = pos.shape[0]
    rest = jnp.zeros((n, HEAD_DIM - ROT_DIM), F32)
    zero = jnp.zeros((n, half), F32)
    cos_h = jnp.concatenate([cos, cos, rest + 1.0], axis=1)
    sa_h = jnp.concatenate([-sin, zero, rest], axis=1)
    sb_h = jnp.concatenate([zero, sin, rest], axis=1)
    rep = LANES // HEAD_DIM
    return tuple(jnp.tile(a, (1, rep)) for a in (cos_h, sa_h, sb_h))


def kernel(x_prompt, x_sample, state_pool, cache_k, cache_v, w_in_a, w_grp_a, scale_a, w_out_a,
           w_kv, w_in_b, sinks_b, w_out_b, ln_g, ln_b):
    B, S, D = x_prompt.shape
    R = x_sample.shape[0]
    xp = x_prompt
    xs = x_sample.reshape(R, D)
    tab_p = _rope_tables(jnp.arange(S, dtype=jnp.int32))
    tab_s = _rope_tables(jnp.full((R,), PAST_LEN, jnp.int32))
    qtab_p = tuple(a * SM_SCALE for a in tab_p)
    qtab_s = tuple(a * SM_SCALE for a in tab_s)
    w_in_a, w_grp_a, w_out_a, w_in_b, w_out_b = (
        w.astype(BF16) for w in (w_in_a, w_grp_a, w_out_a, w_in_b, w_out_b))
    scale_a, ln_g, ln_b = (p.reshape(p.shape[0], 1, D) for p in (scale_a, ln_g, ln_b))
    state_rows = state_pool.transpose(0, 2, 1, 3)
    pool_p, pool_s = [], []
    for i in range(N_A_LAYERS):
        params = [_layer(p, i) for p in (w_in_a, w_grp_a, scale_a, w_out_a, ln_g, ln_b)]
        xp, sp, xs, ss = _pool_layer(xp, xs, state_rows, i, params)
        pool_p.append(sp[:, HALO - POOL_STATE:])
        pool_s.append(ss)
    wkv = w_kv.astype(BF16)
    new_k_p, new_v_p, kdup_p, vt_p = _kv_prompt(xp, tab_p, wkv)
    k_s, v_s = _kv_sample(xs, tab_s, wkv)
    ck = cache_k.reshape(R, WINDOW, KV_DIM)
    cv = cache_v.reshape(R, WINDOW, KV_DIM)
    for j in range(DEPTH - N_A_LAYERS):
        i = N_A_LAYERS + j
        win, wout, g, b = _layer(w_in_b, j), _layer(w_out_b, j), _layer(ln_g, i), _layer(ln_b, i)
        xp = _attn_prompt(xp, qtab_p, kdup_p, vt_p, win, sinks_b[j], wout, g, b)
        q_s, gate_s = _qgate_sample(xs, qtab_s, win)
        a_s, nk, nv = _attn_sample(q_s, k_s, v_s, ck, cv, sinks_b[j][:, None])
        xs = _gated_out_sample(xs, a_s, gate_s, wout, g, b)
    kv4 = (N_KV_HEADS, HEAD_DIM)
    return (xp, xs.reshape(R, 1, D), jnp.stack(pool_p, axis=0), jnp.stack(pool_s, axis=0).transpose(0, 2, 1, 3),
            new_k_p.reshape(B, WINDOW, *kv4), new_v_p.reshape(B, WINDOW, *kv4),
            nk.reshape(R, WINDOW, *kv4), nv.reshape(R, WINDOW, *kv4))
```

```python
import functools

import jax
import jax.numpy as jnp
from jax import lax
from jax.experimental import pallas as pl
from jax.experimental.pallas import tpu as pltpu

F32 = jnp.float32
BF16 = jnp.bfloat16

D_MODEL = 2048
DEPTH = 4
PAST_LEN = 16384
N_A_LAYERS = DEPTH // 2
POOL_WINDOWS = (2, 4, 8, 16)
POOL_GROUP = D_MODEL // len(POOL_WINDOWS)
POOL_STATE = max(POOL_WINDOWS) - 1
HEAD_DIM = 64
N_HEADS = D_MODEL // HEAD_DIM
N_KV_HEADS = N_HEADS // 8
GROUP = N_HEADS // N_KV_HEADS
KV_DIM = N_KV_HEADS * HEAD_DIM
WINDOW = 128
ROT_DIM = HEAD_DIM // 4
ROPE_THETA = 500000.0
ALPHA = (2 * DEPTH) ** 0.25
LN_EPS = 1e-5
NEG = -1e30
SM_SCALE = HEAD_DIM ** -0.5

LANES = 128
SUBLANES = 8
HALO = 16
COL_CHUNK = 512
N_CHUNKS = D_MODEL // COL_CHUNK
TM = 256
TK = 512
LN_ROWS = 16
SAMPLE_ATTN_BATCH = 8
SUM_ROWS = 16
VMEM_LIMIT_BYTES = 56 * 1024 * 1024


def _params(n_axes):
    return pltpu.CompilerParams(dimension_semantics=("arbitrary",) * n_axes,
                                vmem_limit_bytes=VMEM_LIMIT_BYTES)


def _resident(shape):
    zeros = (0,) * len(shape)
    return pl.BlockSpec(shape, lambda *_: zeros, pipeline_mode=pl.Buffered(1))


def _layer(stacked, i):
    zeros = (0,) * (stacked.ndim - 1)
    return stacked, pl.BlockSpec((None,) + stacked.shape[1:], lambda *_: (i,) + zeros,
                                 pipeline_mode=pl.Buffered(1))


def _silu(g):
    return g / (1.0 + jnp.exp(-g))


def _rope(x, cos, sa, sb):
    return x * cos + pltpu.roll(x, LANES - ROT_DIM // 2, 1) * sa + pltpu.roll(x, ROT_DIM // 2, 1) * sb


def _outproj_residual(h_ref, x_rows, wout_ref, r_rows_set, rows):
    for n in range(N_CHUNKS):
        cols = slice(n * COL_CHUNK, (n + 1) * COL_CHUNK)
        y = jnp.dot(h_ref[0:rows, :], wout_ref[:, cols], preferred_element_type=F32)
        r_rows_set(slice(0, rows), cols, ALPHA * x_rows(slice(0, rows), cols) + y)


def _zero_after(v):
    bits = pltpu.bitcast(v, jnp.int32)
    half = jnp.full(bits.shape, 16, jnp.int32)
    return lax.shift_right_logical(lax.shift_right_logical(bits, half), half).astype(F32)


def _layer_norm(r_rows_get, o_rows_set, g_ref, b_ref, rows):
    step = min(LN_ROWS, rows)
    anchors = []
    for r0 in range(0, rows, step):
        rs = slice(r0, r0 + step)
        r = r_rows_get(rs, slice(None))
        mu = jnp.mean(r, axis=-1, keepdims=True)
        c = r - mu
        var = jnp.mean(c * c, axis=-1, keepdims=True)
        out = c * lax.rsqrt(var + LN_EPS) * g_ref[...] + b_ref[...]
        o_rows_set(rs, slice(None), out)
        folded = sum(out[i:i + SUBLANES, j:j + LANES]
                     for i in range(0, step, SUBLANES) for j in range(0, out.shape[1], LANES))
        anchors.append(_zero_after(folded))
    return anchors


def _outproj_ln(h_ref, x_rows, wout_ref, g_ref, b_ref, o_rows_set, o_rows_get, rows):
    _outproj_residual(h_ref, x_rows, wout_ref, o_rows_set, rows)
    _layer_norm(o_rows_get, o_rows_set, g_ref, b_ref, rows)


def _pool_layer_kernel(layer, tiles_per_batch,
                       x_ref, xs_ref, st_hbm, win_ref, wgrp_ref, scale_ref, wout_ref, g_ref, b_ref,
                       o_ref, state_ref, ys_ref, nst_hbm,
                       xb_buf, u_buf, h_buf, r_buf, st_buf, us_buf, sems):
    step = pl.program_id(0)
    n_prompt = pl.num_programs(0) - 1
    rows_s = xs_ref.shape[0]

    def mix(xb, d, g):
        cols = slice(g * POOL_GROUP, (g + 1) * POOL_GROUP)
        gate = jnp.dot(xb, win_ref[:, D_MODEL + g * POOL_GROUP:D_MODEL + (g + 1) * POOL_GROUP],
                       preferred_element_type=F32)
        d = jnp.dot(d.astype(BF16), wgrp_ref[g], preferred_element_type=F32) * scale_ref[:, cols]
        return (d * _silu(gate)).astype(BF16)

    def set_out(rs, cs, v):
        o_ref[0, rs, cs] = v

    def set_r(rs, cs, v):
        r_buf[rs, cs] = v

    def norm_previous_tile():
        return _layer_norm(lambda rs, cs: r_buf[rs, cs], set_out, g_ref, b_ref, TM)

    @pl.when(step == 0)
    def _():
        r_buf[...] = jnp.zeros(r_buf.shape, F32)

    @pl.when(step < n_prompt)
    def _():
        t = step % tiles_per_batch

        @pl.when(t == 0)
        def _():
            u_buf[0:HALO, :] = jnp.zeros((HALO, D_MODEL), F32)

        xb_buf[...] = x_ref[0].astype(BF16)
        row = lax.broadcasted_iota(jnp.int32, (TM, 1), 0) + t * TM

        def project_u(g):
            cols = slice(g * POOL_GROUP, (g + 1) * POOL_GROUP)
            u_buf[HALO:, cols] = jnp.dot(xb_buf[...], win_ref[:, cols], preferred_element_type=F32)

        project_u(0)
        anchors = norm_previous_tile()
        per_group = len(anchors) // len(POOL_WINDOWS)
        for g, w in enumerate(POOL_WINDOWS):
            cols = slice(g * POOL_GROUP, (g + 1) * POOL_GROUP)
            if g + 1 < len(POOL_WINDOWS):
                project_u(g + 1)
            ext = u_buf[:, cols]
            s = ext
            shift = 1
            while shift < w:
                s = s + pltpu.roll(s, shift, 0)
                shift *= 2
            inv_cnt = 1.0 / jnp.minimum(w, row + 1).astype(F32)
            anchor = sum(anchors[g * per_group:(g + 1) * per_group])
            anchor = jnp.tile(anchor, (TM // anchor.shape[0], POOL_GROUP // anchor.shape[1]))
            h_buf[:, cols] = mix(xb_buf[...], s[HALO:, :] * inv_cnt - ext[HALO:, :] + anchor, g)

        state_ref[0] = u_buf[TM:TM + HALO, :]
        u_buf[0:HALO, :] = u_buf[TM:TM + HALO, :]
        _outproj_residual(h_buf, lambda rs, cs: x_ref[0, rs, cs], wout_ref, set_r, TM)

    @pl.when(step == n_prompt)
    def _():
        load = pltpu.make_async_copy(st_hbm.at[layer], st_buf, sems.at[0])
        shift_old = pltpu.make_async_copy(st_buf.at[pl.ds(1, POOL_STATE - 1)],
                                          nst_hbm.at[pl.ds(0, POOL_STATE - 1)], sems.at[1])
        append_new = pltpu.make_async_copy(us_buf, nst_hbm.at[POOL_STATE - 1], sems.at[2])
        load.start()
        norm_previous_tile()
        xb = xs_ref[...].astype(BF16)
        load.wait()
        shift_old.start()
        for g, w in enumerate(POOL_WINDOWS):
            cols = slice(g * POOL_GROUP, (g + 1) * POOL_GROUP)
            u = jnp.dot(xb, win_ref[:, cols], preferred_element_type=F32)
            us_buf[:, cols] = u
            acc = u
            for j in range(1, w):
                acc = acc + st_buf[POOL_STATE - j, :, cols]
            h_buf[0:rows_s, cols] = mix(xb, acc * (1.0 / min(w, PAST_LEN + 1)) - u, g)
        append_new.start()

        def set_ys(rs, cs, v):
            ys_ref[rs, cs] = v

        _outproj_ln(h_buf, lambda rs, cs: xs_ref[rs, cs], wout_ref, g_ref, b_ref,
                    set_ys, lambda rs, cs: ys_ref[rs, cs], rows_s)
        shift_old.wait()
        append_new.wait()


def _pool_layer(x, xs, state_rows, layer, params):
    B, S, D = x.shape
    R = xs.shape[0]
    tiles_per_batch = S // TM
    n_prompt = B * tiles_per_batch

    def tile(step):
        step = jnp.clip(step, 0, n_prompt - 1)
        return step // tiles_per_batch, step % tiles_per_batch

    return pl.pallas_call(
        functools.partial(_pool_layer_kernel, layer, tiles_per_batch),
        grid=(n_prompt + 1,),
        in_specs=[pl.BlockSpec((1, TM, D), lambda i: (*tile(i), 0)), _resident(xs.shape),
                  pl.BlockSpec(memory_space=pl.ANY)] + [spec for _, spec in params],
        out_specs=[
            pl.BlockSpec((1, TM, D), lambda i: (*tile(i - 1), 0)),
            pl.BlockSpec((1, HALO, D), lambda i: (tile(i)[0], 0, 0)),
            pl.BlockSpec((R, D), lambda i: (0, 0)),
            pl.BlockSpec(memory_space=pl.ANY),
        ],
        out_shape=[jax.ShapeDtypeStruct((B, S, D), F32), jax.ShapeDtypeStruct((B, HALO, D), F32),
                   jax.ShapeDtypeStruct((R, D), F32), jax.ShapeDtypeStruct((POOL_STATE, R, D), F32)],
        scratch_shapes=[pltpu.VMEM((TM, D), BF16), pltpu.VMEM((HALO + TM, D), F32), pltpu.VMEM((TM, D), BF16),
                        pltpu.VMEM((TM, D), F32), pltpu.VMEM((POOL_STATE, R, D), F32), pltpu.VMEM((R, D), F32),
                        pltpu.SemaphoreType.DMA((3,))],
        compiler_params=_params(1),
        name="pool_layer",
    )(x, xs, state_rows, *[a for a, _ in params])


def _project_kv(xb, cos, sa, sb, wkv_ref):
    kv = jnp.dot(xb, wkv_ref[...], preferred_element_type=F32)
    k_slabs = [_rope(kv[:, j * LANES:(j + 1) * LANES], cos, sa, sb) for j in range(KV_DIM // LANES)]
    return k_slabs, kv[:, KV_DIM:]


def _kv_prompt_kernel(x_ref, cos_ref, sa_ref, sb_ref, wkv_ref, knew_ref, vnew_ref, kdup_ref, vt_ref):
    k_slabs, v = _project_kv(x_ref[0].astype(BF16), cos_ref[...], sa_ref[...], sb_ref[...], wkv_ref)
    low = lax.broadcasted_iota(jnp.int32, (TK, LANES), 1) < HEAD_DIM
    for j, k in enumerate(k_slabs):
        swapped = pltpu.roll(k, HEAD_DIM, 1)
        kdup_ref[0, :, (2 * j) * LANES:(2 * j + 1) * LANES] = jnp.where(low, k, swapped).astype(BF16)
        kdup_ref[0, :, (2 * j + 1) * LANES:(2 * j + 2) * LANES] = jnp.where(low, swapped, k).astype(BF16)
    for i in range(TK // WINDOW):
        vt_ref[0, i] = v[i * WINDOW:(i + 1) * WINDOW, :].T.astype(BF16)

    @pl.when(pl.program_id(1) == pl.num_programs(1) - 1)
    def _():
        for j, k in enumerate(k_slabs):
            knew_ref[0, :, j * LANES:(j + 1) * LANES] = k[TK - WINDOW:, :]
        vnew_ref[0] = v[TK - WINDOW:, :]


def _kv_prompt(x, tables, wkv):
    B, S, D = x.shape
    tab = pl.BlockSpec((TK, LANES), lambda bi, t: (t, 0))
    last = pl.BlockSpec((1, WINDOW, KV_DIM), lambda bi, t: (bi, 0, 0))
    return pl.pallas_call(
        _kv_prompt_kernel,
        grid=(B, S // TK),
        in_specs=[pl.BlockSpec((1, TK, D), lambda bi, t: (bi, t, 0)), tab, tab, tab, _resident(wkv.shape)],
        out_specs=[last, last,
                   pl.BlockSpec((1, TK, N_KV_HEADS * LANES), lambda bi, t: (bi, t, 0)),
                   pl.BlockSpec((1, TK // WINDOW, KV_DIM, WINDOW), lambda bi, t: (bi, t, 0, 0))],
        out_shape=[jax.ShapeDtypeStruct((B, WINDOW, KV_DIM), F32), jax.ShapeDtypeStruct((B, WINDOW, KV_DIM), F32),
                   jax.ShapeDtypeStruct((B, S, N_KV_HEADS * LANES), BF16),
                   jax.ShapeDtypeStruct((B, S // WINDOW, KV_DIM, WINDOW), BF16)],
        compiler_params=_params(2),
        name="kv_prompt",
    )(x, *tables, wkv)


def _kv_sample_kernel(x_ref, cos_ref, sa_ref, sb_ref, wkv_ref, k_ref, v_ref):
    k_slabs, v = _project_kv(x_ref[...].astype(BF16), cos_ref[...], sa_ref[...], sb_ref[...], wkv_ref)
    for j, k in enumerate(k_slabs):
        k_ref[:, j * LANES:(j + 1) * LANES] = k
    v_ref[...] = v


def _kv_sample(x, tables, wkv):
    R, D = x.shape
    out = pl.BlockSpec((R, KV_DIM), lambda i: (0, 0))
    return pl.pallas_call(
        _kv_sample_kernel,
        grid=(1,),
        in_specs=[_resident(x.shape)] + [_resident(t.shape) for t in tables] + [_resident(wkv.shape)],
        out_specs=[out, out],
        out_shape=[jax.ShapeDtypeStruct((R, KV_DIM), F32), jax.ShapeDtypeStruct((R, KV_DIM), F32)],
        compiler_params=_params(1),
        name="kv_sample",
    )(x, *tables, wkv)


def _attn_layer_kernel(tiles_per_batch,
                       x_ref, cos_ref, sa_ref, sb_ref, kdup_ref, vt_ref, xs_ref, cos_s_ref, sa_s_ref, sb_s_ref,
                       win_ref, sink_ref, wout_ref, g_ref, b_ref,
                       o_ref, qs_ref, gs_ref, xb_buf, q_buf, h_buf, g_buf, r_buf):
    step = pl.program_id(0)
    n_prompt = pl.num_programs(0) - 1
    group_cols = GROUP * HEAD_DIM

    def set_out(rs, cs, v):
        o_ref[0, rs, cs] = v

    def set_r(rs, cs, v):
        r_buf[rs, cs] = v

    def norm_previous_tile():
        return _layer_norm(lambda rs, cs: r_buf[rs, cs], set_out, g_ref, b_ref, TM)

    @pl.when(step == 0)
    def _():
        r_buf[...] = jnp.zeros(r_buf.shape, F32)

    @pl.when(step < n_prompt)
    def _():
        t = step % tiles_per_batch
        xb_buf[...] = x_ref[0].astype(BF16)

        def project_q(kv, anchor):
            q = jnp.dot(xb_buf[...], win_ref[:, kv * group_cols:(kv + 1) * group_cols], preferred_element_type=F32)
            if anchor is not None:
                q = q + anchor
            for j in range(group_cols // LANES):
                qj = _rope(q[:, j * LANES:(j + 1) * LANES], cos_ref[...], sa_ref[...], sb_ref[...])
                c0 = kv * group_cols + j * LANES
                q_buf[:, c0:c0 + LANES] = qj.astype(BF16)

        def project_gate(kv, anchor):
            cols = slice(kv * group_cols, (kv + 1) * group_cols)
            gate = jnp.dot(xb_buf[...], win_ref[:, D_MODEL + kv * group_cols:D_MODEL + (kv + 1) * group_cols],
                           preferred_element_type=F32)
            g_buf[:, cols] = _silu(gate) if anchor is None else _silu(gate) + anchor

        key = lax.broadcasted_iota(jnp.int32, (2 * WINDOW, WINDOW), 0)
        qry = lax.broadcasted_iota(jnp.int32, (2 * WINDOW, WINDOW), 1)
        band = (key > qry) & (key <= qry + WINDOW)
        low_half = lax.broadcasted_iota(jnp.int32, (WINDOW, LANES), 1) < HEAD_DIM
        ones_rows = jnp.ones((SUM_ROWS, 2 * WINDOW), BF16)

        def block_ids(qb):
            blk = t * (TM // WINDOW) + qb
            return blk, jnp.maximum(blk - 1, 0)

        def group_heads(kv):
            return [(kv * (GROUP // 2) + pair, par) for pair in range(GROUP // 2) for par in range(2)]

        def scores(qb, kv):
            blk, prev_blk = block_ids(qb)
            prev = pl.multiple_of(prev_blk * WINDOW, WINDOW)
            cur = pl.multiple_of(blk * WINDOW, WINDOW)
            rows = slice(qb * WINDOW, (qb + 1) * WINDOW)
            ks = slice(kv * LANES, (kv + 1) * LANES)
            k2 = jnp.concatenate([kdup_ref[0, pl.ds(prev, WINDOW), ks], kdup_ref[0, pl.ds(cur, WINDOW), ks]], axis=0)
            q_all = []
            for slab, par in group_heads(kv):
                q_slab = q_buf[rows, slab * LANES:(slab + 1) * LANES]
                q_all.append(jnp.where(low_half == (par == 0), q_slab, jnp.zeros_like(q_slab)))
            return lax.dot_general(k2, jnp.concatenate(q_all, axis=0), (((1,), (1,)), ((), ())),
                                   preferred_element_type=F32)

        def finish(qb, kv, s_t):
            blk, prev_blk = block_ids(qb)
            vis = band & ((key >= WINDOW) | (blk > 0))
            rows = slice(qb * WINDOW, (qb + 1) * WINDOW)
            vs = slice(kv * HEAD_DIM, (kv + 1) * HEAD_DIM)
            v_aug = jnp.concatenate([vt_ref[0, prev_blk, vs, :], vt_ref[0, blk, vs, :]], axis=1)
            v_aug = jnp.concatenate([v_aug, ones_rows], axis=0)
            for pair in range(GROUP // 2):
                slab = kv * (GROUP // 2) + pair
                p_t, sink_terms = [], []
                for par in range(2):
                    i = 2 * pair + par
                    s = jnp.where(vis, s_t[:, i * WINDOW:(i + 1) * WINDOW], NEG)
                    sink = sink_ref[2 * slab + par]
                    m = jnp.maximum(jnp.max(s, axis=0, keepdims=True), sink)
                    p_t.append(jnp.exp(s - m).astype(BF16))
                    sink_terms.append(jnp.exp(sink - m))
                o_t = jnp.dot(v_aug, jnp.concatenate(p_t, axis=1), preferred_element_type=F32)
                both = []
                for par in range(2):
                    cs = slice(par * WINDOW, (par + 1) * WINDOW)
                    inv = 1.0 / (o_t[HEAD_DIM:HEAD_DIM + 1, cs] + sink_terms[par])
                    both.append(o_t[:HEAD_DIM, cs] * inv)
                attn = jnp.concatenate(both, axis=0).T
                cs = slice(slab * LANES, (slab + 1) * LANES)
                h_buf[rows, cs] = (attn * g_buf[rows, cs]).astype(BF16)

        project_q(0, None)
        anchors = norm_previous_tile()
        early = 2 * (N_KV_HEADS // 2)
        per_proj = len(anchors) // early

        def next_anchor():
            if not anchors:
                return None
            anchor = sum(anchors.pop(0) for _ in range(per_proj))
            return jnp.tile(anchor, (TM // anchor.shape[0], group_cols // anchor.shape[1]))

        for kv in range(N_KV_HEADS):
            s_t = [scores(qb, kv) for qb in range(TM // WINDOW)]
            if kv + 1 < N_KV_HEADS:
                project_q(kv + 1, next_anchor())
            project_gate(kv, next_anchor())
            for qb in range(TM // WINDOW):
                finish(qb, kv, s_t[qb])
        _outproj_residual(h_buf, lambda rs, cs: x_ref[0, rs, cs], wout_ref, set_r, TM)

    @pl.when(step == n_prompt)
    def _():
        norm_previous_tile()
        xb = xs_ref[...].astype(BF16)
        for n in range(N_CHUNKS):
            q = jnp.dot(xb, win_ref[:, n * COL_CHUNK:(n + 1) * COL_CHUNK], preferred_element_type=F32)
            for j in range(COL_CHUNK // LANES):
                c0 = n * COL_CHUNK + j * LANES
                qs_ref[:, c0:c0 + LANES] = _rope(q[:, j * LANES:(j + 1) * LANES],
                                                 cos_s_ref[...], sa_s_ref[...], sb_s_ref[...])
            gs_ref[:, n * COL_CHUNK:(n + 1) * COL_CHUNK] = jnp.dot(
                xb, win_ref[:, D_MODEL + n * COL_CHUNK:D_MODEL + (n + 1) * COL_CHUNK],
                preferred_element_type=F32)


def _attn_layer(x, tables, kdup, vt, xs, tables_s, win, sinks, wout, g, b):
    B, S, D = x.shape
    R = xs.shape[0]
    tiles_per_batch = S // TM
    n_prompt = B * tiles_per_batch

    def tile(step):
        step = jnp.clip(step, 0, n_prompt - 1)
        return step // tiles_per_batch, step % tiles_per_batch

    tab = pl.BlockSpec((TM, LANES), lambda i: (tile(i)[1], 0))

    def per_batch(a):
        return pl.BlockSpec((1,) + a.shape[1:], lambda i: (tile(i)[0],) + (0,) * (a.ndim - 1),
                            pipeline_mode=pl.Buffered(1))

    rows = pl.BlockSpec((R, D), lambda i: (0, 0))
    return pl.pallas_call(
        functools.partial(_attn_layer_kernel, tiles_per_batch),
        grid=(n_prompt + 1,),
        in_specs=[
            pl.BlockSpec((1, TM, D), lambda i: (*tile(i), 0)), tab, tab, tab, per_batch(kdup), per_batch(vt),
            _resident(xs.shape)] + [_resident(t.shape) for t in tables_s] + [
            win[1], pl.BlockSpec(memory_space=pltpu.SMEM), wout[1], g[1], b[1],
        ],
        out_specs=[pl.BlockSpec((1, TM, D), lambda i: (*tile(i - 1), 0)), rows, rows],
        out_shape=[jax.ShapeDtypeStruct((B, S, D), F32),
                   jax.ShapeDtypeStruct((R, D), F32), jax.ShapeDtypeStruct((R, D), F32)],
        scratch_shapes=[pltpu.VMEM((TM, D), BF16), pltpu.VMEM((TM, D), BF16), pltpu.VMEM((TM, D), BF16),
                        pltpu.VMEM((TM, D), F32), pltpu.VMEM((TM, D), F32)],
        compiler_params=_params(1),
        name="attn_layer",
    )(x, *tables, kdup, vt, xs, *tables_s, win[0], sinks, wout[0], g[0], b[0])


def _attn_sample_kernel(q_ref, kn_ref, vn_ref, ck_ref, cv_ref, sink_ref, o_ref, nk_ref, nv_ref):
    nb = q_ref.shape[0]
    head_of_lane = lax.broadcasted_iota(jnp.int32, (N_HEADS, D_MODEL), 1) // HEAD_DIM
    own_head = head_of_lane == lax.broadcasted_iota(jnp.int32, (N_HEADS, D_MODEL), 0)
    low_half = lax.broadcasted_iota(jnp.int32, (N_HEADS, LANES), 1) < HEAD_DIM
    last_row = lax.broadcasted_iota(jnp.int32, (WINDOW, KV_DIM), 0) == WINDOW - 1
    sink = sink_ref[...]
    heads_per_slab = LANES // HEAD_DIM
    slabs_per_group = GROUP // heads_per_slab

    def body(i, carry):
        newk = jnp.where(last_row, kn_ref[pl.ds(i, 1), :], pltpu.roll(ck_ref[i], WINDOW - 1, 0))
        newv = jnp.where(last_row, vn_ref[pl.ds(i, 1), :], pltpu.roll(cv_ref[i], WINDOW - 1, 0))
        nk_ref[i] = newk
        nv_ref[i] = newv
        qh = jnp.where(own_head, jnp.broadcast_to(q_ref[pl.ds(i, 1), :], (N_HEADS, D_MODEL)), 0.0)
        folded = []
        for kv in range(N_KV_HEADS):
            w = qh[:, kv * GROUP * HEAD_DIM:kv * GROUP * HEAD_DIM + LANES]
            for sl in range(1, slabs_per_group):
                c0 = kv * GROUP * HEAD_DIM + sl * LANES
                w = w + qh[:, c0:c0 + LANES]
            folded.append(w + pltpu.roll(w, HEAD_DIM, 1))
        qg = jnp.concatenate([jnp.where(low_half, folded[2 * j], folded[2 * j + 1])
                              for j in range(N_KV_HEADS // 2)], axis=1)
        s = lax.dot_general(qg.astype(BF16), newk.astype(BF16), (((1,), (1,)), ((), ())),
                            preferred_element_type=F32)
        m = jnp.maximum(jnp.max(s, axis=-1, keepdims=True), sink)
        p = jnp.exp(s - m)
        denom = jnp.sum(p, axis=-1, keepdims=True) + jnp.exp(sink - m)
        og = jnp.dot(p.astype(BF16), newv.astype(BF16), preferred_element_type=F32) / denom
        slabs = []
        for kv in range(N_KV_HEADS):
            xs = og[:, (kv // 2) * LANES:(kv // 2 + 1) * LANES]
            rolled = pltpu.roll(xs, HEAD_DIM, 1)
            both = jnp.where(low_half, xs, rolled) if kv % 2 == 0 else jnp.where(low_half, rolled, xs)
            slabs.extend([both] * slabs_per_group)
        full = jnp.concatenate(slabs, axis=1)
        o_ref[pl.ds(i, 1), :] = jnp.sum(jnp.where(own_head, full, 0.0), axis=0, keepdims=True)
        return carry

    lax.fori_loop(0, nb, body, 0)


def _attn_sample(q, kn, vn, ck, cv, sinks_col):
    R, D = q.shape
    nb = SAMPLE_ATTN_BATCH
    row2 = lambda w: pl.BlockSpec((nb, w), lambda i: (i, 0))
    cache = pl.BlockSpec((nb, WINDOW, KV_DIM), lambda i: (i, 0, 0))
    return pl.pallas_call(
        _attn_sample_kernel,
        grid=(R // nb,),
        in_specs=[row2(D), row2(KV_DIM), row2(KV_DIM), cache, cache, _resident(sinks_col.shape)],
        out_specs=[row2(D), cache, cache],
        out_shape=[jax.ShapeDtypeStruct((R, D), F32),
                   jax.ShapeDtypeStruct(ck.shape, F32), jax.ShapeDtypeStruct(cv.shape, F32)],
        compiler_params=_params(1),
        name="attn_sample",
    )(q, kn, vn, ck, cv, sinks_col)


def _gated_out_sample_kernel(x_ref, a_ref, gate_ref, wout_ref, g_ref, b_ref, o_ref, h_buf):
    rows = x_ref.shape[0]
    h_buf[...] = (a_ref[...] * _silu(gate_ref[...])).astype(BF16)

    def set_rows(rs, cs, v):
        o_ref[rs, cs] = v

    _outproj_ln(h_buf, lambda rs, cs: x_ref[rs, cs], wout_ref, g_ref, b_ref,
                set_rows, lambda rs, cs: o_ref[rs, cs], rows)


def _gated_out_sample(x, a, gate, wout, g, b):
    R, D = x.shape
    return pl.pallas_call(
        _gated_out_sample_kernel,
        grid=(1,),
        in_specs=[_resident(t.shape) for t in (x, a, gate)] + [wout[1], g[1], b[1]],
        out_specs=pl.BlockSpec((R, D), lambda i: (0, 0)),
        out_shape=jax.ShapeDtypeStruct((R, D), F32),
        scratch_shapes=[pltpu.VMEM((R, D), BF16)],
        compiler_params=_params(1),
        name="gated_out_sample",
    )(x, a, gate, wout[0], g[0], b[0])


def _rope_tables(pos):
    half = ROT_DIM // 2
    inv_freq = ROPE_THETA ** (-jnp.arange(0, ROT_DIM, 2, dtype=F32) / ROT_DIM)
    ang = pos.astype(F32)[:, None] * inv_freq[None, :]
    cos, sin = jnp.cos(ang), jnp.sin(ang)
    n = pos.shape[0]
    rest = jnp.zeros((n, HEAD_DIM - ROT_DIM), F32)
    zero = jnp.zeros((n, half), F32)
    cos_h = jnp.concatenate([cos, cos, rest + 1.0], axis=1)
    sa_h = jnp.concatenate([-sin, zero, rest], axis=1)
    sb_h = jnp.concatenate([zero, sin, rest], axis=1)
    rep = LANES // HEAD_DIM
    return tuple(jnp.tile(a, (1, rep)) for a in (cos_h, sa_h, sb_h))


def kernel(x_prompt, x_sample, state_pool, cache_k, cache_v, w_in_a, w_grp_a, scale_a, w_out_a,
           w_kv, w_in_b, sinks_b, w_out_b, ln_g, ln_b):
    B, S, D = x_prompt.shape
    R = x_sample.shape[0]
    xp = x_prompt
    xs = x_sample.reshape(R, D)
    tab_p = _rope_tables(jnp.arange(S, dtype=jnp.int32))
    tab_s = _rope_tables(jnp.full((R,), PAST_LEN, jnp.int32))
    qtab_p = tuple(a * SM_SCALE for a in tab_p)
    qtab_s = tuple(a * SM_SCALE for a in tab_s)
    w_in_a, w_grp_a, w_out_a, w_in_b, w_out_b = (
        w.astype(BF16) for w in (w_in_a, w_grp_a, w_out_a, w_in_b, w_out_b))
    scale_a, ln_g, ln_b = (p.reshape(p.shape[0], 1, D) for p in (scale_a, ln_g, ln_b))
    state_rows = state_pool.transpose(0, 2, 1, 3)
    pool_p, pool_s = [], []
    for i in range(N_A_LAYERS):
        params = [_layer(p, i) for p in (w_in_a, w_grp_a, scale_a, w_out_a, ln_g, ln_b)]
        xp, sp, xs, ss = _pool_layer(xp, xs, state_rows, i, params)
        pool_p.append(sp[:, HALO - POOL_STATE:])
        pool_s.append(ss)
    wkv = w_kv.astype(BF16)
    new_k_p, new_v_p, kdup_p, vt_p = _kv_prompt(xp, tab_p, wkv)
    k_s, v_s = _kv_sample(xs, tab_s, wkv)
    ck = cache_k.reshape(R, WINDOW, KV_DIM)
    cv = cache_v.reshape(R, WINDOW, KV_DIM)
    for j in range(DEPTH - N_A_LAYERS):
        i = N_A_LAYERS + j
        win, wout, g, b = _layer(w_in_b, j), _layer(w_out_b, j), _layer(ln_g, i), _layer(ln_b, i)
        xp, q_s, gate_s = _attn_layer(xp, qtab_p, kdup_p, vt_p, xs, qtab_s, win, sinks_b[j], wout, g, b)
        a_s, nk, nv = _attn_sample(q_s, k_s, v_s, ck, cv, sinks_b[j][:, None])
        xs = _gated_out_sample(xs, a_s, gate_s, wout, g, b)
    kv4 = (N_KV_HEADS, HEAD_DIM)
    return (xp, xs.reshape(R, 1, D), jnp.stack(pool_p, axis=0), jnp.stack(pool_s, axis=0).transpose(0, 2, 1, 3),
            new_k_p.reshape(B, WINDOW, *kv4), new_v_p.reshape(B, WINDOW, *kv4),
            nk.reshape(R, WINDOW, *kv4), nv.reshape(R, WINDOW, *kv4))
```

```python
import functools

import jax
import jax.numpy as jnp
from jax import lax
from jax.experimental import pallas as pl
from jax.experimental.pallas import tpu as pltpu

F32 = jnp.float32
BF16 = jnp.bfloat16

D_MODEL = 2048
DEPTH = 4
PAST_LEN = 16384
N_A_LAYERS = DEPTH // 2
POOL_WINDOWS = (2, 4, 8, 16)
POOL_GROUP = D_MODEL // len(POOL_WINDOWS)
POOL_STATE = max(POOL_WINDOWS) - 1
HEAD_DIM = 64
N_HEADS = D_MODEL // HEAD_DIM
N_KV_HEADS = N_HEADS // 8
GROUP = N_HEADS // N_KV_HEADS
KV_DIM = N_KV_HEADS * HEAD_DIM
WINDOW = 128
ROT_DIM = HEAD_DIM // 4
ROPE_THETA = 500000.0
ALPHA = (2 * DEPTH) ** 0.25
LN_EPS = 1e-5
NEG = -1e30
SM_SCALE = HEAD_DIM ** -0.5

LANES = 128
SUBLANES = 8
HALO = 16
COL_CHUNK = 512
N_CHUNKS = D_MODEL // COL_CHUNK
TM = 256
TK = 512
LN_ROWS = 16
SAMPLE_ATTN_BATCH = 8
SUM_ROWS = 16
CAST_ROWS = 128
VMEM_LIMIT_BYTES = 56 * 1024 * 1024


def _params(n_axes):
    return pltpu.CompilerParams(dimension_semantics=("arbitrary",) * n_axes,
                                vmem_limit_bytes=VMEM_LIMIT_BYTES)


def _resident(shape):
    zeros = (0,) * len(shape)
    return pl.BlockSpec(shape, lambda *_: zeros, pipeline_mode=pl.Buffered(1))


def _layer(stacked, i):
    zeros = (0,) * (stacked.ndim - 1)
    return stacked, pl.BlockSpec((None,) + stacked.shape[1:], lambda *_: (i,) + zeros,
                                 pipeline_mode=pl.Buffered(1))


def _silu(g):
    return g / (1.0 + jnp.exp(-g))


def _rope(x, cos, sa, sb):
    return x * cos + pltpu.roll(x, LANES - ROT_DIM // 2, 1) * sa + pltpu.roll(x, ROT_DIM // 2, 1) * sb


def _load_as_bf16(w_hbm, dst_ref, stage, sems):
    n_cols = dst_ref.shape[1]
    chunk = stage.shape[1]
    n_chunks = dst_ref.shape[0] // chunk

    def copy(c):
        return pltpu.make_async_copy(w_hbm.at[pl.ds(c * chunk, chunk)],
                                     stage.at[c % 2, :, pl.ds(0, n_cols)], sems.at[c % 2])

    copy(0).start()
    for c in range(n_chunks):
        if c + 1 < n_chunks:
            copy(c + 1).start()
        copy(c).wait()
        dst_ref[c * chunk:(c + 1) * chunk, :] = stage[c % 2, :, 0:n_cols].astype(BF16)


def _outproj_residual(h_ref, x_rows, wout_ref, r_rows_set, rows):
    for n in range(N_CHUNKS):
        cols = slice(n * COL_CHUNK, (n + 1) * COL_CHUNK)
        y = jnp.dot(h_ref[0:rows, :], wout_ref[:, cols], preferred_element_type=F32)
        r_rows_set(slice(0, rows), cols, ALPHA * x_rows(slice(0, rows), cols) + y)


def _zero_after(v):
    bits = pltpu.bitcast(v, jnp.int32)
    half = jnp.full(bits.shape, 16, jnp.int32)
    return lax.shift_right_logical(lax.shift_right_logical(bits, half), half).astype(F32)


def _layer_norm(r_rows_get, o_rows_set, g_ref, b_ref, rows):
    step = min(LN_ROWS, rows)
    anchors = []
    for r0 in range(0, rows, step):
        rs = slice(r0, r0 + step)
        r = r_rows_get(rs, slice(None))
        mu = jnp.mean(r, axis=-1, keepdims=True)
        c = r - mu
        var = jnp.mean(c * c, axis=-1, keepdims=True)
        out = c * lax.rsqrt(var + LN_EPS) * g_ref[...] + b_ref[...]
        o_rows_set(rs, slice(None), out)
        folded = sum(out[i:i + SUBLANES, j:j + LANES]
                     for i in range(0, step, SUBLANES) for j in range(0, out.shape[1], LANES))
        anchors.append(_zero_after(folded))
    return anchors


def _outproj_ln(h_ref, x_rows, wout_ref, g_ref, b_ref, o_rows_set, o_rows_get, rows):
    _outproj_residual(h_ref, x_rows, wout_ref, o_rows_set, rows)
    _layer_norm(o_rows_get, o_rows_set, g_ref, b_ref, rows)


def _pool_layer_kernel(layer, tiles_per_batch,
                       x_ref, xs_ref, st_hbm, win_hbm, wgrp_ref, scale_ref, wout_hbm, g_ref, b_ref,
                       o_ref, state_ref, ys_ref, nst_hbm,
                       xb_buf, u_buf, h_buf, r_buf, st_buf, us_buf, win_ref, wout_ref, stage, sems, wsems):
    step = pl.program_id(0)
    n_prompt = pl.num_programs(0) - 1
    rows_s = xs_ref.shape[0]

    def mix(xb, d, g):
        cols = slice(g * POOL_GROUP, (g + 1) * POOL_GROUP)
        gate = jnp.dot(xb, win_ref[:, D_MODEL + g * POOL_GROUP:D_MODEL + (g + 1) * POOL_GROUP],
                       preferred_element_type=F32)
        d = jnp.dot(d.astype(BF16), wgrp_ref[g], preferred_element_type=F32) * scale_ref[:, cols]
        return (d * _silu(gate)).astype(BF16)

    def set_out(rs, cs, v):
        o_ref[0, rs, cs] = v

    def set_r(rs, cs, v):
        r_buf[rs, cs] = v

    def norm_previous_tile():
        return _layer_norm(lambda rs, cs: r_buf[rs, cs], set_out, g_ref, b_ref, TM)

    @pl.when(step == 0)
    def _():
        r_buf[...] = jnp.zeros(r_buf.shape, F32)
        _load_as_bf16(win_hbm.at[layer], win_ref, stage, wsems)
        _load_as_bf16(wout_hbm.at[layer], wout_ref, stage, wsems)

    @pl.when(step < n_prompt)
    def _():
        t = step % tiles_per_batch

        @pl.when(t == 0)
        def _():
            u_buf[0:HALO, :] = jnp.zeros((HALO, D_MODEL), F32)

        xb_buf[...] = x_ref[0].astype(BF16)
        row = lax.broadcasted_iota(jnp.int32, (TM, 1), 0) + t * TM

        def project_u(g):
            cols = slice(g * POOL_GROUP, (g + 1) * POOL_GROUP)
            u_buf[HALO:, cols] = jnp.dot(xb_buf[...], win_ref[:, cols], preferred_element_type=F32)

        project_u(0)
        anchors = norm_previous_tile()
        per_group = len(anchors) // len(POOL_WINDOWS)
        for g, w in enumerate(POOL_WINDOWS):
            cols = slice(g * POOL_GROUP, (g + 1) * POOL_GROUP)
            if g + 1 < len(POOL_WINDOWS):
                project_u(g + 1)
            ext = u_buf[:, cols]
            s = ext
            shift = 1
            while shift < w:
                s = s + pltpu.roll(s, shift, 0)
                shift *= 2
            inv_cnt = 1.0 / jnp.minimum(w, row + 1).astype(F32)
            anchor = sum(anchors[g * per_group:(g + 1) * per_group])
            anchor = jnp.tile(anchor, (TM // anchor.shape[0], POOL_GROUP // anchor.shape[1]))
            h_buf[:, cols] = mix(xb_buf[...], s[HALO:, :] * inv_cnt - ext[HALO:, :] + anchor, g)

        state_ref[0] = u_buf[TM:TM + HALO, :]
        u_buf[0:HALO, :] = u_buf[TM:TM + HALO, :]
        _outproj_residual(h_buf, lambda rs, cs: x_ref[0, rs, cs], wout_ref, set_r, TM)

    @pl.when(step == n_prompt)
    def _():
        load = pltpu.make_async_copy(st_hbm.at[layer], st_buf, sems.at[0])
        shift_old = pltpu.make_async_copy(st_buf.at[pl.ds(1, POOL_STATE - 1)],
                                          nst_hbm.at[pl.ds(0, POOL_STATE - 1)], sems.at[1])
        append_new = pltpu.make_async_copy(us_buf, nst_hbm.at[POOL_STATE - 1], sems.at[2])
        load.start()
        norm_previous_tile()
        xb = xs_ref[...].astype(BF16)
        load.wait()
        shift_old.start()
        for g, w in enumerate(POOL_WINDOWS):
            cols = slice(g * POOL_GROUP, (g + 1) * POOL_GROUP)
            u = jnp.dot(xb, win_ref[:, cols], preferred_element_type=F32)
            us_buf[:, cols] = u
            acc = u
            for j in range(1, w):
                acc = acc + st_buf[POOL_STATE - j, :, cols]
            h_buf[0:rows_s, cols] = mix(xb, acc * (1.0 / min(w, PAST_LEN + 1)) - u, g)
        append_new.start()

        def set_ys(rs, cs, v):
            ys_ref[rs, cs] = v

        _outproj_ln(h_buf, lambda rs, cs: xs_ref[rs, cs], wout_ref, g_ref, b_ref,
                    set_ys, lambda rs, cs: ys_ref[rs, cs], rows_s)
        shift_old.wait()
        append_new.wait()


def _pool_layer(x, xs, state_rows, layer, w_in, w_out, params):
    B, S, D = x.shape
    R = xs.shape[0]
    tiles_per_batch = S // TM
    n_prompt = B * tiles_per_batch

    def tile(step):
        step = jnp.clip(step, 0, n_prompt - 1)
        return step // tiles_per_batch, step % tiles_per_batch

    hbm = pl.BlockSpec(memory_space=pl.ANY)
    (wgrp, wgrp_spec), (scale, scale_spec), (g, g_spec), (b, b_spec) = params
    return pl.pallas_call(
        functools.partial(_pool_layer_kernel, layer, tiles_per_batch),
        grid=(n_prompt + 1,),
        in_specs=[pl.BlockSpec((1, TM, D), lambda i: (*tile(i), 0)), _resident(xs.shape), hbm,
                  hbm, wgrp_spec, scale_spec, hbm, g_spec, b_spec],
        out_specs=[
            pl.BlockSpec((1, TM, D), lambda i: (*tile(i - 1), 0)),
            pl.BlockSpec((1, HALO, D), lambda i: (tile(i)[0], 0, 0)),
            pl.BlockSpec((R, D), lambda i: (0, 0)),
            pl.BlockSpec(memory_space=pl.ANY),
        ],
        out_shape=[jax.ShapeDtypeStruct((B, S, D), F32), jax.ShapeDtypeStruct((B, HALO, D), F32),
                   jax.ShapeDtypeStruct((R, D), F32), jax.ShapeDtypeStruct((POOL_STATE, R, D), F32)],
        scratch_shapes=[pltpu.VMEM((TM, D), BF16), pltpu.VMEM((HALO + TM, D), F32), pltpu.VMEM((TM, D), BF16),
                        pltpu.VMEM((TM, D), F32), pltpu.VMEM((POOL_STATE, R, D), F32), pltpu.VMEM((R, D), F32),
                        pltpu.VMEM(w_in.shape[1:], BF16), pltpu.VMEM(w_out.shape[1:], BF16),
                        pltpu.VMEM((2, CAST_ROWS, w_in.shape[2]), F32),
                        pltpu.SemaphoreType.DMA((3,)), pltpu.SemaphoreType.DMA((2,))],
        compiler_params=_params(1),
        name="pool_layer",
    )(x, xs, state_rows, w_in, wgrp, scale, w_out, g, b)


def _project_kv(xb, cos, sa, sb, wkv_ref):
    kv = jnp.dot(xb, wkv_ref[...], preferred_element_type=F32)
    k_slabs = [_rope(kv[:, j * LANES:(j + 1) * LANES], cos, sa, sb) for j in range(KV_DIM // LANES)]
    return k_slabs, kv[:, KV_DIM:]


def _kv_prompt_kernel(x_ref, cos_ref, sa_ref, sb_ref, wkv_ref, knew_ref, vnew_ref, kdup_ref, vt_ref):
    k_slabs, v = _project_kv(x_ref[0].astype(BF16), cos_ref[...], sa_ref[...], sb_ref[...], wkv_ref)
    low = lax.broadcasted_iota(jnp.int32, (TK, LANES), 1) < HEAD_DIM
    for j, k in enumerate(k_slabs):
        swapped = pltpu.roll(k, HEAD_DIM, 1)
        kdup_ref[0, :, (2 * j) * LANES:(2 * j + 1) * LANES] = jnp.where(low, k, swapped).astype(BF16)
        kdup_ref[0, :, (2 * j + 1) * LANES:(2 * j + 2) * LANES] = jnp.where(low, swapped, k).astype(BF16)
    for i in range(TK // WINDOW):
        vt_ref[0, i] = v[i * WINDOW:(i + 1) * WINDOW, :].T.astype(BF16)

    @pl.when(pl.program_id(1) == pl.num_programs(1) - 1)
    def _():
        for j, k in enumerate(k_slabs):
            knew_ref[0, :, j * LANES:(j + 1) * LANES] = k[TK - WINDOW:, :]
        vnew_ref[0] = v[TK - WINDOW:, :]


def _kv_prompt(x, tables, wkv):
    B, S, D = x.shape
    tab = pl.BlockSpec((TK, LANES), lambda bi, t: (t, 0))
    last = pl.BlockSpec((1, WINDOW, KV_DIM), lambda bi, t: (bi, 0, 0))
    return pl.pallas_call(
        _kv_prompt_kernel,
        grid=(B, S // TK),
        in_specs=[pl.BlockSpec((1, TK, D), lambda bi, t: (bi, t, 0)), tab, tab, tab, _resident(wkv.shape)],
        out_specs=[last, last,
                   pl.BlockSpec((1, TK, N_KV_HEADS * LANES), lambda bi, t: (bi, t, 0)),
                   pl.BlockSpec((1, TK // WINDOW, KV_DIM, WINDOW), lambda bi, t: (bi, t, 0, 0))],
        out_shape=[jax.ShapeDtypeStruct((B, WINDOW, KV_DIM), F32), jax.ShapeDtypeStruct((B, WINDOW, KV_DIM), F32),
                   jax.ShapeDtypeStruct((B, S, N_KV_HEADS * LANES), BF16),
                   jax.ShapeDtypeStruct((B, S // WINDOW, KV_DIM, WINDOW), BF16)],
        compiler_params=_params(2),
        name="kv_prompt",
    )(x, *tables, wkv)


def _kv_sample_kernel(x_ref, cos_ref, sa_ref, sb_ref, wkv_ref, k_ref, v_ref):
    k_slabs, v = _project_kv(x_ref[...].astype(BF16), cos_ref[...], sa_ref[...], sb_ref[...], wkv_ref)
    for j, k in enumerate(k_slabs):
        k_ref[:, j * LANES:(j + 1) * LANES] = k
    v_ref[...] = v


def _kv_sample(x, tables, wkv):
    R, D = x.shape
    out = pl.BlockSpec((R, KV_DIM), lambda i: (0, 0))
    return pl.pallas_call(
        _kv_sample_kernel,
        grid=(1,),
        in_specs=[_resident(x.shape)] + [_resident(t.shape) for t in tables] + [_resident(wkv.shape)],
        out_specs=[out, out],
        out_shape=[jax.ShapeDtypeStruct((R, KV_DIM), F32), jax.ShapeDtypeStruct((R, KV_DIM), F32)],
        compiler_params=_params(1),
        name="kv_sample",
    )(x, *tables, wkv)


def _attn_layer_kernel(layer, tiles_per_batch,
                       x_ref, cos_ref, sa_ref, sb_ref, bias_ref, kdup_ref, vt_ref, xs_ref, cos_s_ref, sa_s_ref, sb_s_ref,
                       win_hbm, sink_ref, wout_hbm, g_ref, b_ref,
                       o_ref, qs_ref, gs_ref, xb_buf, q_buf, h_buf, g_buf, r_buf, win_ref, wout_ref, stage, wsems):
    step = pl.program_id(0)
    n_prompt = pl.num_programs(0) - 1
    group_cols = GROUP * HEAD_DIM

    def set_out(rs, cs, v):
        o_ref[0, rs, cs] = v

    def set_r(rs, cs, v):
        r_buf[rs, cs] = v

    def norm_previous_tile():
        return _layer_norm(lambda rs, cs: r_buf[rs, cs], set_out, g_ref, b_ref, TM)

    @pl.when(step == 0)
    def _():
        r_buf[...] = jnp.zeros(r_buf.shape, F32)
        _load_as_bf16(win_hbm.at[layer], win_ref, stage, wsems)
        _load_as_bf16(wout_hbm.at[layer], wout_ref, stage, wsems)

    @pl.when(step < n_prompt)
    def _():
        t = step % tiles_per_batch
        xb_buf[...] = x_ref[0].astype(BF16)

        def project_q(kv, anchor):
            q = jnp.dot(xb_buf[...], win_ref[:, kv * group_cols:(kv + 1) * group_cols], preferred_element_type=F32)
            if anchor is not None:
                q = q + anchor
            for j in range(group_cols // LANES):
                qj = _rope(q[:, j * LANES:(j + 1) * LANES], cos_ref[...], sa_ref[...], sb_ref[...])
                c0 = kv * group_cols + j * LANES
                q_buf[:, c0:c0 + LANES] = qj.astype(BF16)

        def project_gate(kv, anchor):
            cols = slice(kv * group_cols, (kv + 1) * group_cols)
            gate = jnp.dot(xb_buf[...], win_ref[:, D_MODEL + kv * group_cols:D_MODEL + (kv + 1) * group_cols],
                           preferred_element_type=F32)
            g_buf[:, cols] = _silu(gate) if anchor is None else _silu(gate) + anchor

        low_half = lax.broadcasted_iota(jnp.int32, (WINDOW, LANES), 1) < HEAD_DIM
        ones_rows = jnp.ones((SUM_ROWS, 2 * WINDOW), BF16)

        def block_ids(qb):
            blk = t * (TM // WINDOW) + qb
            return blk, jnp.maximum(blk - 1, 0)

        def group_heads(kv):
            return [(kv * (GROUP // 2) + pair, par) for pair in range(GROUP // 2) for par in range(2)]

        def scores(qb, kv):
            blk, prev_blk = block_ids(qb)
            prev = pl.multiple_of(prev_blk * WINDOW, WINDOW)
            cur = pl.multiple_of(blk * WINDOW, WINDOW)
            rows = slice(qb * WINDOW, (qb + 1) * WINDOW)
            ks = slice(kv * LANES, (kv + 1) * LANES)
            k2 = jnp.concatenate([kdup_ref[0, pl.ds(prev, WINDOW), ks], kdup_ref[0, pl.ds(cur, WINDOW), ks]], axis=0)
            q_all = []
            for slab, par in group_heads(kv):
                q_slab = q_buf[rows, slab * LANES:(slab + 1) * LANES]
                q_all.append(jnp.where(low_half == (par == 0), q_slab, jnp.zeros_like(q_slab)))
            return lax.dot_general(k2, jnp.concatenate(q_all, axis=0), (((1,), (1,)), ((), ())),
                                   preferred_element_type=F32)

        def finish(qb, kv, s_t):
            blk, prev_blk = block_ids(qb)
            bias = bias_ref[jnp.minimum(blk, 1)]
            rows = slice(qb * WINDOW, (qb + 1) * WINDOW)
            vs = slice(kv * HEAD_DIM, (kv + 1) * HEAD_DIM)
            v_aug = jnp.concatenate([vt_ref[0, prev_blk, vs, :], vt_ref[0, blk, vs, :]], axis=1)
            v_aug = jnp.concatenate([v_aug, ones_rows], axis=0)
            for pair in range(GROUP // 2):
                slab = kv * (GROUP // 2) + pair
                p_t, sink_terms = [], []
                for par in range(2):
                    i = 2 * pair + par
                    s = s_t[:, i * WINDOW:(i + 1) * WINDOW] + bias
                    sink = sink_ref[2 * slab + par]
                    m = jnp.maximum(jnp.max(s, axis=0, keepdims=True), sink)
                    p_t.append(jnp.exp(s - m).astype(BF16))
                    sink_terms.append(jnp.exp(sink - m))
                o_t = jnp.dot(v_aug, jnp.concatenate(p_t, axis=1), preferred_element_type=F32)
                both = []
                for par in range(2):
                    cs = slice(par * WINDOW, (par + 1) * WINDOW)
                    inv = 1.0 / (o_t[HEAD_DIM:HEAD_DIM + 1, cs] + sink_terms[par])
                    both.append(o_t[:HEAD_DIM, cs] * inv)
                attn = jnp.concatenate(both, axis=0).T
                cs = slice(slab * LANES, (slab + 1) * LANES)
                h_buf[rows, cs] = (attn * g_buf[rows, cs]).astype(BF16)

        project_q(0, None)
        anchors = norm_previous_tile()
        early = 2 * (N_KV_HEADS // 2)
        per_proj = len(anchors) // early

        def next_anchor():
            if not anchors:
                return None
            anchor = sum(anchors.pop(0) for _ in range(per_proj))
            return jnp.tile(anchor, (TM // anchor.shape[0], group_cols // anchor.shape[1]))

        for kv in range(N_KV_HEADS):
            s_t = [scores(qb, kv) for qb in range(TM // WINDOW)]
            if kv + 1 < N_KV_HEADS:
                project_q(kv + 1, next_anchor())
            project_gate(kv, next_anchor())
            for qb in range(TM // WINDOW):
                finish(qb, kv, s_t[qb])
        _outproj_residual(h_buf, lambda rs, cs: x_ref[0, rs, cs], wout_ref, set_r, TM)

    @pl.when(step == n_prompt)
    def _():
        norm_previous_tile()
        xb = xs_ref[...].astype(BF16)
        for n in range(N_CHUNKS):
            q = jnp.dot(xb, win_ref[:, n * COL_CHUNK:(n + 1) * COL_CHUNK], preferred_element_type=F32)
            for j in range(COL_CHUNK // LANES):
                c0 = n * COL_CHUNK + j * LANES
                qs_ref[:, c0:c0 + LANES] = _rope(q[:, j * LANES:(j + 1) * LANES],
                                                 cos_s_ref[...], sa_s_ref[...], sb_s_ref[...])
            gs_ref[:, n * COL_CHUNK:(n + 1) * COL_CHUNK] = jnp.dot(
                xb, win_ref[:, D_MODEL + n * COL_CHUNK:D_MODEL + (n + 1) * COL_CHUNK],
                preferred_element_type=F32)


def _band_bias():
    key = jnp.arange(2 * WINDOW, dtype=jnp.int32)[:, None]
    qry = jnp.arange(WINDOW, dtype=jnp.int32)[None, :]
    band = (key > qry) & (key <= qry + WINDOW)
    return jnp.where(jnp.stack([band & (key >= WINDOW), band]), 0.0, NEG).astype(F32)


def _attn_layer(x, tables, bias, kdup, vt, xs, tables_s, layer, w_in, sinks, w_out, g, b):
    B, S, D = x.shape
    R = xs.shape[0]
    tiles_per_batch = S // TM
    n_prompt = B * tiles_per_batch

    def tile(step):
        step = jnp.clip(step, 0, n_prompt - 1)
        return step // tiles_per_batch, step % tiles_per_batch

    tab = pl.BlockSpec((TM, LANES), lambda i: (tile(i)[1], 0))

    def per_batch(a):
        return pl.BlockSpec((1,) + a.shape[1:], lambda i: (tile(i)[0],) + (0,) * (a.ndim - 1),
                            pipeline_mode=pl.Buffered(1))

    rows = pl.BlockSpec((R, D), lambda i: (0, 0))
    return pl.pallas_call(
        functools.partial(_attn_layer_kernel, layer, tiles_per_batch),
        grid=(n_prompt + 1,),
        in_specs=[
            pl.BlockSpec((1, TM, D), lambda i: (*tile(i), 0)), tab, tab, tab, _resident(bias.shape),
            per_batch(kdup), per_batch(vt),
            _resident(xs.shape)] + [_resident(t.shape) for t in tables_s] + [
            pl.BlockSpec(memory_space=pl.ANY), pl.BlockSpec(memory_space=pltpu.SMEM),
            pl.BlockSpec(memory_space=pl.ANY), g[1], b[1],
        ],
        out_specs=[pl.BlockSpec((1, TM, D), lambda i: (*tile(i - 1), 0)), rows, rows],
        out_shape=[jax.ShapeDtypeStruct((B, S, D), F32),
                   jax.ShapeDtypeStruct((R, D), F32), jax.ShapeDtypeStruct((R, D), F32)],
        scratch_shapes=[pltpu.VMEM((TM, D), BF16), pltpu.VMEM((TM, D), BF16), pltpu.VMEM((TM, D), BF16),
                        pltpu.VMEM((TM, D), F32), pltpu.VMEM((TM, D), F32),
                        pltpu.VMEM(w_in.shape[1:], BF16), pltpu.VMEM(w_out.shape[1:], BF16),
                        pltpu.VMEM((2, CAST_ROWS, w_in.shape[2]), F32), pltpu.SemaphoreType.DMA((2,))],
        compiler_params=_params(1),
        name="attn_layer",
    )(x, *tables, bias, kdup, vt, xs, *tables_s, w_in, sinks, w_out, g[0], b[0])


def _attn_sample_kernel(q_ref, kn_ref, vn_ref, ck_ref, cv_ref, sink_ref, o_ref, nk_ref, nv_ref):
    nb = q_ref.shape[0]
    head_of_lane = lax.broadcasted_iota(jnp.int32, (N_HEADS, D_MODEL), 1) // HEAD_DIM
    own_head = head_of_lane == lax.broadcasted_iota(jnp.int32, (N_HEADS, D_MODEL), 0)
    low_half = lax.broadcasted_iota(jnp.int32, (N_HEADS, LANES), 1) < HEAD_DIM
    last_row = lax.broadcasted_iota(jnp.int32, (WINDOW, KV_DIM), 0) == WINDOW - 1
    sink = sink_ref[...]
    heads_per_slab = LANES // HEAD_DIM
    slabs_per_group = GROUP // heads_per_slab

    def body(i, carry):
        newk = jnp.where(last_row, kn_ref[pl.ds(i, 1), :], pltpu.roll(ck_ref[i], WINDOW - 1, 0))
        newv = jnp.where(last_row, vn_ref[pl.ds(i, 1), :], pltpu.roll(cv_ref[i], WINDOW - 1, 0))
        nk_ref[i] = newk
        nv_ref[i] = newv
        qh = jnp.where(own_head, jnp.broadcast_to(q_ref[pl.ds(i, 1), :], (N_HEADS, D_MODEL)), 0.0)
        folded = []
        for kv in range(N_KV_HEADS):
            w = qh[:, kv * GROUP * HEAD_DIM:kv * GROUP * HEAD_DIM + LANES]
            for sl in range(1, slabs_per_group):
                c0 = kv * GROUP * HEAD_DIM + sl * LANES
                w = w + qh[:, c0:c0 + LANES]
            folded.append(w + pltpu.roll(w, HEAD_DIM, 1))
        qg = jnp.concatenate([jnp.where(low_half, folded[2 * j], folded[2 * j + 1])
                              for j in range(N_KV_HEADS // 2)], axis=1)
        s = lax.dot_general(qg.astype(BF16), newk.astype(BF16), (((1,), (1,)), ((), ())),
                            preferred_element_type=F32)
        m = jnp.maximum(jnp.max(s, axis=-1, keepdims=True), sink)
        p = jnp.exp(s - m)
        denom = jnp.sum(p, axis=-1, keepdims=True) + jnp.exp(sink - m)
        og = jnp.dot(p.astype(BF16), newv.astype(BF16), preferred_element_type=F32) / denom
        slabs = []
        for kv in range(N_KV_HEADS):
            xs = og[:, (kv // 2) * LANES:(kv // 2 + 1) * LANES]
            rolled = pltpu.roll(xs, HEAD_DIM, 1)
            both = jnp.where(low_half, xs, rolled) if kv % 2 == 0 else jnp.where(low_half, rolled, xs)
            slabs.extend([both] * slabs_per_group)
        full = jnp.concatenate(slabs, axis=1)
        o_ref[pl.ds(i, 1), :] = jnp.sum(jnp.where(own_head, full, 0.0), axis=0, keepdims=True)
        return carry

    lax.fori_loop(0, nb, body, 0, unroll=True)


def _attn_sample(q, kn, vn, ck, cv, sinks_col):
    R, D = q.shape
    nb = SAMPLE_ATTN_BATCH
    row2 = lambda w: pl.BlockSpec((nb, w), lambda i: (i, 0))
    cache = pl.BlockSpec((nb, WINDOW, KV_DIM), lambda i: (i, 0, 0))
    return pl.pallas_call(
        _attn_sample_kernel,
        grid=(R // nb,),
        in_specs=[row2(D), row2(KV_DIM), row2(KV_DIM), cache, cache, _resident(sinks_col.shape)],
        out_specs=[row2(D), cache, cache],
        out_shape=[jax.ShapeDtypeStruct((R, D), F32),
                   jax.ShapeDtypeStruct(ck.shape, F32), jax.ShapeDtypeStruct(cv.shape, F32)],
        compiler_params=_params(1),
        name="attn_sample",
    )(q, kn, vn, ck, cv, sinks_col)


def _gated_out_sample_kernel(x_ref, a_ref, gate_ref, wout_ref, g_ref, b_ref, o_ref, h_buf):
    rows = x_ref.shape[0]
    h_buf[...] = (a_ref[...] * _silu(gate_ref[...])).astype(BF16)

    def set_rows(rs, cs, v):
        o_ref[rs, cs] = v

    for n in range(N_CHUNKS):
        cols = slice(n * COL_CHUNK, (n + 1) * COL_CHUNK)
        y = jnp.dot(h_buf[...], wout_ref[:, cols].astype(BF16), preferred_element_type=F32)
        o_ref[:, cols] = ALPHA * x_ref[:, cols] + y
    _layer_norm(lambda rs, cs: o_ref[rs, cs], set_rows, g_ref, b_ref, rows)


def _gated_out_sample(x, a, gate, wout, g, b):
    R, D = x.shape
    return pl.pallas_call(
        _gated_out_sample_kernel,
        grid=(1,),
        in_specs=[_resident(t.shape) for t in (x, a, gate)] + [wout[1], g[1], b[1]],
        out_specs=pl.BlockSpec((R, D), lambda i: (0, 0)),
        out_shape=jax.ShapeDtypeStruct((R, D), F32),
        scratch_shapes=[pltpu.VMEM((R, D), BF16)],
        compiler_params=_params(1),
        name="gated_out_sample",
    )(x, a, gate, wout[0], g[0], b[0])


def _rope_tables(pos):
    half = ROT_DIM // 2
    inv_freq = ROPE_THETA ** (-jnp.arange(0, ROT_DIM, 2, dtype=F32) / ROT_DIM)
    ang = pos.astype(F32)[:, None] * inv_freq[None, :]
    cos, sin = jnp.cos(ang), jnp.sin(ang)
    n = pos.shape[0]
    rest = jnp.zeros((n, HEAD_DIM - ROT_DIM), F32)
    zero = jnp.zeros((n, half), F32)
    cos_h = jnp.concatenate([cos, cos, rest + 1.0], axis=1)
    sa_h = jnp.concatenate([-sin, zero, rest], axis=1)
    sb_h = jnp.concatenate([zero, sin, rest], axis=1)
    rep = LANES // HEAD_DIM
    return tuple(jnp.tile(a, (1, rep)) for a in (cos_h, sa_h, sb_h))


def kernel(x_prompt, x_sample, state_pool, cache_k, cache_v, w_in_a, w_grp_a, scale_a, w_out_a,
           w_kv, w_in_b, sinks_b, w_out_b, ln_g, ln_b):
    B, S, D = x_prompt.shape
    R = x_sample.shape[0]
    xp = x_prompt
    xs = x_sample.reshape(R, D)
    tab_p = _rope_tables(jnp.arange(S, dtype=jnp.int32))
    tab_s = _rope_tables(jnp.full((R,), PAST_LEN, jnp.int32))
    qtab_p = tuple(a * SM_SCALE for a in tab_p)
    qtab_s = tuple(a * SM_SCALE for a in tab_s)
    w_grp_a = w_grp_a.astype(BF16)
    scale_a, ln_g, ln_b = (p.reshape(p.shape[0], 1, D) for p in (scale_a, ln_g, ln_b))
    state_rows = state_pool.transpose(0, 2, 1, 3)
    pool_p, pool_s = [], []
    for i in range(N_A_LAYERS):
        params = [_layer(p, i) for p in (w_grp_a, scale_a, ln_g, ln_b)]
        xp, sp, xs, ss = _pool_layer(xp, xs, state_rows, i, w_in_a, w_out_a, params)
        pool_p.append(sp[:, HALO - POOL_STATE:])
        pool_s.append(ss)
    wkv = w_kv.astype(BF16)
    new_k_p, new_v_p, kdup_p, vt_p = _kv_prompt(xp, tab_p, wkv)
    k_s, v_s = _kv_sample(xs, tab_s, wkv)
    bias = _band_bias()
    ck = cache_k.reshape(R, WINDOW, KV_DIM)
    cv = cache_v.reshape(R, WINDOW, KV_DIM)
    for j in range(DEPTH - N_A_LAYERS):
        i = N_A_LAYERS + j
        wout, g, b = _layer(w_out_b, j), _layer(ln_g, i), _layer(ln_b, i)
        xp, q_s, gate_s = _attn_layer(xp, qtab_p, bias, kdup_p, vt_p, xs, qtab_s, j, w_in_b, sinks_b[j], w_out_b, g, b)
        a_s, nk, nv = _attn_sample(q_s, k_s, v_s, ck, cv, sinks_b[j][:, None])
        xs = _gated_out_sample(xs, a_s, gate_s, wout, g, b)
    kv4 = (N_KV_HEADS, HEAD_DIM)
    return (xp, xs.reshape(R, 1, D), jnp.stack(pool_p, axis=0), jnp.stack(pool_s, axis=0).transpose(0, 2, 1, 3),
            new_k_p.reshape(B, WINDOW, *kv4), new_v_p.reshape(B, WINDOW, *kv4),
            nk.reshape(R, WINDOW, *kv4), nv.reshape(R, WINDOW, *kv4))
```

```python
import functools

import jax
import jax.numpy as jnp
import numpy as np
from jax import lax
from jax.experimental import pallas as pl
from jax.experimental.pallas import tpu as pltpu

F32 = jnp.float32
BF16 = jnp.bfloat16

D_MODEL = 2048
DEPTH = 4
PAST_LEN = 16384
N_A_LAYERS = DEPTH // 2
POOL_WINDOWS = (2, 4, 8, 16)
POOL_GROUP = D_MODEL // len(POOL_WINDOWS)
POOL_STATE = max(POOL_WINDOWS) - 1
HEAD_DIM = 64
N_HEADS = D_MODEL // HEAD_DIM
N_KV_HEADS = N_HEADS // 8
GROUP = N_HEADS // N_KV_HEADS
KV_DIM = N_KV_HEADS * HEAD_DIM
WINDOW = 128
ROT_DIM = HEAD_DIM // 4
ROPE_THETA = 500000.0
ALPHA = (2 * DEPTH) ** 0.25
LN_EPS = 1e-5
NEG = -1e30
SM_SCALE = HEAD_DIM ** -0.5

LANES = 128
SUBLANES = 8
HALO = 16
COL_CHUNK = 512
N_CHUNKS = D_MODEL // COL_CHUNK
TM = 256
TK = 1024
LN_ROWS = 16
SAMPLE_ATTN_BATCH = 8
SUM_ROWS = 16
CAST_ROWS = 128
VMEM_LIMIT_BYTES = 56 * 1024 * 1024


def _params(n_axes):
    return pltpu.CompilerParams(dimension_semantics=("arbitrary",) * n_axes,
                                vmem_limit_bytes=VMEM_LIMIT_BYTES)


def _resident(shape):
    zeros = (0,) * len(shape)
    return pl.BlockSpec(shape, lambda *_: zeros, pipeline_mode=pl.Buffered(1))


def _layer(stacked, i):
    zeros = (0,) * (stacked.ndim - 1)
    return stacked, pl.BlockSpec((None,) + stacked.shape[1:], lambda *_: (i,) + zeros,
                                 pipeline_mode=pl.Buffered(1))


def _silu(g):
    return g / (1.0 + jnp.exp(-g))


def _rope(x, cos, sa, sb):
    return x * cos + pltpu.roll(x, LANES - ROT_DIM // 2, 1) * sa + pltpu.roll(x, ROT_DIM // 2, 1) * sb


def _load_as_bf16(w_hbm, dst_ref, stage, sems):
    n_cols = dst_ref.shape[1]
    chunk = stage.shape[1]
    n_chunks = dst_ref.shape[0] // chunk

    def copy(c):
        return pltpu.make_async_copy(w_hbm.at[pl.ds(c * chunk, chunk)],
                                     stage.at[c % 2, :, pl.ds(0, n_cols)], sems.at[c % 2])

    copy(0).start()
    for c in range(n_chunks):
        if c + 1 < n_chunks:
            copy(c + 1).start()
        copy(c).wait()
        dst_ref[c * chunk:(c + 1) * chunk, :] = stage[c % 2, :, 0:n_cols].astype(BF16)


def _outproj_residual(h_ref, x_rows, wout_ref, r_rows_set, rows):
    for n in range(N_CHUNKS):
        cols = slice(n * COL_CHUNK, (n + 1) * COL_CHUNK)
        y = jnp.dot(h_ref[0:rows, :], wout_ref[:, cols], preferred_element_type=F32)
        r_rows_set(slice(0, rows), cols, ALPHA * x_rows(slice(0, rows), cols) + y)


def _zero_after(v):
    bits = pltpu.bitcast(v, jnp.int32)
    half = jnp.full(bits.shape, 16, jnp.int32)
    return lax.shift_right_logical(lax.shift_right_logical(bits, half), half).astype(F32)


def _layer_norm(r_rows_get, o_rows_set, g_ref, b_ref, rows):
    step = min(LN_ROWS, rows)
    anchors = []
    for r0 in range(0, rows, step):
        rs = slice(r0, r0 + step)
        r = r_rows_get(rs, slice(None))
        mu = jnp.mean(r, axis=-1, keepdims=True)
        c = r - mu
        var = jnp.mean(c * c, axis=-1, keepdims=True)
        out = c * lax.rsqrt(var + LN_EPS) * g_ref[...] + b_ref[...]
        o_rows_set(rs, slice(None), out)
        folded = sum(out[i:i + SUBLANES, j:j + LANES]
                     for i in range(0, step, SUBLANES) for j in range(0, out.shape[1], LANES))
        anchors.append(_zero_after(folded))
    return anchors


def _outproj_ln(h_ref, x_rows, wout_ref, g_ref, b_ref, o_rows_set, o_rows_get, rows):
    _outproj_residual(h_ref, x_rows, wout_ref, o_rows_set, rows)
    _layer_norm(o_rows_get, o_rows_set, g_ref, b_ref, rows)


def _pool_layer_kernel(layer, tiles_per_batch,
                       x_ref, xs_ref, st_hbm, win_hbm, wgrp_ref, scale_ref, wout_hbm, g_ref, b_ref,
                       o_ref, state_ref, ys_ref, nst_hbm,
                       xb_buf, u_buf, h_buf, r_buf, st_buf, us_buf, win_ref, wout_ref, stage, sems, wsems):
    step = pl.program_id(0)
    n_prompt = pl.num_programs(0) - 1
    rows_s = xs_ref.shape[0]

    def mix(xb, d, g):
        cols = slice(g * POOL_GROUP, (g + 1) * POOL_GROUP)
        gate = jnp.dot(xb, win_ref[:, D_MODEL + g * POOL_GROUP:D_MODEL + (g + 1) * POOL_GROUP],
                       preferred_element_type=F32)
        d = jnp.dot(d.astype(BF16), wgrp_ref[g], preferred_element_type=F32) * scale_ref[:, cols]
        return (d * _silu(gate)).astype(BF16)

    def set_out(rs, cs, v):
        o_ref[0, rs, cs] = v

    def set_r(rs, cs, v):
        r_buf[rs, cs] = v

    def norm_previous_tile():
        return _layer_norm(lambda rs, cs: r_buf[rs, cs], set_out, g_ref, b_ref, TM)

    @pl.when(step == 0)
    def _():
        r_buf[...] = jnp.zeros(r_buf.shape, F32)
        _load_as_bf16(win_hbm.at[layer], win_ref, stage, wsems)
        _load_as_bf16(wout_hbm.at[layer], wout_ref, stage, wsems)

    @pl.when(step < n_prompt)
    def _():
        t = step % tiles_per_batch

        @pl.when(t == 0)
        def _():
            u_buf[0:HALO, :] = jnp.zeros((HALO, D_MODEL), F32)

        xb_buf[...] = x_ref[0].astype(BF16)
        row = lax.broadcasted_iota(jnp.int32, (TM, 1), 0) + t * TM

        def project_u(g):
            cols = slice(g * POOL_GROUP, (g + 1) * POOL_GROUP)
            u_buf[HALO:, cols] = jnp.dot(xb_buf[...], win_ref[:, cols], preferred_element_type=F32)

        project_u(0)
        anchors = norm_previous_tile()
        per_group = len(anchors) // len(POOL_WINDOWS)
        for g, w in enumerate(POOL_WINDOWS):
            cols = slice(g * POOL_GROUP, (g + 1) * POOL_GROUP)
            if g + 1 < len(POOL_WINDOWS):
                project_u(g + 1)
            ext = u_buf[:, cols]
            s = ext
            shift = 1
            while shift < w:
                s = s + pltpu.roll(s, shift, 0)
                shift *= 2
            inv_cnt = 1.0 / jnp.minimum(w, row + 1).astype(F32)
            anchor = sum(anchors[g * per_group:(g + 1) * per_group])
            anchor = jnp.tile(anchor, (TM // anchor.shape[0], POOL_GROUP // anchor.shape[1]))
            h_buf[:, cols] = mix(xb_buf[...], s[HALO:, :] * inv_cnt - ext[HALO:, :] + anchor, g)

        state_ref[0] = u_buf[TM:TM + HALO, :]
        u_buf[0:HALO, :] = u_buf[TM:TM + HALO, :]
        _outproj_residual(h_buf, lambda rs, cs: x_ref[0, rs, cs], wout_ref, set_r, TM)

    @pl.when(step == n_prompt)
    def _():
        load = pltpu.make_async_copy(st_hbm.at[layer], st_buf, sems.at[0])
        shift_old = pltpu.make_async_copy(st_buf.at[pl.ds(1, POOL_STATE - 1)],
                                          nst_hbm.at[pl.ds(0, POOL_STATE - 1)], sems.at[1])
        append_new = pltpu.make_async_copy(us_buf, nst_hbm.at[POOL_STATE - 1], sems.at[2])
        load.start()
        norm_previous_tile()
        xb = xs_ref[...].astype(BF16)
        load.wait()
        shift_old.start()
        for g, w in enumerate(POOL_WINDOWS):
            cols = slice(g * POOL_GROUP, (g + 1) * POOL_GROUP)
            u = jnp.dot(xb, win_ref[:, cols], preferred_element_type=F32)
            us_buf[:, cols] = u
            acc = u
            for j in range(1, w):
                acc = acc + st_buf[POOL_STATE - j, :, cols]
            h_buf[0:rows_s, cols] = mix(xb, acc * (1.0 / min(w, PAST_LEN + 1)) - u, g)
        append_new.start()

        def set_ys(rs, cs, v):
            ys_ref[rs, cs] = v

        _outproj_ln(h_buf, lambda rs, cs: xs_ref[rs, cs], wout_ref, g_ref, b_ref,
                    set_ys, lambda rs, cs: ys_ref[rs, cs], rows_s)
        shift_old.wait()
        append_new.wait()


def _pool_layer(x, xs, state_rows, layer, w_in, w_out, params):
    B, S, D = x.shape
    R = xs.shape[0]
    tiles_per_batch = S // TM
    n_prompt = B * tiles_per_batch

    def tile(step):
        step = jnp.clip(step, 0, n_prompt - 1)
        return step // tiles_per_batch, step % tiles_per_batch

    hbm = pl.BlockSpec(memory_space=pl.ANY)
    (wgrp, wgrp_spec), (scale, scale_spec), (g, g_spec), (b, b_spec) = params
    return pl.pallas_call(
        functools.partial(_pool_layer_kernel, layer, tiles_per_batch),
        grid=(n_prompt + 1,),
        in_specs=[pl.BlockSpec((1, TM, D), lambda i: (*tile(i), 0)), _resident(xs.shape), hbm,
                  hbm, wgrp_spec, scale_spec, hbm, g_spec, b_spec],
        out_specs=[
            pl.BlockSpec((1, TM, D), lambda i: (*tile(i - 1), 0)),
            pl.BlockSpec((1, HALO, D), lambda i: (tile(i)[0], 0, 0)),
            pl.BlockSpec((R, D), lambda i: (0, 0)),
            pl.BlockSpec(memory_space=pl.ANY),
        ],
        out_shape=[jax.ShapeDtypeStruct((B, S, D), F32), jax.ShapeDtypeStruct((B, HALO, D), F32),
                   jax.ShapeDtypeStruct((R, D), F32), jax.ShapeDtypeStruct((POOL_STATE, R, D), F32)],
        scratch_shapes=[pltpu.VMEM((TM, D), BF16), pltpu.VMEM((HALO + TM, D), F32), pltpu.VMEM((TM, D), BF16),
                        pltpu.VMEM((TM, D), F32), pltpu.VMEM((POOL_STATE, R, D), F32), pltpu.VMEM((R, D), F32),
                        pltpu.VMEM(w_in.shape[1:], BF16), pltpu.VMEM(w_out.shape[1:], BF16),
                        pltpu.VMEM((2, CAST_ROWS, w_in.shape[2]), F32),
                        pltpu.SemaphoreType.DMA((3,)), pltpu.SemaphoreType.DMA((2,))],
        compiler_params=_params(1),
        name="pool_layer",
    )(x, xs, state_rows, w_in, wgrp, scale, w_out, g, b)


def _project_kv(xb, cos, sa, sb, wkv_ref):
    kv = jnp.dot(xb, wkv_ref[...], preferred_element_type=F32)
    k_slabs = [_rope(kv[:, j * LANES:(j + 1) * LANES], cos, sa, sb) for j in range(KV_DIM // LANES)]
    return k_slabs, kv[:, KV_DIM:]


def _kv_prompt_kernel(x_ref, cos_ref, sa_ref, sb_ref, wkv_ref, knew_ref, vnew_ref, kdup_ref, vt_ref):
    k_slabs, v = _project_kv(x_ref[0].astype(BF16), cos_ref[...], sa_ref[...], sb_ref[...], wkv_ref)
    low = lax.broadcasted_iota(jnp.int32, (TK, LANES), 1) < HEAD_DIM
    for j, k in enumerate(k_slabs):
        swapped = pltpu.roll(k, HEAD_DIM, 1)
        kdup_ref[0, :, (2 * j) * LANES:(2 * j + 1) * LANES] = jnp.where(low, k, swapped).astype(BF16)
        kdup_ref[0, :, (2 * j + 1) * LANES:(2 * j + 2) * LANES] = jnp.where(low, swapped, k).astype(BF16)
    for i in range(TK // WINDOW):
        vt_ref[0, i] = v[i * WINDOW:(i + 1) * WINDOW, :].T.astype(BF16)

    @pl.when(pl.program_id(1) == pl.num_programs(1) - 1)
    def _():
        for j, k in enumerate(k_slabs):
            knew_ref[0, :, j * LANES:(j + 1) * LANES] = k[TK - WINDOW:, :]
        vnew_ref[0] = v[TK - WINDOW:, :]


def _kv_prompt(x, tables, wkv):
    B, S, D = x.shape
    tab = pl.BlockSpec((TK, LANES), lambda bi, t: (t, 0))
    last = pl.BlockSpec((1, WINDOW, KV_DIM), lambda bi, t: (bi, 0, 0))
    return pl.pallas_call(
        _kv_prompt_kernel,
        grid=(B, S // TK),
        in_specs=[pl.BlockSpec((1, TK, D), lambda bi, t: (bi, t, 0)), tab, tab, tab, _resident(wkv.shape)],
        out_specs=[last, last,
                   pl.BlockSpec((1, TK, N_KV_HEADS * LANES), lambda bi, t: (bi, t, 0)),
                   pl.BlockSpec((1, TK // WINDOW, KV_DIM, WINDOW), lambda bi, t: (bi, t, 0, 0))],
        out_shape=[jax.ShapeDtypeStruct((B, WINDOW, KV_DIM), F32), jax.ShapeDtypeStruct((B, WINDOW, KV_DIM), F32),
                   jax.ShapeDtypeStruct((B, S, N_KV_HEADS * LANES), BF16),
                   jax.ShapeDtypeStruct((B, S // WINDOW, KV_DIM, WINDOW), BF16)],
        compiler_params=_params(2),
        name="kv_prompt",
    )(x, *tables, wkv)


def _kv_sample_kernel(x_ref, cos_ref, sa_ref, sb_ref, wkv_ref, k_ref, v_ref):
    k_slabs, v = _project_kv(x_ref[...].astype(BF16), cos_ref[...], sa_ref[...], sb_ref[...], wkv_ref)
    for j, k in enumerate(k_slabs):
        k_ref[:, j * LANES:(j + 1) * LANES] = k
    v_ref[...] = v


def _kv_sample(x, tables, wkv):
    R, D = x.shape
    out = pl.BlockSpec((R, KV_DIM), lambda i: (0, 0))
    return pl.pallas_call(
        _kv_sample_kernel,
        grid=(1,),
        in_specs=[_resident(x.shape)] + [_resident(t.shape) for t in tables] + [_resident(wkv.shape)],
        out_specs=[out, out],
        out_shape=[jax.ShapeDtypeStruct((R, KV_DIM), F32), jax.ShapeDtypeStruct((R, KV_DIM), F32)],
        compiler_params=_params(1),
        name="kv_sample",
    )(x, *tables, wkv)


def _attn_layer_kernel(layer, tiles_per_batch,
                       x_ref, cos_ref, sa_ref, sb_ref, bias_ref, kdup_ref, vt_ref, xs_ref, cos_s_ref, sa_s_ref, sb_s_ref,
                       win_hbm, sink_ref, wout_hbm, g_ref, b_ref,
                       o_ref, qs_ref, gs_ref, xb_buf, q_buf, h_buf, g_buf, r_buf, win_ref, wout_ref, stage, wsems):
    step = pl.program_id(0)
    n_prompt = pl.num_programs(0) - 1
    group_cols = GROUP * HEAD_DIM

    def set_out(rs, cs, v):
        o_ref[0, rs, cs] = v

    def set_r(rs, cs, v):
        r_buf[rs, cs] = v

    def norm_previous_tile():
        return _layer_norm(lambda rs, cs: r_buf[rs, cs], set_out, g_ref, b_ref, TM)

    @pl.when(step == 0)
    def _():
        r_buf[...] = jnp.zeros(r_buf.shape, F32)
        _load_as_bf16(win_hbm.at[layer], win_ref, stage, wsems)
        _load_as_bf16(wout_hbm.at[layer], wout_ref, stage, wsems)

    @pl.when(step < n_prompt)
    def _():
        t = step % tiles_per_batch
        xb_buf[...] = x_ref[0].astype(BF16)

        def project_q(kv, anchor):
            q = jnp.dot(xb_buf[...], win_ref[:, kv * group_cols:(kv + 1) * group_cols], preferred_element_type=F32)
            if anchor is not None:
                q = q + anchor
            for j in range(group_cols // LANES):
                qj = _rope(q[:, j * LANES:(j + 1) * LANES], cos_ref[...], sa_ref[...], sb_ref[...])
                c0 = kv * group_cols + j * LANES
                q_buf[:, c0:c0 + LANES] = qj.astype(BF16)

        def project_gate(kv, anchor):
            cols = slice(kv * group_cols, (kv + 1) * group_cols)
            gate = jnp.dot(xb_buf[...], win_ref[:, D_MODEL + kv * group_cols:D_MODEL + (kv + 1) * group_cols],
                           preferred_element_type=F32)
            g_buf[:, cols] = _silu(gate) if anchor is None else _silu(gate) + anchor

        low_half = lax.broadcasted_iota(jnp.int32, (WINDOW, LANES), 1) < HEAD_DIM
        ones_rows = jnp.ones((SUM_ROWS, 2 * WINDOW), BF16)

        def block_ids(qb):
            blk = t * (TM // WINDOW) + qb
            return blk, jnp.maximum(blk - 1, 0)

        def group_heads(kv):
            return [(kv * (GROUP // 2) + pair, par) for pair in range(GROUP // 2) for par in range(2)]

        def scores(qb, kv):
            blk, prev_blk = block_ids(qb)
            prev = pl.multiple_of(prev_blk * WINDOW, WINDOW)
            cur = pl.multiple_of(blk * WINDOW, WINDOW)
            rows = slice(qb * WINDOW, (qb + 1) * WINDOW)
            ks = slice(kv * LANES, (kv + 1) * LANES)
            k2 = jnp.concatenate([kdup_ref[0, pl.ds(prev, WINDOW), ks], kdup_ref[0, pl.ds(cur, WINDOW), ks]], axis=0)
            q_all = []
            for slab, par in group_heads(kv):
                q_slab = q_buf[rows, slab * LANES:(slab + 1) * LANES]
                q_all.append(jnp.where(low_half == (par == 0), q_slab, jnp.zeros_like(q_slab)))
            return lax.dot_general(k2, jnp.concatenate(q_all, axis=0), (((1,), (1,)), ((), ())),
                                   preferred_element_type=F32)

        def finish(qb, kv, s_t):
            blk, prev_blk = block_ids(qb)
            bias = bias_ref[jnp.minimum(blk, 1)]
            rows = slice(qb * WINDOW, (qb + 1) * WINDOW)
            vs = slice(kv * HEAD_DIM, (kv + 1) * HEAD_DIM)
            v_aug = jnp.concatenate([vt_ref[0, prev_blk, vs, :], vt_ref[0, blk, vs, :]], axis=1)
            v_aug = jnp.concatenate([v_aug, ones_rows], axis=0)
            for pair in range(GROUP // 2):
                slab = kv * (GROUP // 2) + pair
                p_t, sink_terms = [], []
                for par in range(2):
                    i = 2 * pair + par
                    s = s_t[:, i * WINDOW:(i + 1) * WINDOW] + bias
                    sink = sink_ref[2 * slab + par]
                    m = jnp.maximum(jnp.max(s, axis=0, keepdims=True), sink)
                    p_t.append(jnp.exp(s - m).astype(BF16))
                    sink_terms.append(jnp.exp(sink - m))
                o_t = jnp.dot(v_aug, jnp.concatenate(p_t, axis=1), preferred_element_type=F32)
                both = []
                for par in range(2):
                    cs = slice(par * WINDOW, (par + 1) * WINDOW)
                    inv = 1.0 / (o_t[HEAD_DIM:HEAD_DIM + 1, cs] + sink_terms[par])
                    both.append(o_t[:HEAD_DIM, cs] * inv)
                attn = jnp.concatenate(both, axis=0).T
                cs = slice(slab * LANES, (slab + 1) * LANES)
                h_buf[rows, cs] = (attn * g_buf[rows, cs]).astype(BF16)

        project_q(0, None)
        anchors = norm_previous_tile()
        early = 2 * (N_KV_HEADS // 2)
        per_proj = len(anchors) // early

        def next_anchor():
            if not anchors:
                return None
            anchor = sum(anchors.pop(0) for _ in range(per_proj))
            return jnp.tile(anchor, (TM // anchor.shape[0], group_cols // anchor.shape[1]))

        for kv in range(N_KV_HEADS):
            s_t = [scores(qb, kv) for qb in range(TM // WINDOW)]
            if kv + 1 < N_KV_HEADS:
                project_q(kv + 1, next_anchor())
            project_gate(kv, next_anchor())
            for qb in range(TM // WINDOW):
                finish(qb, kv, s_t[qb])
        _outproj_residual(h_buf, lambda rs, cs: x_ref[0, rs, cs], wout_ref, set_r, TM)

    @pl.when(step == n_prompt)
    def _():
        norm_previous_tile()
        xb = xs_ref[...].astype(BF16)
        for n in range(N_CHUNKS):
            q = jnp.dot(xb, win_ref[:, n * COL_CHUNK:(n + 1) * COL_CHUNK], preferred_element_type=F32)
            for j in range(COL_CHUNK // LANES):
                c0 = n * COL_CHUNK + j * LANES
                qs_ref[:, c0:c0 + LANES] = _rope(q[:, j * LANES:(j + 1) * LANES],
                                                 cos_s_ref[...], sa_s_ref[...], sb_s_ref[...])
            gs_ref[:, n * COL_CHUNK:(n + 1) * COL_CHUNK] = jnp.dot(
                xb, win_ref[:, D_MODEL + n * COL_CHUNK:D_MODEL + (n + 1) * COL_CHUNK],
                preferred_element_type=F32)


def _band_bias():
    key = np.arange(2 * WINDOW)[:, None]
    qry = np.arange(WINDOW)[None, :]
    band = (key > qry) & (key <= qry + WINDOW)
    return jnp.asarray(np.where(np.stack([band & (key >= WINDOW), band]), 0.0, NEG).astype(np.float32))


def _attn_layer(x, tables, bias, kdup, vt, xs, tables_s, layer, w_in, sinks, w_out, g, b):
    B, S, D = x.shape
    R = xs.shape[0]
    tiles_per_batch = S // TM
    n_prompt = B * tiles_per_batch

    def tile(step):
        step = jnp.clip(step, 0, n_prompt - 1)
        return step // tiles_per_batch, step % tiles_per_batch

    tab = pl.BlockSpec((TM, LANES), lambda i: (tile(i)[1], 0))

    def per_batch(a):
        return pl.BlockSpec((1,) + a.shape[1:], lambda i: (tile(i)[0],) + (0,) * (a.ndim - 1),
                            pipeline_mode=pl.Buffered(1))

    rows = pl.BlockSpec((R, D), lambda i: (0, 0))
    return pl.pallas_call(
        functools.partial(_attn_layer_kernel, layer, tiles_per_batch),
        grid=(n_prompt + 1,),
        in_specs=[
            pl.BlockSpec((1, TM, D), lambda i: (*tile(i), 0)), tab, tab, tab, _resident(bias.shape),
            per_batch(kdup), per_batch(vt),
            _resident(xs.shape)] + [_resident(t.shape) for t in tables_s] + [
            pl.BlockSpec(memory_space=pl.ANY), pl.BlockSpec(memory_space=pltpu.SMEM),
            pl.BlockSpec(memory_space=pl.ANY), g[1], b[1],
        ],
        out_specs=[pl.BlockSpec((1, TM, D), lambda i: (*tile(i - 1), 0)), rows, rows],
        out_shape=[jax.ShapeDtypeStruct((B, S, D), F32),
                   jax.ShapeDtypeStruct((R, D), F32), jax.ShapeDtypeStruct((R, D), F32)],
        scratch_shapes=[pltpu.VMEM((TM, D), BF16), pltpu.VMEM((TM, D), BF16), pltpu.VMEM((TM, D), BF16),
                        pltpu.VMEM((TM, D), F32), pltpu.VMEM((TM, D), F32),
                        pltpu.VMEM(w_in.shape[1:], BF16), pltpu.VMEM(w_out.shape[1:], BF16),
                        pltpu.VMEM((2, CAST_ROWS, w_in.shape[2]), F32), pltpu.SemaphoreType.DMA((2,))],
        compiler_params=_params(1),
        name="attn_layer",
    )(x, *tables, bias, kdup, vt, xs, *tables_s, w_in, sinks, w_out, g[0], b[0])


def _attn_sample_kernel(q_ref, kn_ref, vn_ref, ck_ref, cv_ref, sink_ref, o_ref, nk_ref, nv_ref):
    nb = q_ref.shape[0]
    head_of_lane = lax.broadcasted_iota(jnp.int32, (N_HEADS, D_MODEL), 1) // HEAD_DIM
    own_head = head_of_lane == lax.broadcasted_iota(jnp.int32, (N_HEADS, D_MODEL), 0)
    low_half = lax.broadcasted_iota(jnp.int32, (N_HEADS, LANES), 1) < HEAD_DIM
    last_row = lax.broadcasted_iota(jnp.int32, (WINDOW, KV_DIM), 0) == WINDOW - 1
    sink = sink_ref[...]
    heads_per_slab = LANES // HEAD_DIM
    slabs_per_group = GROUP // heads_per_slab

    def body(i, carry):
        newk = jnp.where(last_row, kn_ref[pl.ds(i, 1), :], pltpu.roll(ck_ref[i], WINDOW - 1, 0))
        newv = jnp.where(last_row, vn_ref[pl.ds(i, 1), :], pltpu.roll(cv_ref[i], WINDOW - 1, 0))
        nk_ref[i] = newk
        nv_ref[i] = newv
        qh = jnp.where(own_head, jnp.broadcast_to(q_ref[pl.ds(i, 1), :], (N_HEADS, D_MODEL)), 0.0)
        folded = []
        for kv in range(N_KV_HEADS):
            w = qh[:, kv * GROUP * HEAD_DIM:kv * GROUP * HEAD_DIM + LANES]
            for sl in range(1, slabs_per_group):
                c0 = kv * GROUP * HEAD_DIM + sl * LANES
                w = w + qh[:, c0:c0 + LANES]
            folded.append(w + pltpu.roll(w, HEAD_DIM, 1))
        qg = jnp.concatenate([jnp.where(low_half, folded[2 * j], folded[2 * j + 1])
                              for j in range(N_KV_HEADS // 2)], axis=1)
        s = lax.dot_general(qg.astype(BF16), newk.astype(BF16), (((1,), (1,)), ((), ())),
                            preferred_element_type=F32)
        m = jnp.maximum(jnp.max(s, axis=-1, keepdims=True), sink)
        p = jnp.exp(s - m)
        denom = jnp.sum(p, axis=-1, keepdims=True) + jnp.exp(sink - m)
        og = jnp.dot(p.astype(BF16), newv.astype(BF16), preferred_element_type=F32) / denom
        slabs = []
        for kv in range(N_KV_HEADS):
            xs = og[:, (kv // 2) * LANES:(kv // 2 + 1) * LANES]
            rolled = pltpu.roll(xs, HEAD_DIM, 1)
            both = jnp.where(low_half, xs, rolled) if kv % 2 == 0 else jnp.where(low_half, rolled, xs)
            slabs.extend([both] * slabs_per_group)
        full = jnp.concatenate(slabs, axis=1)
        o_ref[pl.ds(i, 1), :] = jnp.sum(jnp.where(own_head, full, 0.0), axis=0, keepdims=True)
        return carry

    lax.fori_loop(0, nb, body, 0, unroll=True)


def _attn_sample(q, kn, vn, ck, cv, sinks_col):
    R, D = q.shape
    nb = SAMPLE_ATTN_BATCH
    row2 = lambda w: pl.BlockSpec((nb, w), lambda i: (i, 0))
    cache = pl.BlockSpec((nb, WINDOW, KV_DIM), lambda i: (i, 0, 0))
    return pl.pallas_call(
        _attn_sample_kernel,
        grid=(R // nb,),
        in_specs=[row2(D), row2(KV_DIM), row2(KV_DIM), cache, cache, _resident(sinks_col.shape)],
        out_specs=[row2(D), cache, cache],
        out_shape=[jax.ShapeDtypeStruct((R, D), F32),
                   jax.ShapeDtypeStruct(ck.shape, F32), jax.ShapeDtypeStruct(cv.shape, F32)],
        compiler_params=_params(1),
        name="attn_sample",
    )(q, kn, vn, ck, cv, sinks_col)


def _gated_out_sample_kernel(x_ref, a_ref, gate_ref, wout_ref, g_ref, b_ref, o_ref, h_buf):
    rows = x_ref.shape[0]
    h_buf[...] = (a_ref[...] * _silu(gate_ref[...])).astype(BF16)

    def set_rows(rs, cs, v):
        o_ref[rs, cs] = v

    for n in range(N_CHUNKS):
        cols = slice(n * COL_CHUNK, (n + 1) * COL_CHUNK)
        y = jnp.dot(h_buf[...], wout_ref[:, cols].astype(BF16), preferred_element_type=F32)
        o_ref[:, cols] = ALPHA * x_ref[:, cols] + y
    _layer_norm(lambda rs, cs: o_ref[rs, cs], set_rows, g_ref, b_ref, rows)


def _gated_out_sample(x, a, gate, wout, g, b):
    R, D = x.shape
    return pl.pallas_call(
        _gated_out_sample_kernel,
        grid=(1,),
        in_specs=[_resident(t.shape) for t in (x, a, gate)] + [wout[1], g[1], b[1]],
        out_specs=pl.BlockSpec((R, D), lambda i: (0, 0)),
        out_shape=jax.ShapeDtypeStruct((R, D), F32),
        scratch_shapes=[pltpu.VMEM((R, D), BF16)],
        compiler_params=_params(1),
        name="gated_out_sample",
    )(x, a, gate, wout[0], g[0], b[0])


def _rope_tables(pos, scale=1.0):
    half = ROT_DIM // 2
    inv_freq = (ROPE_THETA ** (-np.arange(0, ROT_DIM, 2, dtype=np.float32) / ROT_DIM)).astype(np.float32)
    ang = pos.astype(np.float32)[:, None] * inv_freq[None, :]
    cos, sin = np.cos(ang), np.sin(ang)
    n = pos.shape[0]
    rest = np.zeros((n, HEAD_DIM - ROT_DIM), np.float32)
    zero = np.zeros((n, half), np.float32)
    cos_h = np.concatenate([cos, cos, rest + 1.0], axis=1)
    sa_h = np.concatenate([-sin, zero, rest], axis=1)
    sb_h = np.concatenate([zero, sin, rest], axis=1)
    rep = LANES // HEAD_DIM
    return tuple(jnp.asarray(np.tile(a, (1, rep)) * np.float32(scale)) for a in (cos_h, sa_h, sb_h))


def kernel(x_prompt, x_sample, state_pool, cache_k, cache_v, w_in_a, w_grp_a, scale_a, w_out_a,
           w_kv, w_in_b, sinks_b, w_out_b, ln_g, ln_b):
    B, S, D = x_prompt.shape
    R = x_sample.shape[0]
    xp = x_prompt
    xs = x_sample.reshape(R, D)
    pos_p, pos_s = np.arange(S), np.full((R,), PAST_LEN)
    tab_p, tab_s = _rope_tables(pos_p), _rope_tables(pos_s)
    qtab_p, qtab_s = _rope_tables(pos_p, SM_SCALE), _rope_tables(pos_s, SM_SCALE)
    w_grp_a = w_grp_a.astype(BF16)
    scale_a, ln_g, ln_b = (p.reshape(p.shape[0], 1, D) for p in (scale_a, ln_g, ln_b))
    state_rows = state_pool.transpose(0, 2, 1, 3)
    pool_p, pool_s = [], []
    for i in range(N_A_LAYERS):
        params = [_layer(p, i) for p in (w_grp_a, scale_a, ln_g, ln_b)]
        xp, sp, xs, ss = _pool_layer(xp, xs, state_rows, i, w_in_a, w_out_a, params)
        pool_p.append(sp[:, HALO - POOL_STATE:])
        pool_s.append(ss)
    wkv = w_kv.astype(BF16)
    new_k_p, new_v_p, kdup_p, vt_p = _kv_prompt(xp, tab_p, wkv)
    k_s, v_s = _kv_sample(xs, tab_s, wkv)
    bias = _band_bias()
    ck = cache_k.reshape(R, WINDOW, KV_DIM)
    cv = cache_v.reshape(R, WINDOW, KV_DIM)
    for j in range(DEPTH - N_A_LAYERS):
        i = N_A_LAYERS + j
        wout, g, b = _layer(w_out_b, j), _layer(ln_g, i), _layer(ln_b, i)
        xp, q_s, gate_s = _attn_layer(xp, qtab_p, bias, kdup_p, vt_p, xs, qtab_s, j, w_in_b, sinks_b[j], w_out_b, g, b)
        a_s, nk, nv = _attn_sample(q_s, k_s, v_s, ck, cv, sinks_b[j][:, None])
        xs = _gated_out_sample(xs, a_s, gate_s, wout, g, b)
    kv4 = (N_KV_HEADS, HEAD_DIM)
    return (xp, xs.reshape(R, 1, D), jnp.stack(pool_p, axis=0), jnp.stack(pool_s, axis=0).transpose(0, 2, 1, 3),
            new_k_p.reshape(B, WINDOW, *kv4), new_v_p.reshape(B, WINDOW, *kv4),
            nk.reshape(R, WINDOW, *kv4), nv.reshape(R, WINDOW, *kv4))
```

```python
import functools

import jax
import jax.numpy as jnp
import numpy as np
from jax import lax
from jax.experimental import pallas as pl
from jax.experimental.pallas import tpu as pltpu

F32 = jnp.float32
BF16 = jnp.bfloat16

D_MODEL = 2048
DEPTH = 4
PAST_LEN = 16384
N_A_LAYERS = DEPTH // 2
POOL_WINDOWS = (2, 4, 8, 16)
POOL_GROUP = D_MODEL // len(POOL_WINDOWS)
POOL_STATE = max(POOL_WINDOWS) - 1
HEAD_DIM = 64
N_HEADS = D_MODEL // HEAD_DIM
N_KV_HEADS = N_HEADS // 8
GROUP = N_HEADS // N_KV_HEADS
KV_DIM = N_KV_HEADS * HEAD_DIM
WINDOW = 128
ROT_DIM = HEAD_DIM // 4
ROPE_THETA = 500000.0
ALPHA = (2 * DEPTH) ** 0.25
LN_EPS = 1e-5
NEG = -1e30
SM_SCALE = HEAD_DIM ** -0.5

LANES = 128
SUBLANES = 8
HALO = 16
COL_CHUNK = 512
N_CHUNKS = D_MODEL // COL_CHUNK
TM = 256
TK = 1024
LN_ROWS = 16
SAMPLE_ATTN_BATCH = 8
SUM_ROWS = 16
CAST_ROWS = 128
VMEM_LIMIT_BYTES = 56 * 1024 * 1024


def _params(n_axes):
    return pltpu.CompilerParams(dimension_semantics=("arbitrary",) * n_axes,
                                vmem_limit_bytes=VMEM_LIMIT_BYTES)


def _resident(shape):
    zeros = (0,) * len(shape)
    return pl.BlockSpec(shape, lambda *_: zeros, pipeline_mode=pl.Buffered(1))


def _layer(stacked, i):
    zeros = (0,) * (stacked.ndim - 1)
    return stacked, pl.BlockSpec((None,) + stacked.shape[1:], lambda *_: (i,) + zeros,
                                 pipeline_mode=pl.Buffered(1))


def _silu(g):
    return g / (1.0 + jnp.exp(-g))


def _rope(x, cos, sa, sb):
    return x * cos + pltpu.roll(x, LANES - ROT_DIM // 2, 1) * sa + pltpu.roll(x, ROT_DIM // 2, 1) * sb


def _load_as_bf16(w_hbm, dst_ref, stage, sems):
    n_cols = dst_ref.shape[1]
    chunk = stage.shape[1]
    n_chunks = dst_ref.shape[0] // chunk

    def copy(c):
        return pltpu.make_async_copy(w_hbm.at[pl.ds(c * chunk, chunk)],
                                     stage.at[c % 2, :, pl.ds(0, n_cols)], sems.at[c % 2])

    copy(0).start()
    for c in range(n_chunks):
        if c + 1 < n_chunks:
            copy(c + 1).start()
        copy(c).wait()
        dst_ref[c * chunk:(c + 1) * chunk, :] = stage[c % 2, :, 0:n_cols].astype(BF16)


def _outproj_residual(h_ref, x_rows, wout_ref, r_rows_set, rows):
    for n in range(N_CHUNKS):
        cols = slice(n * COL_CHUNK, (n + 1) * COL_CHUNK)
        y = jnp.dot(h_ref[0:rows, :], wout_ref[:, cols], preferred_element_type=F32)
        r_rows_set(slice(0, rows), cols, ALPHA * x_rows(slice(0, rows), cols) + y)


def _zero_after(v):
    bits = pltpu.bitcast(v, jnp.int32)
    half = jnp.full(bits.shape, 16, jnp.int32)
    return lax.shift_right_logical(lax.shift_right_logical(bits, half), half).astype(F32)


def _layer_norm(r_rows_get, o_rows_set, g_ref, b_ref, rows):
    step = min(LN_ROWS, rows)
    anchors = []
    for r0 in range(0, rows, step):
        rs = slice(r0, r0 + step)
        r = r_rows_get(rs, slice(None))
        mu = jnp.mean(r, axis=-1, keepdims=True)
        c = r - mu
        var = jnp.mean(c * c, axis=-1, keepdims=True)
        out = c * lax.rsqrt(var + LN_EPS) * g_ref[...] + b_ref[...]
        o_rows_set(rs, slice(None), out)
        folded = sum(out[i:i + SUBLANES, j:j + LANES]
                     for i in range(0, step, SUBLANES) for j in range(0, out.shape[1], LANES))
        anchors.append(_zero_after(folded))
    return anchors


def _outproj_ln(h_ref, x_rows, wout_ref, g_ref, b_ref, o_rows_set, o_rows_get, rows):
    _outproj_residual(h_ref, x_rows, wout_ref, o_rows_set, rows)
    _layer_norm(o_rows_get, o_rows_set, g_ref, b_ref, rows)


def _pool_layer_kernel(layer, tiles_per_batch, x_ref, xs_ref, *refs):
    st_refs, refs = refs[:POOL_STATE], refs[POOL_STATE:]
    (win_hbm, wgrp_ref, scale_ref, wout_hbm, g_ref, b_ref, o_ref, state_ref, ys_ref, nst_hbm,
     xb_buf, u_buf, h_buf, r_buf, us_buf, win_ref, wout_ref, stage, sems, wsems) = refs
    step = pl.program_id(0)
    n_prompt = pl.num_programs(0) - 1
    rows_s = xs_ref.shape[0]

    def mix(xb, d, g):
        cols = slice(g * POOL_GROUP, (g + 1) * POOL_GROUP)
        gate = jnp.dot(xb, win_ref[:, D_MODEL + g * POOL_GROUP:D_MODEL + (g + 1) * POOL_GROUP],
                       preferred_element_type=F32)
        d = jnp.dot(d.astype(BF16), wgrp_ref[g], preferred_element_type=F32) * scale_ref[:, cols]
        return (d * _silu(gate)).astype(BF16)

    def set_out(rs, cs, v):
        o_ref[0, rs, cs] = v

    def set_r(rs, cs, v):
        r_buf[rs, cs] = v

    def norm_previous_tile():
        return _layer_norm(lambda rs, cs: r_buf[rs, cs], set_out, g_ref, b_ref, TM)

    @pl.when(step == 0)
    def _():
        r_buf[...] = jnp.zeros(r_buf.shape, F32)
        _load_as_bf16(win_hbm.at[layer], win_ref, stage, wsems)
        _load_as_bf16(wout_hbm.at[layer], wout_ref, stage, wsems)

    @pl.when(step < n_prompt)
    def _():
        t = step % tiles_per_batch

        @pl.when(t == 0)
        def _():
            u_buf[0:HALO, :] = jnp.zeros((HALO, D_MODEL), F32)

        xb_buf[...] = x_ref[0].astype(BF16)
        row = lax.broadcasted_iota(jnp.int32, (TM, 1), 0) + t * TM

        def project_u(g):
            cols = slice(g * POOL_GROUP, (g + 1) * POOL_GROUP)
            u_buf[HALO:, cols] = jnp.dot(xb_buf[...], win_ref[:, cols], preferred_element_type=F32)

        project_u(0)
        anchors = norm_previous_tile()
        per_group = len(anchors) // len(POOL_WINDOWS)
        for g, w in enumerate(POOL_WINDOWS):
            cols = slice(g * POOL_GROUP, (g + 1) * POOL_GROUP)
            if g + 1 < len(POOL_WINDOWS):
                project_u(g + 1)
            ext = u_buf[:, cols]
            s = ext
            shift = 1
            while shift < w:
                s = s + pltpu.roll(s, shift, 0)
                shift *= 2
            inv_cnt = 1.0 / jnp.minimum(w, row + 1).astype(F32)
            anchor = sum(anchors[g * per_group:(g + 1) * per_group])
            anchor = jnp.tile(anchor, (TM // anchor.shape[0], POOL_GROUP // anchor.shape[1]))
            h_buf[:, cols] = mix(xb_buf[...], s[HALO:, :] * inv_cnt - ext[HALO:, :] + anchor, g)

        state_ref[0] = u_buf[TM:TM + HALO, :]
        u_buf[0:HALO, :] = u_buf[TM:TM + HALO, :]
        _outproj_residual(h_buf, lambda rs, cs: x_ref[0, rs, cs], wout_ref, set_r, TM)

    @pl.when(step == n_prompt)
    def _():
        def new_state_row(j):
            return nst_hbm.at[pl.ds(0, rows_s), pl.ds(j * D_MODEL, D_MODEL)]

        shifts = [pltpu.make_async_copy(st_refs[j + 1].at[0], new_state_row(j), sems.at[j])
                  for j in range(POOL_STATE - 1)]
        append_new = pltpu.make_async_copy(us_buf, new_state_row(POOL_STATE - 1), sems.at[POOL_STATE - 1])
        for copy in shifts:
            copy.start()
        norm_previous_tile()
        xb = xs_ref[...].astype(BF16)
        for g, w in enumerate(POOL_WINDOWS):
            cols = slice(g * POOL_GROUP, (g + 1) * POOL_GROUP)
            u = jnp.dot(xb, win_ref[:, cols], preferred_element_type=F32)
            us_buf[:, cols] = u
            acc = u
            for j in range(1, w):
                acc = acc + st_refs[POOL_STATE - j][0, :, cols]
            h_buf[0:rows_s, cols] = mix(xb, acc * (1.0 / min(w, PAST_LEN + 1)) - u, g)
        append_new.start()

        def set_ys(rs, cs, v):
            ys_ref[rs, cs] = v

        _outproj_ln(h_buf, lambda rs, cs: xs_ref[rs, cs], wout_ref, g_ref, b_ref,
                    set_ys, lambda rs, cs: ys_ref[rs, cs], rows_s)
        for copy in shifts:
            copy.wait()
        append_new.wait()


def _pool_layer(x, xs, state_cols, layer, w_in, w_out, params):
    B, S, D = x.shape
    R = xs.shape[0]
    tiles_per_batch = S // TM
    n_prompt = B * tiles_per_batch

    def tile(step):
        step = jnp.clip(step, 0, n_prompt - 1)
        return step // tiles_per_batch, step % tiles_per_batch

    hbm = pl.BlockSpec(memory_space=pl.ANY)
    (wgrp, wgrp_spec), (scale, scale_spec), (g, g_spec), (b, b_spec) = params
    state_specs = [pl.BlockSpec((1, R, D), lambda i, j=j: (layer, 0, j), pipeline_mode=pl.Buffered(1))
                   for j in range(POOL_STATE)]
    return pl.pallas_call(
        functools.partial(_pool_layer_kernel, layer, tiles_per_batch),
        grid=(n_prompt + 1,),
        in_specs=[pl.BlockSpec((1, TM, D), lambda i: (*tile(i), 0)), _resident(xs.shape), *state_specs,
                  hbm, wgrp_spec, scale_spec, hbm, g_spec, b_spec],
        out_specs=[
            pl.BlockSpec((1, TM, D), lambda i: (*tile(i - 1), 0)),
            pl.BlockSpec((1, HALO, D), lambda i: (tile(i)[0], 0, 0)),
            pl.BlockSpec((R, D), lambda i: (0, 0)),
            pl.BlockSpec(memory_space=pl.ANY),
        ],
        out_shape=[jax.ShapeDtypeStruct((B, S, D), F32), jax.ShapeDtypeStruct((B, HALO, D), F32),
                   jax.ShapeDtypeStruct((R, D), F32), jax.ShapeDtypeStruct((R, POOL_STATE * D), F32)],
        scratch_shapes=[pltpu.VMEM((TM, D), BF16), pltpu.VMEM((HALO + TM, D), F32), pltpu.VMEM((TM, D), BF16),
                        pltpu.VMEM((TM, D), F32), pltpu.VMEM((R, D), F32),
                        pltpu.VMEM(w_in.shape[1:], BF16), pltpu.VMEM(w_out.shape[1:], BF16),
                        pltpu.VMEM((2, CAST_ROWS, w_in.shape[2]), F32),
                        pltpu.SemaphoreType.DMA((POOL_STATE,)), pltpu.SemaphoreType.DMA((2,))],
        compiler_params=_params(1),
        name="pool_layer",
    )(x, xs, *[state_cols] * POOL_STATE, w_in, wgrp, scale, w_out, g, b)


def _project_kv(xb, cos, sa, sb, wkv_ref):
    kv = jnp.dot(xb, wkv_ref[...], preferred_element_type=F32)
    k_slabs = [_rope(kv[:, j * LANES:(j + 1) * LANES], cos, sa, sb) for j in range(KV_DIM // LANES)]
    return k_slabs, kv[:, KV_DIM:]


def _kv_prompt_kernel(x_ref, cos_ref, sa_ref, sb_ref, wkv_ref, knew_ref, vnew_ref, kdup_ref, vt_ref):
    k_slabs, v = _project_kv(x_ref[0].astype(BF16), cos_ref[...], sa_ref[...], sb_ref[...], wkv_ref)
    low = lax.broadcasted_iota(jnp.int32, (TK, LANES), 1) < HEAD_DIM
    for j, k in enumerate(k_slabs):
        swapped = pltpu.roll(k, HEAD_DIM, 1)
        kdup_ref[0, :, (2 * j) * LANES:(2 * j + 1) * LANES] = jnp.where(low, k, swapped).astype(BF16)
        kdup_ref[0, :, (2 * j + 1) * LANES:(2 * j + 2) * LANES] = jnp.where(low, swapped, k).astype(BF16)
    for i in range(TK // WINDOW):
        vt_ref[0, i] = v[i * WINDOW:(i + 1) * WINDOW, :].T.astype(BF16)

    @pl.when(pl.program_id(1) == pl.num_programs(1) - 1)
    def _():
        for j, k in enumerate(k_slabs):
            knew_ref[0, :, j * LANES:(j + 1) * LANES] = k[TK - WINDOW:, :]
        vnew_ref[0] = v[TK - WINDOW:, :]


def _kv_prompt(x, tables, wkv):
    B, S, D = x.shape
    tab = pl.BlockSpec((TK, LANES), lambda bi, t: (t, 0))
    last = pl.BlockSpec((1, WINDOW, KV_DIM), lambda bi, t: (bi, 0, 0))
    return pl.pallas_call(
        _kv_prompt_kernel,
        grid=(B, S // TK),
        in_specs=[pl.BlockSpec((1, TK, D), lambda bi, t: (bi, t, 0)), tab, tab, tab, _resident(wkv.shape)],
        out_specs=[last, last,
                   pl.BlockSpec((1, TK, N_KV_HEADS * LANES), lambda bi, t: (bi, t, 0)),
                   pl.BlockSpec((1, TK // WINDOW, KV_DIM, WINDOW), lambda bi, t: (bi, t, 0, 0))],
        out_shape=[jax.ShapeDtypeStruct((B, WINDOW, KV_DIM), F32), jax.ShapeDtypeStruct((B, WINDOW, KV_DIM), F32),
                   jax.ShapeDtypeStruct((B, S, N_KV_HEADS * LANES), BF16),
                   jax.ShapeDtypeStruct((B, S // WINDOW, KV_DIM, WINDOW), BF16)],
        compiler_params=_params(2),
        name="kv_prompt",
    )(x, *tables, wkv)


def _kv_sample_kernel(x_ref, cos_ref, sa_ref, sb_ref, wkv_ref, k_ref, v_ref):
    k_slabs, v = _project_kv(x_ref[...].astype(BF16), cos_ref[...], sa_ref[...], sb_ref[...], wkv_ref)
    for j, k in enumerate(k_slabs):
        k_ref[:, j * LANES:(j + 1) * LANES] = k
    v_ref[...] = v


def _kv_sample(x, tables, wkv):
    R, D = x.shape
    out = pl.BlockSpec((R, KV_DIM), lambda i: (0, 0))
    return pl.pallas_call(
        _kv_sample_kernel,
        grid=(1,),
        in_specs=[_resident(x.shape)] + [_resident(t.shape) for t in tables] + [_resident(wkv.shape)],
        out_specs=[out, out],
        out_shape=[jax.ShapeDtypeStruct((R, KV_DIM), F32), jax.ShapeDtypeStruct((R, KV_DIM), F32)],
        compiler_params=_params(1),
        name="kv_sample",
    )(x, *tables, wkv)


def _attn_layer_kernel(layer, tiles_per_batch,
                       x_ref, cos_ref, sa_ref, sb_ref, bias_ref, kdup_ref, vt_ref, xs_ref, cos_s_ref, sa_s_ref, sb_s_ref,
                       win_hbm, sink_ref, wout_hbm, g_ref, b_ref,
                       o_ref, qs_ref, gs_ref, xb_buf, q_buf, h_buf, g_buf, r_buf, win_ref, wout_ref, stage, wsems):
    step = pl.program_id(0)
    n_prompt = pl.num_programs(0) - 1
    group_cols = GROUP * HEAD_DIM

    def set_out(rs, cs, v):
        o_ref[0, rs, cs] = v

    def set_r(rs, cs, v):
        r_buf[rs, cs] = v

    def norm_previous_tile():
        return _layer_norm(lambda rs, cs: r_buf[rs, cs], set_out, g_ref, b_ref, TM)

    @pl.when(step == 0)
    def _():
        r_buf[...] = jnp.zeros(r_buf.shape, F32)
        _load_as_bf16(win_hbm.at[layer], win_ref, stage, wsems)
        _load_as_bf16(wout_hbm.at[layer], wout_ref, stage, wsems)

    @pl.when(step < n_prompt)
    def _():
        t = step % tiles_per_batch
        xb_buf[...] = x_ref[0].astype(BF16)

        def project_q(kv, anchor):
            q = jnp.dot(xb_buf[...], win_ref[:, kv * group_cols:(kv + 1) * group_cols], preferred_element_type=F32)
            if anchor is not None:
                q = q + anchor
            for j in range(group_cols // LANES):
                qj = _rope(q[:, j * LANES:(j + 1) * LANES], cos_ref[...], sa_ref[...], sb_ref[...])
                c0 = kv * group_cols + j * LANES
                q_buf[:, c0:c0 + LANES] = qj.astype(BF16)

        def project_gate(kv, anchor):
            cols = slice(kv * group_cols, (kv + 1) * group_cols)
            gate = jnp.dot(xb_buf[...], win_ref[:, D_MODEL + kv * group_cols:D_MODEL + (kv + 1) * group_cols],
                           preferred_element_type=F32)
            g_buf[:, cols] = _silu(gate) if anchor is None else _silu(gate) + anchor

        low_half = lax.broadcasted_iota(jnp.int32, (WINDOW, LANES), 1) < HEAD_DIM
        ones_rows = jnp.ones((SUM_ROWS, 2 * WINDOW), BF16)

        def block_ids(qb):
            blk = t * (TM // WINDOW) + qb
            return blk, jnp.maximum(blk - 1, 0)

        def group_heads(kv):
            return [(kv * (GROUP // 2) + pair, par) for pair in range(GROUP // 2) for par in range(2)]

        def scores(qb, kv):
            blk, prev_blk = block_ids(qb)
            prev = pl.multiple_of(prev_blk * WINDOW, WINDOW)
            cur = pl.multiple_of(blk * WINDOW, WINDOW)
            rows = slice(qb * WINDOW, (qb + 1) * WINDOW)
            ks = slice(kv * LANES, (kv + 1) * LANES)
            k2 = jnp.concatenate([kdup_ref[0, pl.ds(prev, WINDOW), ks], kdup_ref[0, pl.ds(cur, WINDOW), ks]], axis=0)
            q_all = []
            for slab, par in group_heads(kv):
                q_slab = q_buf[rows, slab * LANES:(slab + 1) * LANES]
                q_all.append(jnp.where(low_half == (par == 0), q_slab, jnp.zeros_like(q_slab)))
            return lax.dot_general(k2, jnp.concatenate(q_all, axis=0), (((1,), (1,)), ((), ())),
                                   preferred_element_type=F32)

        def finish(qb, kv, s_t):
            blk, prev_blk = block_ids(qb)
            bias = bias_ref[jnp.minimum(blk, 1)]
            rows = slice(qb * WINDOW, (qb + 1) * WINDOW)
            vs = slice(kv * HEAD_DIM, (kv + 1) * HEAD_DIM)
            v_aug = jnp.concatenate([vt_ref[0, prev_blk, vs, :], vt_ref[0, blk, vs, :]], axis=1)
            v_aug = jnp.concatenate([v_aug, ones_rows], axis=0)
            for pair in range(GROUP // 2):
                slab = kv * (GROUP // 2) + pair
                p_t, sink_terms = [], []
                for par in range(2):
                    i = 2 * pair + par
                    s = s_t[:, i * WINDOW:(i + 1) * WINDOW] + bias
                    sink = sink_ref[2 * slab + par]
                    m = jnp.maximum(jnp.max(s, axis=0, keepdims=True), sink)
                    p_t.append(jnp.exp(s - m).astype(BF16))
                    sink_terms.append(jnp.exp(sink - m))
                o_t = jnp.dot(v_aug, jnp.concatenate(p_t, axis=1), preferred_element_type=F32)
                both = []
                for par in range(2):
                    cs = slice(par * WINDOW, (par + 1) * WINDOW)
                    inv = 1.0 / (o_t[HEAD_DIM:HEAD_DIM + 1, cs] + sink_terms[par])
                    both.append(o_t[:HEAD_DIM, cs] * inv)
                attn = jnp.concatenate(both, axis=0).T
                cs = slice(slab * LANES, (slab + 1) * LANES)
                h_buf[rows, cs] = (attn * g_buf[rows, cs]).astype(BF16)

        project_q(0, None)
        anchors = norm_previous_tile()
        early = 2 * (N_KV_HEADS // 2)
        per_proj = len(anchors) // early

        def next_anchor():
            if not anchors:
                return None
            anchor = sum(anchors.pop(0) for _ in range(per_proj))
            return jnp.tile(anchor, (TM // anchor.shape[0], group_cols // anchor.shape[1]))

        for kv in range(N_KV_HEADS):
            s_t = [scores(qb, kv) for qb in range(TM // WINDOW)]
            if kv + 1 < N_KV_HEADS:
                project_q(kv + 1, next_anchor())
            project_gate(kv, next_anchor())
            for qb in range(TM // WINDOW):
                finish(qb, kv, s_t[qb])
        _outproj_residual(h_buf, lambda rs, cs: x_ref[0, rs, cs], wout_ref, set_r, TM)

    @pl.when(step == n_prompt)
    def _():
        norm_previous_tile()
        xb = xs_ref[...].astype(BF16)
        for n in range(N_CHUNKS):
            q = jnp.dot(xb, win_ref[:, n * COL_CHUNK:(n + 1) * COL_CHUNK], preferred_element_type=F32)
            for j in range(COL_CHUNK // LANES):
                c0 = n * COL_CHUNK + j * LANES
                qs_ref[:, c0:c0 + LANES] = _rope(q[:, j * LANES:(j + 1) * LANES],
                                                 cos_s_ref[...], sa_s_ref[...], sb_s_ref[...])
            gs_ref[:, n * COL_CHUNK:(n + 1) * COL_CHUNK] = jnp.dot(
                xb, win_ref[:, D_MODEL + n * COL_CHUNK:D_MODEL + (n + 1) * COL_CHUNK],
                preferred_element_type=F32)


def _band_bias():
    key = np.arange(2 * WINDOW)[:, None]
    qry = np.arange(WINDOW)[None, :]
    band = (key > qry) & (key <= qry + WINDOW)
    return jnp.asarray(np.where(np.stack([band & (key >= WINDOW), band]), 0.0, NEG).astype(np.float32))


def _attn_layer(x, tables, bias, kdup, vt, xs, tables_s, layer, w_in, sinks, w_out, g, b):
    B, S, D = x.shape
    R = xs.shape[0]
    tiles_per_batch = S // TM
    n_prompt = B * tiles_per_batch

    def tile(step):
        step = jnp.clip(step, 0, n_prompt - 1)
        return step // tiles_per_batch, step % tiles_per_batch

    tab = pl.BlockSpec((TM, LANES), lambda i: (tile(i)[1], 0))

    def per_batch(a):
        return pl.BlockSpec((1,) + a.shape[1:], lambda i: (tile(i)[0],) + (0,) * (a.ndim - 1),
                            pipeline_mode=pl.Buffered(1))

    rows = pl.BlockSpec((R, D), lambda i: (0, 0))
    return pl.pallas_call(
        functools.partial(_attn_layer_kernel, layer, tiles_per_batch),
        grid=(n_prompt + 1,),
        in_specs=[
            pl.BlockSpec((1, TM, D), lambda i: (*tile(i), 0)), tab, tab, tab, _resident(bias.shape),
            per_batch(kdup), per_batch(vt),
            _resident(xs.shape)] + [_resident(t.shape) for t in tables_s] + [
            pl.BlockSpec(memory_space=pl.ANY), pl.BlockSpec(memory_space=pltpu.SMEM),
            pl.BlockSpec(memory_space=pl.ANY), g[1], b[1],
        ],
        out_specs=[pl.BlockSpec((1, TM, D), lambda i: (*tile(i - 1), 0)), rows, rows],
        out_shape=[jax.ShapeDtypeStruct((B, S, D), F32),
                   jax.ShapeDtypeStruct((R, D), F32), jax.ShapeDtypeStruct((R, D), F32)],
        scratch_shapes=[pltpu.VMEM((TM, D), BF16), pltpu.VMEM((TM, D), BF16), pltpu.VMEM((TM, D), BF16),
                        pltpu.VMEM((TM, D), F32), pltpu.VMEM((TM, D), F32),
                        pltpu.VMEM(w_in.shape[1:], BF16), pltpu.VMEM(w_out.shape[1:], BF16),
                        pltpu.VMEM((2, CAST_ROWS, w_in.shape[2]), F32), pltpu.SemaphoreType.DMA((2,))],
        compiler_params=_params(1),
        name="attn_layer",
    )(x, *tables, bias, kdup, vt, xs, *tables_s, w_in, sinks, w_out, g[0], b[0])


def _attn_sample_kernel(q_ref, kn_ref, vn_ref, ck_ref, cv_ref, sink_ref, o_ref, nk_ref, nv_ref):
    nb = q_ref.shape[0]
    head_of_lane = lax.broadcasted_iota(jnp.int32, (N_HEADS, D_MODEL), 1) // HEAD_DIM
    own_head = head_of_lane == lax.broadcasted_iota(jnp.int32, (N_HEADS, D_MODEL), 0)
    low_half = lax.broadcasted_iota(jnp.int32, (N_HEADS, LANES), 1) < HEAD_DIM
    last_row = lax.broadcasted_iota(jnp.int32, (WINDOW, KV_DIM), 0) == WINDOW - 1
    sink = sink_ref[...]
    heads_per_slab = LANES // HEAD_DIM
    slabs_per_group = GROUP // heads_per_slab

    def body(i, carry):
        newk = jnp.where(last_row, kn_ref[pl.ds(i, 1), :], pltpu.roll(ck_ref[i], WINDOW - 1, 0))
        newv = jnp.where(last_row, vn_ref[pl.ds(i, 1), :], pltpu.roll(cv_ref[i], WINDOW - 1, 0))
        nk_ref[i] = newk
        nv_ref[i] = newv
        qh = jnp.where(own_head, jnp.broadcast_to(q_ref[pl.ds(i, 1), :], (N_HEADS, D_MODEL)), 0.0)
        folded = []
        for kv in range(N_KV_HEADS):
            w = qh[:, kv * GROUP * HEAD_DIM:kv * GROUP * HEAD_DIM + LANES]
            for sl in range(1, slabs_per_group):
                c0 = kv * GROUP * HEAD_DIM + sl * LANES
                w = w + qh[:, c0:c0 + LANES]
            folded.append(w + pltpu.roll(w, HEAD_DIM, 1))
        qg = jnp.concatenate([jnp.where(low_half, folded[2 * j], folded[2 * j + 1])
                              for j in range(N_KV_HEADS // 2)], axis=1)
        s = lax.dot_general(qg.astype(BF16), newk.astype(BF16), (((1,), (1,)), ((), ())),
                            preferred_element_type=F32)
        m = jnp.maximum(jnp.max(s, axis=-1, keepdims=True), sink)
        p = jnp.exp(s - m)
        denom = jnp.sum(p, axis=-1, keepdims=True) + jnp.exp(sink - m)
        og = jnp.dot(p.astype(BF16), newv.astype(BF16), preferred_element_type=F32) / denom
        slabs = []
        for kv in range(N_KV_HEADS):
            xs = og[:, (kv // 2) * LANES:(kv // 2 + 1) * LANES]
            rolled = pltpu.roll(xs, HEAD_DIM, 1)
            both = jnp.where(low_half, xs, rolled) if kv % 2 == 0 else jnp.where(low_half, rolled, xs)
            slabs.extend([both] * slabs_per_group)
        full = jnp.concatenate(slabs, axis=1)
        o_ref[pl.ds(i, 1), :] = jnp.sum(jnp.where(own_head, full, 0.0), axis=0, keepdims=True)
        return carry

    lax.fori_loop(0, nb, body, 0, unroll=True)


def _attn_sample(q, kn, vn, ck, cv, sinks_col):
    R, D = q.shape
    nb = SAMPLE_ATTN_BATCH
    row2 = lambda w: pl.BlockSpec((nb, w), lambda i: (i, 0))
    cache = pl.BlockSpec((nb, WINDOW, KV_DIM), lambda i: (i, 0, 0))
    return pl.pallas_call(
        _attn_sample_kernel,
        grid=(R // nb,),
        in_specs=[row2(D), row2(KV_DIM), row2(KV_DIM), cache, cache, _resident(sinks_col.shape)],
        out_specs=[row2(D), cache, cache],
        out_shape=[jax.ShapeDtypeStruct((R, D), F32),
                   jax.ShapeDtypeStruct(ck.shape, F32), jax.ShapeDtypeStruct(cv.shape, F32)],
        compiler_params=_params(1),
        name="attn_sample",
    )(q, kn, vn, ck, cv, sinks_col)


def _gated_out_sample_kernel(x_ref, a_ref, gate_ref, wout_ref, g_ref, b_ref, o_ref, h_buf):
    rows = x_ref.shape[0]
    h_buf[...] = (a_ref[...] * _silu(gate_ref[...])).astype(BF16)

    def set_rows(rs, cs, v):
        o_ref[rs, cs] = v

    for n in range(N_CHUNKS):
        cols = slice(n * COL_CHUNK, (n + 1) * COL_CHUNK)
        y = jnp.dot(h_buf[...], wout_ref[:, cols].astype(BF16), preferred_element_type=F32)
        o_ref[:, cols] = ALPHA * x_ref[:, cols] + y
    _layer_norm(lambda rs, cs: o_ref[rs, cs], set_rows, g_ref, b_ref, rows)


def _gated_out_sample(x, a, gate, wout, g, b):
    R, D = x.shape
    return pl.pallas_call(
        _gated_out_sample_kernel,
        grid=(1,),
        in_specs=[_resident(t.shape) for t in (x, a, gate)] + [wout[1], g[1], b[1]],
        out_specs=pl.BlockSpec((R, D), lambda i: (0, 0)),
        out_shape=jax.ShapeDtypeStruct((R, D), F32),
        scratch_shapes=[pltpu.VMEM((R, D), BF16)],
        compiler_params=_params(1),
        name="gated_out_sample",
    )(x, a, gate, wout[0], g[0], b[0])


def _rope_tables(pos, scale=1.0):
    half = ROT_DIM // 2
    inv_freq = (ROPE_THETA ** (-np.arange(0, ROT_DIM, 2, dtype=np.float32) / ROT_DIM)).astype(np.float32)
    ang = pos.astype(np.float32)[:, None] * inv_freq[None, :]
    cos, sin = np.cos(ang), np.sin(ang)
    n = pos.shape[0]
    rest = np.zeros((n, HEAD_DIM - ROT_DIM), np.float32)
    zero = np.zeros((n, half), np.float32)
    cos_h = np.concatenate([cos, cos, rest + 1.0], axis=1)
    sa_h = np.concatenate([-sin, zero, rest], axis=1)
    sb_h = np.concatenate([zero, sin, rest], axis=1)
    rep = LANES // HEAD_DIM
    return tuple(jnp.asarray(np.tile(a, (1, rep)) * np.float32(scale)) for a in (cos_h, sa_h, sb_h))


def kernel(x_prompt, x_sample, state_pool, cache_k, cache_v, w_in_a, w_grp_a, scale_a, w_out_a,
           w_kv, w_in_b, sinks_b, w_out_b, ln_g, ln_b):
    B, S, D = x_prompt.shape
    R = x_sample.shape[0]
    xp = x_prompt
    xs = x_sample.reshape(R, D)
    pos_p, pos_s = np.arange(S), np.full((R,), PAST_LEN)
    tab_p, tab_s = _rope_tables(pos_p), _rope_tables(pos_s)
    qtab_p, qtab_s = _rope_tables(pos_p, SM_SCALE), _rope_tables(pos_s, SM_SCALE)
    w_grp_a = w_grp_a.astype(BF16)
    scale_a, ln_g, ln_b = (p.reshape(p.shape[0], 1, D) for p in (scale_a, ln_g, ln_b))
    state_cols = state_pool.reshape(N_A_LAYERS, R, POOL_STATE * D)
    pool_p, pool_s = [], []
    for i in range(N_A_LAYERS):
        params = [_layer(p, i) for p in (w_grp_a, scale_a, ln_g, ln_b)]
        xp, sp, xs, ss = _pool_layer(xp, xs, state_cols, i, w_in_a, w_out_a, params)
        pool_p.append(sp[:, HALO - POOL_STATE:])
        pool_s.append(ss)
    wkv = w_kv.astype(BF16)
    new_k_p, new_v_p, kdup_p, vt_p = _kv_prompt(xp, tab_p, wkv)
    k_s, v_s = _kv_sample(xs, tab_s, wkv)
    bias = _band_bias()
    ck = cache_k.reshape(R, WINDOW, KV_DIM)
    cv = cache_v.reshape(R, WINDOW, KV_DIM)
    for j in range(DEPTH - N_A_LAYERS):
        i = N_A_LAYERS + j
        wout, g, b = _layer(w_out_b, j), _layer(ln_g, i), _layer(ln_b, i)
        xp, q_s, gate_s = _attn_layer(xp, qtab_p, bias, kdup_p, vt_p, xs, qtab_s, j, w_in_b, sinks_b[j], w_out_b, g, b)
        a_s, nk, nv = _attn_sample(q_s, k_s, v_s, ck, cv, sinks_b[j][:, None])
        xs = _gated_out_sample(xs, a_s, gate_s, wout, g, b)
    kv4 = (N_KV_HEADS, HEAD_DIM)
    return (xp, xs.reshape(R, 1, D), jnp.stack(pool_p, axis=0), jnp.stack(pool_s, axis=0).reshape(N_A_LAYERS, R, POOL_STATE, D),
            new_k_p.reshape(B, WINDOW, *kv4), new_v_p.reshape(B, WINDOW, *kv4),
            nk.reshape(R, WINDOW, *kv4), nv.reshape(R, WINDOW, *kv4))
```

```python
import functools

import jax
import jax.numpy as jnp
import numpy as np
from jax import lax
from jax.experimental import pallas as pl
from jax.experimental.pallas import tpu as pltpu

F32 = jnp.float32
BF16 = jnp.bfloat16

D_MODEL = 2048
DEPTH = 4
PAST_LEN = 16384
N_A_LAYERS = DEPTH // 2
POOL_WINDOWS = (2, 4, 8, 16)
POOL_GROUP = D_MODEL // len(POOL_WINDOWS)
POOL_STATE = max(POOL_WINDOWS) - 1
HEAD_DIM = 64
N_HEADS = D_MODEL // HEAD_DIM
N_KV_HEADS = N_HEADS // 8
GROUP = N_HEADS // N_KV_HEADS
KV_DIM = N_KV_HEADS * HEAD_DIM
WINDOW = 128
ROT_DIM = HEAD_DIM // 4
ROPE_THETA = 500000.0
ALPHA = (2 * DEPTH) ** 0.25
LN_EPS = 1e-5
NEG = -1e30
SM_SCALE = HEAD_DIM ** -0.5

LANES = 128
SUBLANES = 8
HALO = 16
COL_CHUNK = 512
N_CHUNKS = D_MODEL // COL_CHUNK
TM = 256
TK = 1024
LN_ROWS = 16
SAMPLE_ATTN_BATCH = 8
SUM_ROWS = 16
CAST_ROWS = 128
VMEM_LIMIT_BYTES = 56 * 1024 * 1024


def _params(n_axes):
    return pltpu.CompilerParams(dimension_semantics=("arbitrary",) * n_axes,
                                vmem_limit_bytes=VMEM_LIMIT_BYTES)


def _resident(shape):
    zeros = (0,) * len(shape)
    return pl.BlockSpec(shape, lambda *_: zeros, pipeline_mode=pl.Buffered(1))


def _layer(stacked, i):
    zeros = (0,) * (stacked.ndim - 1)
    return stacked, pl.BlockSpec((None,) + stacked.shape[1:], lambda *_: (i,) + zeros,
                                 pipeline_mode=pl.Buffered(1))


def _silu(g):
    return g / (1.0 + jnp.exp(-g))


def _rope(x, cos, sa, sb):
    return x * cos + pltpu.roll(x, LANES - ROT_DIM // 2, 1) * sa + pltpu.roll(x, ROT_DIM // 2, 1) * sb


def _load_as_bf16(w_hbm, dst_ref, stage, sems):
    n_cols = dst_ref.shape[1]
    chunk = stage.shape[1]
    n_chunks = dst_ref.shape[0] // chunk

    def copy(c):
        return pltpu.make_async_copy(w_hbm.at[pl.ds(c * chunk, chunk)],
                                     stage.at[c % 2, :, pl.ds(0, n_cols)], sems.at[c % 2])

    copy(0).start()
    for c in range(n_chunks):
        if c + 1 < n_chunks:
            copy(c + 1).start()
        copy(c).wait()
        dst_ref[c * chunk:(c + 1) * chunk, :] = stage[c % 2, :, 0:n_cols].astype(BF16)


def _outproj_residual(h_ref, x_rows, wout_ref, r_rows_set, rows):
    for n in range(N_CHUNKS):
        cols = slice(n * COL_CHUNK, (n + 1) * COL_CHUNK)
        y = jnp.dot(h_ref[0:rows, :], wout_ref[:, cols], preferred_element_type=F32)
        r_rows_set(slice(0, rows), cols, ALPHA * x_rows(slice(0, rows), cols) + y)


def _zero_after(v):
    bits = pltpu.bitcast(v, jnp.int32)
    half = jnp.full(bits.shape, 16, jnp.int32)
    return lax.shift_right_logical(lax.shift_right_logical(bits, half), half).astype(F32)


def _layer_norm(r_rows_get, o_rows_set, g_ref, b_ref, rows):
    step = min(LN_ROWS, rows)
    anchors = []
    for r0 in range(0, rows, step):
        rs = slice(r0, r0 + step)
        r = r_rows_get(rs, slice(None))
        mu = jnp.mean(r, axis=-1, keepdims=True)
        c = r - mu
        var = jnp.mean(c * c, axis=-1, keepdims=True)
        out = c * lax.rsqrt(var + LN_EPS) * g_ref[...] + b_ref[...]
        o_rows_set(rs, slice(None), out)
        folded = sum(out[i:i + SUBLANES, j:j + LANES]
                     for i in range(0, step, SUBLANES) for j in range(0, out.shape[1], LANES))
        anchors.append(_zero_after(folded))
    return anchors


def _outproj_ln(h_ref, x_rows, wout_ref, g_ref, b_ref, o_rows_set, o_rows_get, rows):
    _outproj_residual(h_ref, x_rows, wout_ref, o_rows_set, rows)
    _layer_norm(o_rows_get, o_rows_set, g_ref, b_ref, rows)


def _pool_layer_kernel(layer, tiles_per_batch,
                       x_ref, xs_ref, st_hbm, win_hbm, wgrp_ref, scale_ref, wout_hbm, g_ref, b_ref,
                       o_ref, state_ref, ys_ref, nst_hbm,
                       xb_buf, u_buf, h_buf, r_buf, st_buf, us_buf, win_ref, wout_ref, stage, sems, wsems):
    step = pl.program_id(0)
    n_prompt = pl.num_programs(0) - 1
    rows_s = xs_ref.shape[0]

    def mix(xb, d, g):
        cols = slice(g * POOL_GROUP, (g + 1) * POOL_GROUP)
        gate = jnp.dot(xb, win_ref[:, D_MODEL + g * POOL_GROUP:D_MODEL + (g + 1) * POOL_GROUP],
                       preferred_element_type=F32)
        d = jnp.dot(d.astype(BF16), wgrp_ref[g].astype(BF16), preferred_element_type=F32) * scale_ref[:, cols]
        return (d * _silu(gate)).astype(BF16)

    def set_out(rs, cs, v):
        o_ref[0, rs, cs] = v

    def set_r(rs, cs, v):
        r_buf[rs, cs] = v

    def norm_previous_tile():
        return _layer_norm(lambda rs, cs: r_buf[rs, cs], set_out, g_ref, b_ref, TM)

    @pl.when(step == 0)
    def _():
        r_buf[...] = jnp.zeros(r_buf.shape, F32)
        _load_as_bf16(win_hbm.at[layer], win_ref, stage, wsems)
        _load_as_bf16(wout_hbm.at[layer], wout_ref, stage, wsems)

    @pl.when(step < n_prompt)
    def _():
        t = step % tiles_per_batch

        @pl.when(t == 0)
        def _():
            u_buf[0:HALO, :] = jnp.zeros((HALO, D_MODEL), F32)

        xb_buf[...] = x_ref[0].astype(BF16)
        row = lax.broadcasted_iota(jnp.int32, (TM, 1), 0) + t * TM

        def project_u(g):
            cols = slice(g * POOL_GROUP, (g + 1) * POOL_GROUP)
            u_buf[HALO:, cols] = jnp.dot(xb_buf[...], win_ref[:, cols], preferred_element_type=F32)

        project_u(0)
        anchors = norm_previous_tile()
        per_group = len(anchors) // len(POOL_WINDOWS)
        for g, w in enumerate(POOL_WINDOWS):
            cols = slice(g * POOL_GROUP, (g + 1) * POOL_GROUP)
            if g + 1 < len(POOL_WINDOWS):
                project_u(g + 1)
            ext = u_buf[:, cols]
            s = ext
            shift = 1
            while shift < w:
                s = s + pltpu.roll(s, shift, 0)
                shift *= 2
            inv_cnt = 1.0 / jnp.minimum(w, row + 1).astype(F32)
            anchor = sum(anchors[g * per_group:(g + 1) * per_group])
            anchor = jnp.tile(anchor, (TM // anchor.shape[0], POOL_GROUP // anchor.shape[1]))
            h_buf[:, cols] = mix(xb_buf[...], s[HALO:, :] * inv_cnt - ext[HALO:, :] + anchor, g)

        state_ref[0] = u_buf[TM:TM + HALO, :]
        u_buf[0:HALO, :] = u_buf[TM:TM + HALO, :]
        _outproj_residual(h_buf, lambda rs, cs: x_ref[0, rs, cs], wout_ref, set_r, TM)

    @pl.when(step == n_prompt)
    def _():
        load = pltpu.make_async_copy(st_hbm.at[layer], st_buf, sems.at[0])
        shift_old = pltpu.make_async_copy(st_buf.at[pl.ds(1, POOL_STATE - 1)],
                                          nst_hbm.at[pl.ds(0, POOL_STATE - 1)], sems.at[1])
        append_new = pltpu.make_async_copy(us_buf, nst_hbm.at[POOL_STATE - 1], sems.at[2])
        load.start()
        norm_previous_tile()
        xb = xs_ref[...].astype(BF16)
        load.wait()
        shift_old.start()
        for g, w in enumerate(POOL_WINDOWS):
            cols = slice(g * POOL_GROUP, (g + 1) * POOL_GROUP)
            u = jnp.dot(xb, win_ref[:, cols], preferred_element_type=F32)
            us_buf[:, cols] = u
            acc = u
            for j in range(1, w):
                acc = acc + st_buf[POOL_STATE - j, :, cols]
            h_buf[0:rows_s, cols] = mix(xb, acc * (1.0 / min(w, PAST_LEN + 1)) - u, g)
        append_new.start()

        def set_ys(rs, cs, v):
            ys_ref[rs, cs] = v

        _outproj_ln(h_buf, lambda rs, cs: xs_ref[rs, cs], wout_ref, g_ref, b_ref,
                    set_ys, lambda rs, cs: ys_ref[rs, cs], rows_s)
        shift_old.wait()
        append_new.wait()


def _pool_layer(x, xs, state_rows, layer, w_in, w_out, params):
    B, S, D = x.shape
    R = xs.shape[0]
    tiles_per_batch = S // TM
    n_prompt = B * tiles_per_batch

    def tile(step):
        step = jnp.clip(step, 0, n_prompt - 1)
        return step // tiles_per_batch, step % tiles_per_batch

    hbm = pl.BlockSpec(memory_space=pl.ANY)
    (wgrp, wgrp_spec), (scale, scale_spec), (g, g_spec), (b, b_spec) = params
    return pl.pallas_call(
        functools.partial(_pool_layer_kernel, layer, tiles_per_batch),
        grid=(n_prompt + 1,),
        in_specs=[pl.BlockSpec((1, TM, D), lambda i: (*tile(i), 0)), _resident(xs.shape), hbm,
                  hbm, wgrp_spec, scale_spec, hbm, g_spec, b_spec],
        out_specs=[
            pl.BlockSpec((1, TM, D), lambda i: (*tile(i - 1), 0)),
            pl.BlockSpec((1, HALO, D), lambda i: (tile(i)[0], 0, 0)),
            pl.BlockSpec((R, D), lambda i: (0, 0)),
            pl.BlockSpec(memory_space=pl.ANY),
        ],
        out_shape=[jax.ShapeDtypeStruct((B, S, D), F32), jax.ShapeDtypeStruct((B, HALO, D), F32),
                   jax.ShapeDtypeStruct((R, D), F32), jax.ShapeDtypeStruct((POOL_STATE, R, D), F32)],
        scratch_shapes=[pltpu.VMEM((TM, D), BF16), pltpu.VMEM((HALO + TM, D), F32), pltpu.VMEM((TM, D), BF16),
                        pltpu.VMEM((TM, D), F32), pltpu.VMEM((POOL_STATE, R, D), F32), pltpu.VMEM((R, D), F32),
                        pltpu.VMEM(w_in.shape[1:], BF16), pltpu.VMEM(w_out.shape[1:], BF16),
                        pltpu.VMEM((2, CAST_ROWS, w_in.shape[2]), F32),
                        pltpu.SemaphoreType.DMA((3,)), pltpu.SemaphoreType.DMA((2,))],
        compiler_params=_params(1),
        name="pool_layer",
    )(x, xs, state_rows, w_in, wgrp, scale, w_out, g, b)


def _project_kv(xb, cos, sa, sb, wkv_ref):
    kv = jnp.dot(xb, wkv_ref[...].astype(BF16), preferred_element_type=F32)
    k_slabs = [_rope(kv[:, j * LANES:(j + 1) * LANES], cos, sa, sb) for j in range(KV_DIM // LANES)]
    return k_slabs, kv[:, KV_DIM:]


def _kv_prompt_kernel(x_ref, cos_ref, sa_ref, sb_ref, wkv_ref, knew_ref, vnew_ref, kdup_ref, vt_ref):
    k_slabs, v = _project_kv(x_ref[0].astype(BF16), cos_ref[...], sa_ref[...], sb_ref[...], wkv_ref)
    low = lax.broadcasted_iota(jnp.int32, (TK, LANES), 1) < HEAD_DIM
    for j, k in enumerate(k_slabs):
        swapped = pltpu.roll(k, HEAD_DIM, 1)
        kdup_ref[0, :, (2 * j) * LANES:(2 * j + 1) * LANES] = jnp.where(low, k, swapped).astype(BF16)
        kdup_ref[0, :, (2 * j + 1) * LANES:(2 * j + 2) * LANES] = jnp.where(low, swapped, k).astype(BF16)
    for i in range(TK // WINDOW):
        vt_ref[0, i] = v[i * WINDOW:(i + 1) * WINDOW, :].T.astype(BF16)

    @pl.when(pl.program_id(1) == pl.num_programs(1) - 1)
    def _():
        for j, k in enumerate(k_slabs):
            knew_ref[0, :, j * LANES:(j + 1) * LANES] = k[TK - WINDOW:, :]
        vnew_ref[0] = v[TK - WINDOW:, :]


def _kv_prompt(x, tables, wkv):
    B, S, D = x.shape
    tab = pl.BlockSpec((TK, LANES), lambda bi, t: (t, 0))
    last = pl.BlockSpec((1, WINDOW, KV_DIM), lambda bi, t: (bi, 0, 0))
    return pl.pallas_call(
        _kv_prompt_kernel,
        grid=(B, S // TK),
        in_specs=[pl.BlockSpec((1, TK, D), lambda bi, t: (bi, t, 0)), tab, tab, tab, _resident(wkv.shape)],
        out_specs=[last, last,
                   pl.BlockSpec((1, TK, N_KV_HEADS * LANES), lambda bi, t: (bi, t, 0)),
                   pl.BlockSpec((1, TK // WINDOW, KV_DIM, WINDOW), lambda bi, t: (bi, t, 0, 0))],
        out_shape=[jax.ShapeDtypeStruct((B, WINDOW, KV_DIM), F32), jax.ShapeDtypeStruct((B, WINDOW, KV_DIM), F32),
                   jax.ShapeDtypeStruct((B, S, N_KV_HEADS * LANES), BF16),
                   jax.ShapeDtypeStruct((B, S // WINDOW, KV_DIM, WINDOW), BF16)],
        compiler_params=_params(2),
        name="kv_prompt",
    )(x, *tables, wkv)


def _kv_sample_kernel(x_ref, cos_ref, sa_ref, sb_ref, wkv_ref, k_ref, v_ref):
    k_slabs, v = _project_kv(x_ref[...].astype(BF16), cos_ref[...], sa_ref[...], sb_ref[...], wkv_ref)
    for j, k in enumerate(k_slabs):
        k_ref[:, j * LANES:(j + 1) * LANES] = k
    v_ref[...] = v


def _kv_sample(x, tables, wkv):
    R, D = x.shape
    out = pl.BlockSpec((R, KV_DIM), lambda i: (0, 0))
    return pl.pallas_call(
        _kv_sample_kernel,
        grid=(1,),
        in_specs=[_resident(x.shape)] + [_resident(t.shape) for t in tables] + [_resident(wkv.shape)],
        out_specs=[out, out],
        out_shape=[jax.ShapeDtypeStruct((R, KV_DIM), F32), jax.ShapeDtypeStruct((R, KV_DIM), F32)],
        compiler_params=_params(1),
        name="kv_sample",
    )(x, *tables, wkv)


def _attn_layer_kernel(layer, tiles_per_batch,
                       x_ref, cos_ref, sa_ref, sb_ref, bias_ref, kdup_ref, vt_ref, xs_ref, cos_s_ref, sa_s_ref, sb_s_ref,
                       win_hbm, sink_ref, wout_hbm, g_ref, b_ref,
                       o_ref, qs_ref, gs_ref, xb_buf, q_buf, h_buf, g_buf, r_buf, win_ref, wout_ref, stage, wsems):
    step = pl.program_id(0)
    n_prompt = pl.num_programs(0) - 1
    group_cols = GROUP * HEAD_DIM

    def set_out(rs, cs, v):
        o_ref[0, rs, cs] = v

    def set_r(rs, cs, v):
        r_buf[rs, cs] = v

    def norm_previous_tile():
        return _layer_norm(lambda rs, cs: r_buf[rs, cs], set_out, g_ref, b_ref, TM)

    @pl.when(step == 0)
    def _():
        r_buf[...] = jnp.zeros(r_buf.shape, F32)
        _load_as_bf16(win_hbm.at[layer], win_ref, stage, wsems)
        _load_as_bf16(wout_hbm.at[layer], wout_ref, stage, wsems)

    @pl.when(step < n_prompt)
    def _():
        t = step % tiles_per_batch
        xb_buf[...] = x_ref[0].astype(BF16)

        def project_q(kv, anchor):
            q = jnp.dot(xb_buf[...], win_ref[:, kv * group_cols:(kv + 1) * group_cols], preferred_element_type=F32)
            if anchor is not None:
                q = q + anchor
            for j in range(group_cols // LANES):
                qj = _rope(q[:, j * LANES:(j + 1) * LANES], cos_ref[...], sa_ref[...], sb_ref[...])
                c0 = kv * group_cols + j * LANES
                q_buf[:, c0:c0 + LANES] = qj.astype(BF16)

        def project_gate(kv, anchor):
            cols = slice(kv * group_cols, (kv + 1) * group_cols)
            gate = jnp.dot(xb_buf[...], win_ref[:, D_MODEL + kv * group_cols:D_MODEL + (kv + 1) * group_cols],
                           preferred_element_type=F32)
            g_buf[:, cols] = _silu(gate) if anchor is None else _silu(gate) + anchor

        low_half = lax.broadcasted_iota(jnp.int32, (WINDOW, LANES), 1) < HEAD_DIM
        ones_rows = jnp.ones((SUM_ROWS, 2 * WINDOW), BF16)

        def block_ids(qb):
            blk = t * (TM // WINDOW) + qb
            return blk, jnp.maximum(blk - 1, 0)

        def group_heads(kv):
            return [(kv * (GROUP // 2) + pair, par) for pair in range(GROUP // 2) for par in range(2)]

        def scores(qb, kv):
            blk, prev_blk = block_ids(qb)
            prev = pl.multiple_of(prev_blk * WINDOW, WINDOW)
            cur = pl.multiple_of(blk * WINDOW, WINDOW)
            rows = slice(qb * WINDOW, (qb + 1) * WINDOW)
            ks = slice(kv * LANES, (kv + 1) * LANES)
            k2 = jnp.concatenate([kdup_ref[0, pl.ds(prev, WINDOW), ks], kdup_ref[0, pl.ds(cur, WINDOW), ks]], axis=0)
            q_all = []
            for slab, par in group_heads(kv):
                q_slab = q_buf[rows, slab * LANES:(slab + 1) * LANES]
                q_all.append(jnp.where(low_half == (par == 0), q_slab, jnp.zeros_like(q_slab)))
            return lax.dot_general(k2, jnp.concatenate(q_all, axis=0), (((1,), (1,)), ((), ())),
                                   preferred_element_type=F32)

        def finish(qb, kv, s_t):
            blk, prev_blk = block_ids(qb)
            bias = bias_ref[jnp.minimum(blk, 1)]
            rows = slice(qb * WINDOW, (qb + 1) * WINDOW)
            vs = slice(kv * HEAD_DIM, (kv + 1) * HEAD_DIM)
            v_aug = jnp.concatenate([vt_ref[0, prev_blk, vs, :], vt_ref[0, blk, vs, :]], axis=1)
            v_aug = jnp.concatenate([v_aug, ones_rows], axis=0)
            for pair in range(GROUP // 2):
                slab = kv * (GROUP // 2) + pair
                p_t, sink_terms = [], []
                for par in range(2):
                    i = 2 * pair + par
                    s = s_t[:, i * WINDOW:(i + 1) * WINDOW] + bias
                    sink = sink_ref[2 * slab + par]
                    m = jnp.maximum(jnp.max(s, axis=0, keepdims=True), sink)
                    p_t.append(jnp.exp(s - m).astype(BF16))
                    sink_terms.append(jnp.exp(sink - m))
                o_t = jnp.dot(v_aug, jnp.concatenate(p_t, axis=1), preferred_element_type=F32)
                both = []
                for par in range(2):
                    cs = slice(par * WINDOW, (par + 1) * WINDOW)
                    inv = 1.0 / (o_t[HEAD_DIM:HEAD_DIM + 1, cs] + sink_terms[par])
                    both.append(o_t[:HEAD_DIM, cs] * inv)
                attn = jnp.concatenate(both, axis=0).T
                cs = slice(slab * LANES, (slab + 1) * LANES)
                h_buf[rows, cs] = (attn * g_buf[rows, cs]).astype(BF16)

        project_q(0, None)
        anchors = norm_previous_tile()
        early = 2 * (N_KV_HEADS // 2)
        per_proj = len(anchors) // early

        def next_anchor():
            if not anchors:
                return None
            anchor = sum(anchors.pop(0) for _ in range(per_proj))
            return jnp.tile(anchor, (TM // anchor.shape[0], group_cols // anchor.shape[1]))

        for kv in range(N_KV_HEADS):
            s_t = [scores(qb, kv) for qb in range(TM // WINDOW)]
            if kv + 1 < N_KV_HEADS:
                project_q(kv + 1, next_anchor())
            project_gate(kv, next_anchor())
            for qb in range(TM // WINDOW):
                finish(qb, kv, s_t[qb])
        _outproj_residual(h_buf, lambda rs, cs: x_ref[0, rs, cs], wout_ref, set_r, TM)

    @pl.when(step == n_prompt)
    def _():
        norm_previous_tile()
        xb = xs_ref[...].astype(BF16)
        for n in range(N_CHUNKS):
            q = jnp.dot(xb, win_ref[:, n * COL_CHUNK:(n + 1) * COL_CHUNK], preferred_element_type=F32)
            for j in range(COL_CHUNK // LANES):
                c0 = n * COL_CHUNK + j * LANES
                qs_ref[:, c0:c0 + LANES] = _rope(q[:, j * LANES:(j + 1) * LANES],
                                                 cos_s_ref[...], sa_s_ref[...], sb_s_ref[...])
            gs_ref[:, n * COL_CHUNK:(n + 1) * COL_CHUNK] = jnp.dot(
                xb, win_ref[:, D_MODEL + n * COL_CHUNK:D_MODEL + (n + 1) * COL_CHUNK],
                preferred_element_type=F32)


def _band_bias():
    key = np.arange(2 * WINDOW)[:, None]
    qry = np.arange(WINDOW)[None, :]
    band = (key > qry) & (key <= qry + WINDOW)
    return jnp.asarray(np.where(np.stack([band & (key >= WINDOW), band]), 0.0, NEG).astype(np.float32))


def _attn_layer(x, tables, bias, kdup, vt, xs, tables_s, layer, w_in, sinks, w_out, g, b):
    B, S, D = x.shape
    R = xs.shape[0]
    tiles_per_batch = S // TM
    n_prompt = B * tiles_per_batch

    def tile(step):
        step = jnp.clip(step, 0, n_prompt - 1)
        return step // tiles_per_batch, step % tiles_per_batch

    tab = pl.BlockSpec((TM, LANES), lambda i: (tile(i)[1], 0))

    def per_batch(a):
        return pl.BlockSpec((1,) + a.shape[1:], lambda i: (tile(i)[0],) + (0,) * (a.ndim - 1),
                            pipeline_mode=pl.Buffered(1))

    rows = pl.BlockSpec((R, D), lambda i: (0, 0))
    return pl.pallas_call(
        functools.partial(_attn_layer_kernel, layer, tiles_per_batch),
        grid=(n_prompt + 1,),
        in_specs=[
            pl.BlockSpec((1, TM, D), lambda i: (*tile(i), 0)), tab, tab, tab, _resident(bias.shape),
            per_batch(kdup), per_batch(vt),
            _resident(xs.shape)] + [_resident(t.shape) for t in tables_s] + [
            pl.BlockSpec(memory_space=pl.ANY), pl.BlockSpec(memory_space=pltpu.SMEM),
            pl.BlockSpec(memory_space=pl.ANY), g[1], b[1],
        ],
        out_specs=[pl.BlockSpec((1, TM, D), lambda i: (*tile(i - 1), 0)), rows, rows],
        out_shape=[jax.ShapeDtypeStruct((B, S, D), F32),
                   jax.ShapeDtypeStruct((R, D), F32), jax.ShapeDtypeStruct((R, D), F32)],
        scratch_shapes=[pltpu.VMEM((TM, D), BF16), pltpu.VMEM((TM, D), BF16), pltpu.VMEM((TM, D), BF16),
                        pltpu.VMEM((TM, D), F32), pltpu.VMEM((TM, D), F32),
                        pltpu.VMEM(w_in.shape[1:], BF16), pltpu.VMEM(w_out.shape[1:], BF16),
                        pltpu.VMEM((2, CAST_ROWS, w_in.shape[2]), F32), pltpu.SemaphoreType.DMA((2,))],
        compiler_params=_params(1),
        name="attn_layer",
    )(x, *tables, bias, kdup, vt, xs, *tables_s, w_in, sinks, w_out, g[0], b[0])


def _attn_sample_kernel(q_ref, kn_ref, vn_ref, ck_ref, cv_ref, sink_ref, o_ref, nk_ref, nv_ref):
    nb = q_ref.shape[0]
    head_of_lane = lax.broadcasted_iota(jnp.int32, (N_HEADS, D_MODEL), 1) // HEAD_DIM
    own_head = head_of_lane == lax.broadcasted_iota(jnp.int32, (N_HEADS, D_MODEL), 0)
    low_half = lax.broadcasted_iota(jnp.int32, (N_HEADS, LANES), 1) < HEAD_DIM
    last_row = lax.broadcasted_iota(jnp.int32, (WINDOW, KV_DIM), 0) == WINDOW - 1
    sink = sink_ref[...]
    heads_per_slab = LANES // HEAD_DIM
    slabs_per_group = GROUP // heads_per_slab

    def body(i, carry):
        newk = jnp.where(last_row, kn_ref[pl.ds(i, 1), :], pltpu.roll(ck_ref[i], WINDOW - 1, 0))
        newv = jnp.where(last_row, vn_ref[pl.ds(i, 1), :], pltpu.roll(cv_ref[i], WINDOW - 1, 0))
        nk_ref[i] = newk
        nv_ref[i] = newv
        qh = jnp.where(own_head, jnp.broadcast_to(q_ref[pl.ds(i, 1), :], (N_HEADS, D_MODEL)), 0.0)
        folded = []
        for kv in range(N_KV_HEADS):
            w = qh[:, kv * GROUP * HEAD_DIM:kv * GROUP * HEAD_DIM + LANES]
            for sl in range(1, slabs_per_group):
                c0 = kv * GROUP * HEAD_DIM + sl * LANES
                w = w + qh[:, c0:c0 + LANES]
            folded.append(w + pltpu.roll(w, HEAD_DIM, 1))
        qg = jnp.concatenate([jnp.where(low_half, folded[2 * j], folded[2 * j + 1])
                              for j in range(N_KV_HEADS // 2)], axis=1)
        s = lax.dot_general(qg.astype(BF16), newk.astype(BF16), (((1,), (1,)), ((), ())),
                            preferred_element_type=F32)
        m = jnp.maximum(jnp.max(s, axis=-1, keepdims=True), sink)
        p = jnp.exp(s - m)
        denom = jnp.sum(p, axis=-1, keepdims=True) + jnp.exp(sink - m)
        og = jnp.dot(p.astype(BF16), newv.astype(BF16), preferred_element_type=F32) / denom
        slabs = []
        for kv in range(N_KV_HEADS):
            xs = og[:, (kv // 2) * LANES:(kv // 2 + 1) * LANES]
            rolled = pltpu.roll(xs, HEAD_DIM, 1)
            both = jnp.where(low_half, xs, rolled) if kv % 2 == 0 else jnp.where(low_half, rolled, xs)
            slabs.extend([both] * slabs_per_group)
        full = jnp.concatenate(slabs, axis=1)
        o_ref[pl.ds(i, 1), :] = jnp.sum(jnp.where(own_head, full, 0.0), axis=0, keepdims=True)
        return carry

    lax.fori_loop(0, nb, body, 0, unroll=True)


def _attn_sample(q, kn, vn, ck, cv, sinks_col):
    R, D = q.shape
    nb = SAMPLE_ATTN_BATCH
    row2 = lambda w: pl.BlockSpec((nb, w), lambda i: (i, 0))
    cache = pl.BlockSpec((nb, WINDOW, KV_DIM), lambda i: (i, 0, 0))
    return pl.pallas_call(
        _attn_sample_kernel,
        grid=(R // nb,),
        in_specs=[row2(D), row2(KV_DIM), row2(KV_DIM), cache, cache, _resident(sinks_col.shape)],
        out_specs=[row2(D), cache, cache],
        out_shape=[jax.ShapeDtypeStruct((R, D), F32),
                   jax.ShapeDtypeStruct(ck.shape, F32), jax.ShapeDtypeStruct(cv.shape, F32)],
        compiler_params=_params(1),
        name="attn_sample",
    )(q, kn, vn, ck, cv, sinks_col)


def _gated_out_sample_kernel(x_ref, a_ref, gate_ref, wout_ref, g_ref, b_ref, o_ref, h_buf):
    rows = x_ref.shape[0]
    h_buf[...] = (a_ref[...] * _silu(gate_ref[...])).astype(BF16)

    def set_rows(rs, cs, v):
        o_ref[rs, cs] = v

    for n in range(N_CHUNKS):
        cols = slice(n * COL_CHUNK, (n + 1) * COL_CHUNK)
        y = jnp.dot(h_buf[...], wout_ref[:, cols].astype(BF16), preferred_element_type=F32)
        o_ref[:, cols] = ALPHA * x_ref[:, cols] + y
    _layer_norm(lambda rs, cs: o_ref[rs, cs], set_rows, g_ref, b_ref, rows)


def _gated_out_sample(x, a, gate, wout, g, b):
    R, D = x.shape
    return pl.pallas_call(
        _gated_out_sample_kernel,
        grid=(1,),
        in_specs=[_resident(t.shape) for t in (x, a, gate)] + [wout[1], g[1], b[1]],
        out_specs=pl.BlockSpec((R, D), lambda i: (0, 0)),
        out_shape=jax.ShapeDtypeStruct((R, D), F32),
        scratch_shapes=[pltpu.VMEM((R, D), BF16)],
        compiler_params=_params(1),
        name="gated_out_sample",
    )(x, a, gate, wout[0], g[0], b[0])


def _rope_tables(pos, scale=1.0):
    half = ROT_DIM // 2
    inv_freq = (ROPE_THETA ** (-np.arange(0, ROT_DIM, 2, dtype=np.float32) / ROT_DIM)).astype(np.float32)
    ang = pos.astype(np.float32)[:, None] * inv_freq[None, :]
    cos, sin = np.cos(ang), np.sin(ang)
    n = pos.shape[0]
    rest = np.zeros((n, HEAD_DIM - ROT_DIM), np.float32)
    zero = np.zeros((n, half), np.float32)
    cos_h = np.concatenate([cos, cos, rest + 1.0], axis=1)
    sa_h = np.concatenate([-sin, zero, rest], axis=1)
    sb_h = np.concatenate([zero, sin, rest], axis=1)
    rep = LANES // HEAD_DIM
    return tuple(jnp.asarray(np.tile(a, (1, rep)) * np.float32(scale)) for a in (cos_h, sa_h, sb_h))


def kernel(x_prompt, x_sample, state_pool, cache_k, cache_v, w_in_a, w_grp_a, scale_a, w_out_a,
           w_kv, w_in_b, sinks_b, w_out_b, ln_g, ln_b):
    B, S, D = x_prompt.shape
    R = x_sample.shape[0]
    xp = x_prompt
    xs = x_sample.reshape(R, D)
    pos_p, pos_s = np.arange(S), np.full((R,), PAST_LEN)
    tab_p, tab_s = _rope_tables(pos_p), _rope_tables(pos_s)
    qtab_p, qtab_s = _rope_tables(pos_p, SM_SCALE), _rope_tables(pos_s, SM_SCALE)
    scale_a, ln_g, ln_b = (p.reshape(p.shape[0], 1, D) for p in (scale_a, ln_g, ln_b))
    state_rows = state_pool.transpose(0, 2, 1, 3)
    pool_p, pool_s = [], []
    for i in range(N_A_LAYERS):
        params = [_layer(p, i) for p in (w_grp_a, scale_a, ln_g, ln_b)]
        xp, sp, xs, ss = _pool_layer(xp, xs, state_rows, i, w_in_a, w_out_a, params)
        pool_p.append(sp[:, HALO - POOL_STATE:])
        pool_s.append(ss)
    new_k_p, new_v_p, kdup_p, vt_p = _kv_prompt(xp, tab_p, w_kv)
    k_s, v_s = _kv_sample(xs, tab_s, w_kv)
    bias = _band_bias()
    ck = cache_k.reshape(R, WINDOW, KV_DIM)
    cv = cache_v.reshape(R, WINDOW, KV_DIM)
    for j in range(DEPTH - N_A_LAYERS):
        i = N_A_LAYERS + j
        wout, g, b = _layer(w_out_b, j), _layer(ln_g, i), _layer(ln_b, i)
        xp, q_s, gate_s = _attn_layer(xp, qtab_p, bias, kdup_p, vt_p, xs, qtab_s, j, w_in_b, sinks_b[j], w_out_b, g, b)
        a_s, nk, nv = _attn_sample(q_s, k_s, v_s, ck, cv, sinks_b[j][:, None])
        xs = _gated_out_sample(xs, a_s, gate_s, wout, g, b)
    kv4 = (N_KV_HEADS, HEAD_DIM)
    return (xp, xs.reshape(R, 1, D), jnp.stack(pool_p, axis=0), jnp.stack(pool_s, axis=0).transpose(0, 2, 1, 3),
            new_k_p.reshape(B, WINDOW, *kv4), new_v_p.reshape(B, WINDOW, *kv4),
            nk.reshape(R, WINDOW, *kv4), nv.reshape(R, WINDOW, *kv4))
```

```python
import functools

import jax
import jax.numpy as jnp
import numpy as np
from jax import lax
from jax.experimental import pallas as pl
from jax.experimental.pallas import tpu as pltpu

F32 = jnp.float32
BF16 = jnp.bfloat16

D_MODEL = 2048
DEPTH = 4
PAST_LEN = 16384
N_A_LAYERS = DEPTH // 2
POOL_WINDOWS = (2, 4, 8, 16)
POOL_GROUP = D_MODEL // len(POOL_WINDOWS)
POOL_STATE = max(POOL_WINDOWS) - 1
HEAD_DIM = 64
N_HEADS = D_MODEL // HEAD_DIM
N_KV_HEADS = N_HEADS // 8
GROUP = N_HEADS // N_KV_HEADS
KV_DIM = N_KV_HEADS * HEAD_DIM
WINDOW = 128
ROT_DIM = HEAD_DIM // 4
ROPE_THETA = 500000.0
ALPHA = (2 * DEPTH) ** 0.25
LN_EPS = 1e-5
NEG = -1e30
SM_SCALE = HEAD_DIM ** -0.5

LANES = 128
SUBLANES = 8
HALO = 16
COL_CHUNK = 512
N_CHUNKS = D_MODEL // COL_CHUNK
TM = 256
TK = 1024
LN_ROWS = 16
SAMPLE_ATTN_BATCH = 8
SUM_ROWS = 16
CAST_ROWS = 128
VMEM_LIMIT_BYTES = 60 * 1024 * 1024


def _params(n_axes):
    return pltpu.CompilerParams(dimension_semantics=("arbitrary",) * n_axes,
                                vmem_limit_bytes=VMEM_LIMIT_BYTES)


def _resident(shape):
    zeros = (0,) * len(shape)
    return pl.BlockSpec(shape, lambda *_: zeros, pipeline_mode=pl.Buffered(1))


def _layer(stacked, i):
    zeros = (0,) * (stacked.ndim - 1)
    return stacked, pl.BlockSpec((None,) + stacked.shape[1:], lambda *_: (i,) + zeros,
                                 pipeline_mode=pl.Buffered(1))


def _silu(g):
    return g / (1.0 + jnp.exp(-g))


def _rope(x, cos, sa, sb):
    return x * cos + pltpu.roll(x, LANES - ROT_DIM // 2, 1) * sa + pltpu.roll(x, ROT_DIM // 2, 1) * sb


def _load_as_bf16(w_hbm, dst_ref, stage, sems):
    n_cols = dst_ref.shape[1]
    chunk = stage.shape[1]
    n_chunks = dst_ref.shape[0] // chunk

    def copy(c):
        return pltpu.make_async_copy(w_hbm.at[pl.ds(c * chunk, chunk)],
                                     stage.at[c % 2, :, pl.ds(0, n_cols)], sems.at[c % 2])

    copy(0).start()
    for c in range(n_chunks):
        if c + 1 < n_chunks:
            copy(c + 1).start()
        copy(c).wait()
        dst_ref[c * chunk:(c + 1) * chunk, :] = stage[c % 2, :, 0:n_cols].astype(BF16)


def _outproj_residual(h_ref, x_rows, wout_ref, r_rows_set, rows):
    for n in range(N_CHUNKS):
        cols = slice(n * COL_CHUNK, (n + 1) * COL_CHUNK)
        y = jnp.dot(h_ref[0:rows, :], wout_ref[:, cols], preferred_element_type=F32)
        r_rows_set(slice(0, rows), cols, ALPHA * x_rows(slice(0, rows), cols) + y)


def _zero_after(v):
    bits = pltpu.bitcast(v, jnp.int32)
    half = jnp.full(bits.shape, 16, jnp.int32)
    return lax.shift_right_logical(lax.shift_right_logical(bits, half), half).astype(F32)


def _layer_norm(r_rows_get, o_rows_set, g_ref, b_ref, rows):
    step = min(LN_ROWS, rows)
    anchors = []
    for r0 in range(0, rows, step):
        rs = slice(r0, r0 + step)
        r = r_rows_get(rs, slice(None))
        mu = jnp.mean(r, axis=-1, keepdims=True)
        c = r - mu
        var = jnp.mean(c * c, axis=-1, keepdims=True)
        out = c * lax.rsqrt(var + LN_EPS) * g_ref[...] + b_ref[...]
        o_rows_set(rs, slice(None), out)
        folded = sum(out[i:i + SUBLANES, j:j + LANES]
                     for i in range(0, step, SUBLANES) for j in range(0, out.shape[1], LANES))
        anchors.append(_zero_after(folded))
    return anchors


def _outproj_ln(h_ref, x_rows, wout_ref, g_ref, b_ref, o_rows_set, o_rows_get, rows):
    _outproj_residual(h_ref, x_rows, wout_ref, o_rows_set, rows)
    _layer_norm(o_rows_get, o_rows_set, g_ref, b_ref, rows)


def _pool_layer_kernel(layer, tiles_per_batch,
                       x_ref, xs_ref, st_hbm, win_hbm, wgrp_ref, scale_ref, wout_hbm, g_ref, b_ref,
                       o_ref, state_ref, ys_ref, nst_hbm,
                       xb_buf, u_buf, h_buf, r_buf, st_buf, us_buf, win_ref, wout_ref, stage, sems, wsems):
    step = pl.program_id(0)
    n_prompt = pl.num_programs(0) - 1
    rows_s = xs_ref.shape[0]

    def mix(xb, d, g):
        cols = slice(g * POOL_GROUP, (g + 1) * POOL_GROUP)
        gate = jnp.dot(xb, win_ref[:, D_MODEL + g * POOL_GROUP:D_MODEL + (g + 1) * POOL_GROUP],
                       preferred_element_type=F32)
        d = jnp.dot(d.astype(BF16), wgrp_ref[g].astype(BF16), preferred_element_type=F32) * scale_ref[:, cols]
        return (d * _silu(gate)).astype(BF16)

    def set_out(rs, cs, v):
        o_ref[0, rs, cs] = v

    def set_r(rs, cs, v):
        r_buf[rs, cs] = v

    def norm_previous_tile():
        return _layer_norm(lambda rs, cs: r_buf[rs, cs], set_out, g_ref, b_ref, TM)

    @pl.when(step == 0)
    def _():
        r_buf[...] = jnp.zeros(r_buf.shape, F32)
        _load_as_bf16(win_hbm.at[layer], win_ref, stage, wsems)
        _load_as_bf16(wout_hbm.at[layer], wout_ref, stage, wsems)

    @pl.when(step < n_prompt)
    def _():
        t = step % tiles_per_batch

        @pl.when(t == 0)
        def _():
            u_buf[0:HALO, :] = jnp.zeros((HALO, D_MODEL), F32)

        xb_buf[...] = x_ref[0].astype(BF16)
        row = lax.broadcasted_iota(jnp.int32, (TM, 1), 0) + t * TM

        def project_u(g):
            cols = slice(g * POOL_GROUP, (g + 1) * POOL_GROUP)
            u_buf[HALO:, cols] = jnp.dot(xb_buf[...], win_ref[:, cols], preferred_element_type=F32)

        project_u(0)
        anchors = norm_previous_tile()
        per_group = len(anchors) // len(POOL_WINDOWS)
        for g, w in enumerate(POOL_WINDOWS):
            cols = slice(g * POOL_GROUP, (g + 1) * POOL_GROUP)
            if g + 1 < len(POOL_WINDOWS):
                project_u(g + 1)
            ext = u_buf[:, cols]
            s = ext
            shift = 1
            while shift < w:
                s = s + pltpu.roll(s, shift, 0)
                shift *= 2
            inv_cnt = 1.0 / jnp.minimum(w, row + 1).astype(F32)
            anchor = sum(anchors[g * per_group:(g + 1) * per_group])
            anchor = jnp.tile(anchor, (TM // anchor.shape[0], POOL_GROUP // anchor.shape[1]))
            h_buf[:, cols] = mix(xb_buf[...], s[HALO:, :] * inv_cnt - ext[HALO:, :] + anchor, g)

        state_ref[0] = u_buf[TM:TM + HALO, :]
        u_buf[0:HALO, :] = u_buf[TM:TM + HALO, :]
        _outproj_residual(h_buf, lambda rs, cs: x_ref[0, rs, cs], wout_ref, set_r, TM)

    @pl.when(step == n_prompt)
    def _():
        load = pltpu.make_async_copy(st_hbm.at[layer], st_buf, sems.at[0])
        shift_old = pltpu.make_async_copy(st_buf.at[pl.ds(1, POOL_STATE - 1)],
                                          nst_hbm.at[pl.ds(0, POOL_STATE - 1)], sems.at[1])
        append_new = pltpu.make_async_copy(us_buf, nst_hbm.at[POOL_STATE - 1], sems.at[2])
        load.start()
        norm_previous_tile()
        xb = xs_ref[...].astype(BF16)
        load.wait()
        shift_old.start()
        for g, w in enumerate(POOL_WINDOWS):
            cols = slice(g * POOL_GROUP, (g + 1) * POOL_GROUP)
            u = jnp.dot(xb, win_ref[:, cols], preferred_element_type=F32)
            us_buf[:, cols] = u
            acc = u
            for j in range(1, w):
                acc = acc + st_buf[POOL_STATE - j, :, cols]
            h_buf[0:rows_s, cols] = mix(xb, acc * (1.0 / min(w, PAST_LEN + 1)) - u, g)
        append_new.start()

        def set_ys(rs, cs, v):
            ys_ref[rs, cs] = v

        _outproj_ln(h_buf, lambda rs, cs: xs_ref[rs, cs], wout_ref, g_ref, b_ref,
                    set_ys, lambda rs, cs: ys_ref[rs, cs], rows_s)
        shift_old.wait()
        append_new.wait()


def _pool_layer(x, xs, state_rows, layer, w_in, w_out, params):
    B, S, D = x.shape
    R = xs.shape[0]
    tiles_per_batch = S // TM
    n_prompt = B * tiles_per_batch

    def tile(step):
        step = jnp.clip(step, 0, n_prompt - 1)
        return step // tiles_per_batch, step % tiles_per_batch

    hbm = pl.BlockSpec(memory_space=pl.ANY)
    (wgrp, wgrp_spec), (scale, scale_spec), (g, g_spec), (b, b_spec) = params
    return pl.pallas_call(
        functools.partial(_pool_layer_kernel, layer, tiles_per_batch),
        grid=(n_prompt + 1,),
        in_specs=[pl.BlockSpec((1, TM, D), lambda i: (*tile(i), 0)), _resident(xs.shape), hbm,
                  hbm, wgrp_spec, scale_spec, hbm, g_spec, b_spec],
        out_specs=[
            pl.BlockSpec((1, TM, D), lambda i: (*tile(i - 1), 0)),
            pl.BlockSpec((1, HALO, D), lambda i: (tile(i)[0], 0, 0)),
            pl.BlockSpec((R, D), lambda i: (0, 0)),
            pl.BlockSpec(memory_space=pl.ANY),
        ],
        out_shape=[jax.ShapeDtypeStruct((B, S, D), F32), jax.ShapeDtypeStruct((B, HALO, D), F32),
                   jax.ShapeDtypeStruct((R, D), F32), jax.ShapeDtypeStruct((POOL_STATE, R, D), F32)],
        scratch_shapes=[pltpu.VMEM((TM, D), BF16), pltpu.VMEM((HALO + TM, D), F32), pltpu.VMEM((TM, D), BF16),
                        pltpu.VMEM((TM, D), F32), pltpu.VMEM((POOL_STATE, R, D), F32), pltpu.VMEM((R, D), F32),
                        pltpu.VMEM(w_in.shape[1:], BF16), pltpu.VMEM(w_out.shape[1:], BF16),
                        pltpu.VMEM((2, CAST_ROWS, w_in.shape[2]), F32),
                        pltpu.SemaphoreType.DMA((3,)), pltpu.SemaphoreType.DMA((2,))],
        compiler_params=_params(1),
        name="pool_layer",
    )(x, xs, state_rows, w_in, wgrp, scale, w_out, g, b)


def _project_kv(xb, cos, sa, sb, wkv_ref):
    kv = jnp.dot(xb, wkv_ref[...].astype(BF16), preferred_element_type=F32)
    k_slabs = [_rope(kv[:, j * LANES:(j + 1) * LANES], cos, sa, sb) for j in range(KV_DIM // LANES)]
    return k_slabs, kv[:, KV_DIM:]


def _kv_prompt_kernel(x_ref, cos_ref, sa_ref, sb_ref, wkv_ref, knew_ref, vnew_ref, kdup_ref, vt_ref):
    k_slabs, v = _project_kv(x_ref[0].astype(BF16), cos_ref[...], sa_ref[...], sb_ref[...], wkv_ref)
    low = lax.broadcasted_iota(jnp.int32, (TK, LANES), 1) < HEAD_DIM
    for j, k in enumerate(k_slabs):
        swapped = pltpu.roll(k, HEAD_DIM, 1)
        kdup_ref[0, :, (2 * j) * LANES:(2 * j + 1) * LANES] = jnp.where(low, k, swapped).astype(BF16)
        kdup_ref[0, :, (2 * j + 1) * LANES:(2 * j + 2) * LANES] = jnp.where(low, swapped, k).astype(BF16)
    for i in range(TK // WINDOW):
        vt_ref[0, i] = v[i * WINDOW:(i + 1) * WINDOW, :].T.astype(BF16)

    @pl.when(pl.program_id(1) == pl.num_programs(1) - 1)
    def _():
        for j, k in enumerate(k_slabs):
            knew_ref[0, :, j * LANES:(j + 1) * LANES] = k[TK - WINDOW:, :]
        vnew_ref[0] = v[TK - WINDOW:, :]


def _kv_prompt(x, tables, wkv):
    B, S, D = x.shape
    tab = pl.BlockSpec((TK, LANES), lambda bi, t: (t, 0))
    last = pl.BlockSpec((1, WINDOW, KV_DIM), lambda bi, t: (bi, 0, 0))
    return pl.pallas_call(
        _kv_prompt_kernel,
        grid=(B, S // TK),
        in_specs=[pl.BlockSpec((1, TK, D), lambda bi, t: (bi, t, 0)), tab, tab, tab, _resident(wkv.shape)],
        out_specs=[last, last,
                   pl.BlockSpec((1, TK, N_KV_HEADS * LANES), lambda bi, t: (bi, t, 0)),
                   pl.BlockSpec((1, TK // WINDOW, KV_DIM, WINDOW), lambda bi, t: (bi, t, 0, 0))],
        out_shape=[jax.ShapeDtypeStruct((B, WINDOW, KV_DIM), F32), jax.ShapeDtypeStruct((B, WINDOW, KV_DIM), F32),
                   jax.ShapeDtypeStruct((B, S, N_KV_HEADS * LANES), BF16),
                   jax.ShapeDtypeStruct((B, S // WINDOW, KV_DIM, WINDOW), BF16)],
        compiler_params=_params(2),
        name="kv_prompt",
    )(x, *tables, wkv)


def _kv_sample_kernel(x_ref, cos_ref, sa_ref, sb_ref, wkv_ref, k_ref, v_ref):
    k_slabs, v = _project_kv(x_ref[...].astype(BF16), cos_ref[...], sa_ref[...], sb_ref[...], wkv_ref)
    for j, k in enumerate(k_slabs):
        k_ref[:, j * LANES:(j + 1) * LANES] = k
    v_ref[...] = v


def _kv_sample(x, tables, wkv):
    R, D = x.shape
    out = pl.BlockSpec((R, KV_DIM), lambda i: (0, 0))
    return pl.pallas_call(
        _kv_sample_kernel,
        grid=(1,),
        in_specs=[_resident(x.shape)] + [_resident(t.shape) for t in tables] + [_resident(wkv.shape)],
        out_specs=[out, out],
        out_shape=[jax.ShapeDtypeStruct((R, KV_DIM), F32), jax.ShapeDtypeStruct((R, KV_DIM), F32)],
        compiler_params=_params(1),
        name="kv_sample",
    )(x, *tables, wkv)


def _attn_layer_kernel(layer, tiles_per_batch,
                       x_ref, cos_ref, sa_ref, sb_ref, bias_ref, kdup_ref, vt_ref, xs_ref, cos_s_ref, sa_s_ref, sb_s_ref,
                       win_hbm, sink_ref, wout_hbm, g_ref, b_ref,
                       o_ref, qs_ref, gs_ref, xb_buf, q_buf, h_buf, g_buf, r_buf, win_ref, wout_ref, stage, wsems):
    step = pl.program_id(0)
    n_prompt = pl.num_programs(0) - 1
    group_cols = GROUP * HEAD_DIM

    def set_out(rs, cs, v):
        o_ref[0, rs, cs] = v

    def set_r(rs, cs, v):
        r_buf[rs, cs] = v

    def norm_previous_tile():
        return _layer_norm(lambda rs, cs: r_buf[rs, cs], set_out, g_ref, b_ref, TM)

    @pl.when(step == 0)
    def _():
        r_buf[...] = jnp.zeros(r_buf.shape, F32)
        _load_as_bf16(win_hbm.at[layer], win_ref, stage, wsems)
        _load_as_bf16(wout_hbm.at[layer], wout_ref, stage, wsems)

    @pl.when(step < n_prompt)
    def _():
        t = step % tiles_per_batch
        xb_buf[...] = x_ref[0].astype(BF16)

        def project_q(kv, anchor):
            q = jnp.dot(xb_buf[...], win_ref[:, kv * group_cols:(kv + 1) * group_cols], preferred_element_type=F32)
            if anchor is not None:
                q = q + anchor
            for j in range(group_cols // LANES):
                qj = _rope(q[:, j * LANES:(j + 1) * LANES], cos_ref[...], sa_ref[...], sb_ref[...])
                c0 = kv * group_cols + j * LANES
                q_buf[:, c0:c0 + LANES] = qj.astype(BF16)

        def project_gate(kv, anchor):
            cols = slice(kv * group_cols, (kv + 1) * group_cols)
            gate = jnp.dot(xb_buf[...], win_ref[:, D_MODEL + kv * group_cols:D_MODEL + (kv + 1) * group_cols],
                           preferred_element_type=F32)
            g_buf[:, cols] = _silu(gate) if anchor is None else _silu(gate) + anchor

        low_half = lax.broadcasted_iota(jnp.int32, (WINDOW, LANES), 1) < HEAD_DIM
        ones_rows = jnp.ones((SUM_ROWS, 2 * WINDOW), BF16)

        def block_ids(qb):
            blk = t * (TM // WINDOW) + qb
            return blk, jnp.maximum(blk - 1, 0)

        def group_heads(kv):
            return [(kv * (GROUP // 2) + pair, par) for pair in range(GROUP // 2) for par in range(2)]

        def scores(qb, kv):
            blk, prev_blk = block_ids(qb)
            prev = pl.multiple_of(prev_blk * WINDOW, WINDOW)
            cur = pl.multiple_of(blk * WINDOW, WINDOW)
            rows = slice(qb * WINDOW, (qb + 1) * WINDOW)
            ks = slice(kv * LANES, (kv + 1) * LANES)
            k2 = jnp.concatenate([kdup_ref[0, pl.ds(prev, WINDOW), ks], kdup_ref[0, pl.ds(cur, WINDOW), ks]], axis=0)
            q_all = []
            for slab, par in group_heads(kv):
                q_slab = q_buf[rows, slab * LANES:(slab + 1) * LANES]
                q_all.append(jnp.where(low_half == (par == 0), q_slab, jnp.zeros_like(q_slab)))
            return lax.dot_general(k2, jnp.concatenate(q_all, axis=0), (((1,), (1,)), ((), ())),
                                   preferred_element_type=F32)

        def finish(qb, kv, s_t):
            blk, prev_blk = block_ids(qb)
            bias = bias_ref[jnp.minimum(blk, 1)]
            rows = slice(qb * WINDOW, (qb + 1) * WINDOW)
            vs = slice(kv * HEAD_DIM, (kv + 1) * HEAD_DIM)
            v_aug = jnp.concatenate([vt_ref[0, prev_blk, vs, :], vt_ref[0, blk, vs, :]], axis=1)
            v_aug = jnp.concatenate([v_aug, ones_rows], axis=0)
            for pair in range(GROUP // 2):
                slab = kv * (GROUP // 2) + pair
                p_t, sink_terms = [], []
                for par in range(2):
                    i = 2 * pair + par
                    s = s_t[:, i * WINDOW:(i + 1) * WINDOW] + bias
                    sink = sink_ref[2 * slab + par]
                    m = jnp.maximum(jnp.max(s, axis=0, keepdims=True), sink)
                    p_t.append(jnp.exp(s - m).astype(BF16))
                    sink_terms.append(jnp.exp(sink - m))
                o_t = jnp.dot(v_aug, jnp.concatenate(p_t, axis=1), preferred_element_type=F32)
                both = []
                for par in range(2):
                    cs = slice(par * WINDOW, (par + 1) * WINDOW)
                    inv = 1.0 / (o_t[HEAD_DIM:HEAD_DIM + 1, cs] + sink_terms[par])
                    both.append(o_t[:HEAD_DIM, cs] * inv)
                attn = jnp.concatenate(both, axis=0).T
                cs = slice(slab * LANES, (slab + 1) * LANES)
                h_buf[rows, cs] = (attn * g_buf[rows, cs]).astype(BF16)

        project_q(0, None)
        anchors = norm_previous_tile()
        early = 2 * (N_KV_HEADS // 2)
        per_proj = len(anchors) // early

        def next_anchor():
            if not anchors:
                return None
            anchor = sum(anchors.pop(0) for _ in range(per_proj))
            return jnp.tile(anchor, (TM // anchor.shape[0], group_cols // anchor.shape[1]))

        for kv in range(N_KV_HEADS):
            s_t = [scores(qb, kv) for qb in range(TM // WINDOW)]
            if kv + 1 < N_KV_HEADS:
                project_q(kv + 1, next_anchor())
            project_gate(kv, next_anchor())
            for qb in range(TM // WINDOW):
                finish(qb, kv, s_t[qb])
        _outproj_residual(h_buf, lambda rs, cs: x_ref[0, rs, cs], wout_ref, set_r, TM)

    @pl.when(step == n_prompt)
    def _():
        norm_previous_tile()
        xb = xs_ref[...].astype(BF16)
        for n in range(N_CHUNKS):
            q = jnp.dot(xb, win_ref[:, n * COL_CHUNK:(n + 1) * COL_CHUNK], preferred_element_type=F32)
            for j in range(COL_CHUNK // LANES):
                c0 = n * COL_CHUNK + j * LANES
                qs_ref[:, c0:c0 + LANES] = _rope(q[:, j * LANES:(j + 1) * LANES],
                                                 cos_s_ref[...], sa_s_ref[...], sb_s_ref[...])
            gs_ref[:, n * COL_CHUNK:(n + 1) * COL_CHUNK] = jnp.dot(
                xb, win_ref[:, D_MODEL + n * COL_CHUNK:D_MODEL + (n + 1) * COL_CHUNK],
                preferred_element_type=F32)


def _band_bias():
    key = np.arange(2 * WINDOW)[:, None]
    qry = np.arange(WINDOW)[None, :]
    band = (key > qry) & (key <= qry + WINDOW)
    return jnp.asarray(np.where(np.stack([band & (key >= WINDOW), band]), 0.0, NEG).astype(np.float32))


def _attn_layer(x, tables, bias, kdup, vt, xs, tables_s, layer, w_in, sinks, w_out, g, b):
    B, S, D = x.shape
    R = xs.shape[0]
    tiles_per_batch = S // TM
    n_prompt = B * tiles_per_batch

    def tile(step):
        step = jnp.clip(step, 0, n_prompt - 1)
        return step // tiles_per_batch, step % tiles_per_batch

    tab = pl.BlockSpec((TM, LANES), lambda i: (tile(i)[1], 0))

    def per_batch(a):
        return pl.BlockSpec((1,) + a.shape[1:], lambda i: (tile(i)[0],) + (0,) * (a.ndim - 1),
                            pipeline_mode=pl.Buffered(1))

    rows = pl.BlockSpec((R, D), lambda i: (0, 0))
    return pl.pallas_call(
        functools.partial(_attn_layer_kernel, layer, tiles_per_batch),
        grid=(n_prompt + 1,),
        in_specs=[
            pl.BlockSpec((1, TM, D), lambda i: (*tile(i), 0)), tab, tab, tab, _resident(bias.shape),
            per_batch(kdup), per_batch(vt),
            _resident(xs.shape)] + [_resident(t.shape) for t in tables_s] + [
            pl.BlockSpec(memory_space=pl.ANY), pl.BlockSpec(memory_space=pltpu.SMEM),
            pl.BlockSpec(memory_space=pl.ANY), g[1], b[1],
        ],
        out_specs=[pl.BlockSpec((1, TM, D), lambda i: (*tile(i - 1), 0)), rows, rows],
        out_shape=[jax.ShapeDtypeStruct((B, S, D), F32),
                   jax.ShapeDtypeStruct((R, D), F32), jax.ShapeDtypeStruct((R, D), F32)],
        scratch_shapes=[pltpu.VMEM((TM, D), BF16), pltpu.VMEM((TM, D), BF16), pltpu.VMEM((TM, D), BF16),
                        pltpu.VMEM((TM, D), F32), pltpu.VMEM((TM, D), F32),
                        pltpu.VMEM(w_in.shape[1:], BF16), pltpu.VMEM(w_out.shape[1:], BF16),
                        pltpu.VMEM((2, CAST_ROWS, w_in.shape[2]), F32), pltpu.SemaphoreType.DMA((2,))],
        compiler_params=_params(1),
        name="attn_layer",
    )(x, *tables, bias, kdup, vt, xs, *tables_s, w_in, sinks, w_out, g[0], b[0])


def _attn_sample_kernel(q_ref, kn_ref, vn_ref, ck_ref, cv_ref, sink_ref, o_ref, nk_ref, nv_ref):
    nb = q_ref.shape[0]
    head_of_lane = lax.broadcasted_iota(jnp.int32, (N_HEADS, D_MODEL), 1) // HEAD_DIM
    own_head = head_of_lane == lax.broadcasted_iota(jnp.int32, (N_HEADS, D_MODEL), 0)
    low_half = lax.broadcasted_iota(jnp.int32, (N_HEADS, LANES), 1) < HEAD_DIM
    last_row = lax.broadcasted_iota(jnp.int32, (WINDOW, KV_DIM), 0) == WINDOW - 1
    sink = sink_ref[...]
    heads_per_slab = LANES // HEAD_DIM
    slabs_per_group = GROUP // heads_per_slab

    def body(i, carry):
        newk = jnp.where(last_row, kn_ref[pl.ds(i, 1), :], pltpu.roll(ck_ref[i], WINDOW - 1, 0))
        newv = jnp.where(last_row, vn_ref[pl.ds(i, 1), :], pltpu.roll(cv_ref[i], WINDOW - 1, 0))
        nk_ref[i] = newk
        nv_ref[i] = newv
        qh = jnp.where(own_head, jnp.broadcast_to(q_ref[pl.ds(i, 1), :], (N_HEADS, D_MODEL)), 0.0)
        folded = []
        for kv in range(N_KV_HEADS):
            w = qh[:, kv * GROUP * HEAD_DIM:kv * GROUP * HEAD_DIM + LANES]
            for sl in range(1, slabs_per_group):
                c0 = kv * GROUP * HEAD_DIM + sl * LANES
                w = w + qh[:, c0:c0 + LANES]
            folded.append(w + pltpu.roll(w, HEAD_DIM, 1))
        qg = jnp.concatenate([jnp.where(low_half, folded[2 * j], folded[2 * j + 1])
                              for j in range(N_KV_HEADS // 2)], axis=1)
        s = lax.dot_general(qg.astype(BF16), newk.astype(BF16), (((1,), (1,)), ((), ())),
                            preferred_element_type=F32)
        m = jnp.maximum(jnp.max(s, axis=-1, keepdims=True), sink)
        p = jnp.exp(s - m)
        denom = jnp.sum(p, axis=-1, keepdims=True) + jnp.exp(sink - m)
        og = jnp.dot(p.astype(BF16), newv.astype(BF16), preferred_element_type=F32) / denom
        slabs = []
        for kv in range(N_KV_HEADS):
            xs = og[:, (kv // 2) * LANES:(kv // 2 + 1) * LANES]
            rolled = pltpu.roll(xs, HEAD_DIM, 1)
            both = jnp.where(low_half, xs, rolled) if kv % 2 == 0 else jnp.where(low_half, rolled, xs)
            slabs.extend([both] * slabs_per_group)
        full = jnp.concatenate(slabs, axis=1)
        o_ref[pl.ds(i, 1), :] = jnp.sum(jnp.where(own_head, full, 0.0), axis=0, keepdims=True)
        return carry

    lax.fori_loop(0, nb, body, 0, unroll=True)


def _attn_sample(q, kn, vn, ck, cv, sinks_col):
    R, D = q.shape
    nb = SAMPLE_ATTN_BATCH
    row2 = lambda w: pl.BlockSpec((nb, w), lambda i: (i, 0))
    cache = pl.BlockSpec((nb, WINDOW, KV_DIM), lambda i: (i, 0, 0))
    return pl.pallas_call(
        _attn_sample_kernel,
        grid=(R // nb,),
        in_specs=[row2(D), row2(KV_DIM), row2(KV_DIM), cache, cache, _resident(sinks_col.shape)],
        out_specs=[row2(D), cache, cache],
        out_shape=[jax.ShapeDtypeStruct((R, D), F32),
                   jax.ShapeDtypeStruct(ck.shape, F32), jax.ShapeDtypeStruct(cv.shape, F32)],
        compiler_params=_params(1),
        name="attn_sample",
    )(q, kn, vn, ck, cv, sinks_col)


def _gated_out_sample_kernel(x_ref, a_ref, gate_ref, wout_ref, g_ref, b_ref, o_ref, h_buf):
    rows = x_ref.shape[0]
    h_buf[...] = (a_ref[...] * _silu(gate_ref[...])).astype(BF16)

    def set_rows(rs, cs, v):
        o_ref[rs, cs] = v

    for n in range(N_CHUNKS):
        cols = slice(n * COL_CHUNK, (n + 1) * COL_CHUNK)
        y = jnp.dot(h_buf[...], wout_ref[:, cols].astype(BF16), preferred_element_type=F32)
        o_ref[:, cols] = ALPHA * x_ref[:, cols] + y
    _layer_norm(lambda rs, cs: o_ref[rs, cs], set_rows, g_ref, b_ref, rows)


def _gated_out_sample(x, a, gate, wout, g, b):
    R, D = x.shape
    return pl.pallas_call(
        _gated_out_sample_kernel,
        grid=(1,),
        in_specs=[_resident(t.shape) for t in (x, a, gate)] + [wout[1], g[1], b[1]],
        out_specs=pl.BlockSpec((R, D), lambda i: (0, 0)),
        out_shape=jax.ShapeDtypeStruct((R, D), F32),
        scratch_shapes=[pltpu.VMEM((R, D), BF16)],
        compiler_params=_params(1),
        name="gated_out_sample",
    )(x, a, gate, wout[0], g[0], b[0])


def _rope_tables(pos, scale=1.0):
    half = ROT_DIM // 2
    inv_freq = (ROPE_THETA ** (-np.arange(0, ROT_DIM, 2, dtype=np.float32) / ROT_DIM)).astype(np.float32)
    ang = pos.astype(np.float32)[:, None] * inv_freq[None, :]
    cos, sin = np.cos(ang), np.sin(ang)
    n = pos.shape[0]
    rest = np.zeros((n, HEAD_DIM - ROT_DIM), np.float32)
    zero = np.zeros((n, half), np.float32)
    cos_h = np.concatenate([cos, cos, rest + 1.0], axis=1)
    sa_h = np.concatenate([-sin, zero, rest], axis=1)
    sb_h = np.concatenate([zero, sin, rest], axis=1)
    rep = LANES // HEAD_DIM
    return tuple(jnp.asarray(np.tile(a, (1, rep)) * np.float32(scale)) for a in (cos_h, sa_h, sb_h))


def kernel(x_prompt, x_sample, state_pool, cache_k, cache_v, w_in_a, w_grp_a, scale_a, w_out_a,
           w_kv, w_in_b, sinks_b, w_out_b, ln_g, ln_b):
    B, S, D = x_prompt.shape
    R = x_sample.shape[0]
    xp = x_prompt
    xs = x_sample.reshape(R, D)
    pos_p, pos_s = np.arange(S), np.full((R,), PAST_LEN)
    tab_p, tab_s = _rope_tables(pos_p), _rope_tables(pos_s)
    qtab_p, qtab_s = _rope_tables(pos_p, SM_SCALE), _rope_tables(pos_s, SM_SCALE)
    scale_a, ln_g, ln_b = (p.reshape(p.shape[0], 1, D) for p in (scale_a, ln_g, ln_b))
    state_rows = state_pool.transpose(0, 2, 1, 3)
    pool_p, pool_s = [], []
    for i in range(N_A_LAYERS):
        params = [_layer(p, i) for p in (w_grp_a, scale_a, ln_g, ln_b)]
        xp, sp, xs, ss = _pool_layer(xp, xs, state_rows, i, w_in_a, w_out_a, params)
        pool_p.append(sp[:, HALO - POOL_STATE:])
        pool_s.append(ss)
    new_k_p, new_v_p, kdup_p, vt_p = _kv_prompt(xp, tab_p, w_kv)
    k_s, v_s = _kv_sample(xs, tab_s, w_kv)
    bias = _band_bias()
    ck = cache_k.reshape(R, WINDOW, KV_DIM)
    cv = cache_v.reshape(R, WINDOW, KV_DIM)
    for j in range(DEPTH - N_A_LAYERS):
        i = N_A_LAYERS + j
        wout, g, b = _layer(w_out_b, j), _layer(ln_g, i), _layer(ln_b, i)
        xp, q_s, gate_s = _attn_layer(xp, qtab_p, bias, kdup_p, vt_p, xs, qtab_s, j, w_in_b, sinks_b[j], w_out_b, g, b)
        a_s, nk, nv = _attn_sample(q_s, k_s, v_s, ck, cv, sinks_b[j][:, None])
        xs = _gated_out_sample(xs, a_s, gate_s, wout, g, b)
    kv4 = (N_KV_HEADS, HEAD_DIM)
    return (xp, xs.reshape(R, 1, D), jnp.stack(pool_p, axis=0), jnp.stack(pool_s, axis=0).transpose(0, 2, 1, 3),
            new_k_p.reshape(B, WINDOW, *kv4), new_v_p.reshape(B, WINDOW, *kv4),
            nk.reshape(R, WINDOW, *kv4), nv.reshape(R, WINDOW, *kv4))
```

```python
import functools

import jax
import jax.numpy as jnp
import numpy as np
from jax import lax
from jax.experimental import pallas as pl
from jax.experimental.pallas import tpu as pltpu

F32 = jnp.float32
BF16 = jnp.bfloat16

D_MODEL = 2048
DEPTH = 4
PAST_LEN = 16384
N_A_LAYERS = DEPTH // 2
POOL_WINDOWS = (2, 4, 8, 16)
POOL_GROUP = D_MODEL // len(POOL_WINDOWS)
POOL_STATE = max(POOL_WINDOWS) - 1
HEAD_DIM = 64
N_HEADS = D_MODEL // HEAD_DIM
N_KV_HEADS = N_HEADS // 8
GROUP = N_HEADS // N_KV_HEADS
KV_DIM = N_KV_HEADS * HEAD_DIM
WINDOW = 128
ROT_DIM = HEAD_DIM // 4
ROPE_THETA = 500000.0
ALPHA = (2 * DEPTH) ** 0.25
LN_EPS = 1e-5
NEG = -1e30
SM_SCALE = HEAD_DIM ** -0.5

LANES = 128
SUBLANES = 8
HALO = 16
COL_CHUNK = 512
N_CHUNKS = D_MODEL // COL_CHUNK
TM = 256
TK = 1024
LN_ROWS = 16
SAMPLE_ATTN_BATCH = 8
SUM_ROWS = 16
CAST_ROWS = 64
CAST_SLOTS = 4
VMEM_LIMIT_BYTES = 56 * 1024 * 1024


def _params(n_axes):
    return pltpu.CompilerParams(dimension_semantics=("arbitrary",) * n_axes,
                                vmem_limit_bytes=VMEM_LIMIT_BYTES)


def _resident(shape):
    zeros = (0,) * len(shape)
    return pl.BlockSpec(shape, lambda *_: zeros, pipeline_mode=pl.Buffered(1))


def _layer(stacked, i):
    zeros = (0,) * (stacked.ndim - 1)
    return stacked, pl.BlockSpec((None,) + stacked.shape[1:], lambda *_: (i,) + zeros,
                                 pipeline_mode=pl.Buffered(1))


def _silu(g):
    return g / (1.0 + jnp.exp(-g))


def _rope(x, cos, sa, sb):
    return x * cos + pltpu.roll(x, LANES - ROT_DIM // 2, 1) * sa + pltpu.roll(x, ROT_DIM // 2, 1) * sb


def _load_as_bf16(w_hbm, dst_ref, stage, sems):
    n_cols = dst_ref.shape[1]
    slots, chunk = stage.shape[0], stage.shape[1]
    n_chunks = dst_ref.shape[0] // chunk
    ahead = slots - 1

    def copy(c):
        return pltpu.make_async_copy(w_hbm.at[pl.ds(c * chunk, chunk)],
                                     stage.at[c % slots, :, pl.ds(0, n_cols)], sems.at[c % slots])

    for c in range(min(ahead, n_chunks)):
        copy(c).start(priority=c % 2)
    for c in range(n_chunks):
        if c + ahead < n_chunks:
            copy(c + ahead).start(priority=(c + ahead) % 2)
        copy(c).wait()
        dst_ref[c * chunk:(c + 1) * chunk, :] = stage[c % slots, :, 0:n_cols].astype(BF16)


def _outproj_residual(h_ref, x_rows, wout_ref, r_rows_set, rows):
    for n in range(N_CHUNKS):
        cols = slice(n * COL_CHUNK, (n + 1) * COL_CHUNK)
        y = jnp.dot(h_ref[0:rows, :], wout_ref[:, cols], preferred_element_type=F32)
        r_rows_set(slice(0, rows), cols, ALPHA * x_rows(slice(0, rows), cols) + y)


def _zero_after(v):
    bits = pltpu.bitcast(v, jnp.int32)
    half = jnp.full(bits.shape, 16, jnp.int32)
    return lax.shift_right_logical(lax.shift_right_logical(bits, half), half).astype(F32)


def _layer_norm(r_rows_get, o_rows_set, g_ref, b_ref, rows):
    step = min(LN_ROWS, rows)
    anchors = []
    for r0 in range(0, rows, step):
        rs = slice(r0, r0 + step)
        r = r_rows_get(rs, slice(None))
        mu = jnp.mean(r, axis=-1, keepdims=True)
        c = r - mu
        var = jnp.mean(c * c, axis=-1, keepdims=True)
        out = c * lax.rsqrt(var + LN_EPS) * g_ref[...] + b_ref[...]
        o_rows_set(rs, slice(None), out)
        folded = sum(out[i:i + SUBLANES, j:j + LANES]
                     for i in range(0, step, SUBLANES) for j in range(0, out.shape[1], LANES))
        anchors.append(_zero_after(folded))
    return anchors


def _outproj_ln(h_ref, x_rows, wout_ref, g_ref, b_ref, o_rows_set, o_rows_get, rows):
    _outproj_residual(h_ref, x_rows, wout_ref, o_rows_set, rows)
    _layer_norm(o_rows_get, o_rows_set, g_ref, b_ref, rows)


def _pool_layer_kernel(layer, tiles_per_batch,
                       x_ref, xs_ref, st_hbm, win_hbm, wgrp_ref, scale_ref, wout_hbm, g_ref, b_ref,
                       o_ref, state_ref, ys_ref, nst_hbm,
                       xb_buf, u_buf, h_buf, r_buf, st_buf, us_buf, win_ref, wout_ref, stage, sems, wsems):
    step = pl.program_id(0)
    n_prompt = pl.num_programs(0) - 1
    rows_s = xs_ref.shape[0]

    def mix(xb, d, g):
        cols = slice(g * POOL_GROUP, (g + 1) * POOL_GROUP)
        gate = jnp.dot(xb, win_ref[:, D_MODEL + g * POOL_GROUP:D_MODEL + (g + 1) * POOL_GROUP],
                       preferred_element_type=F32)
        d = jnp.dot(d.astype(BF16), wgrp_ref[g].astype(BF16), preferred_element_type=F32) * scale_ref[:, cols]
        return (d * _silu(gate)).astype(BF16)

    def set_out(rs, cs, v):
        o_ref[0, rs, cs] = v

    def set_r(rs, cs, v):
        r_buf[rs, cs] = v

    def norm_previous_tile():
        return _layer_norm(lambda rs, cs: r_buf[rs, cs], set_out, g_ref, b_ref, TM)

    @pl.when(step == 0)
    def _():
        r_buf[...] = jnp.zeros(r_buf.shape, F32)
        _load_as_bf16(win_hbm.at[layer], win_ref, stage, wsems)
        _load_as_bf16(wout_hbm.at[layer], wout_ref, stage, wsems)

    @pl.when(step < n_prompt)
    def _():
        t = step % tiles_per_batch

        @pl.when(t == 0)
        def _():
            u_buf[0:HALO, :] = jnp.zeros((HALO, D_MODEL), F32)

        xb_buf[...] = x_ref[0].astype(BF16)
        row = lax.broadcasted_iota(jnp.int32, (TM, 1), 0) + t * TM

        def project_u(g):
            cols = slice(g * POOL_GROUP, (g + 1) * POOL_GROUP)
            u_buf[HALO:, cols] = jnp.dot(xb_buf[...], win_ref[:, cols], preferred_element_type=F32)

        project_u(0)
        anchors = norm_previous_tile()
        per_group = len(anchors) // len(POOL_WINDOWS)
        for g, w in enumerate(POOL_WINDOWS):
            cols = slice(g * POOL_GROUP, (g + 1) * POOL_GROUP)
            if g + 1 < len(POOL_WINDOWS):
                project_u(g + 1)
            ext = u_buf[:, cols]
            s = ext
            shift = 1
            while shift < w:
                s = s + pltpu.roll(s, shift, 0)
                shift *= 2
            inv_cnt = 1.0 / jnp.minimum(w, row + 1).astype(F32)
            anchor = sum(anchors[g * per_group:(g + 1) * per_group])
            anchor = jnp.tile(anchor, (TM // anchor.shape[0], POOL_GROUP // anchor.shape[1]))
            h_buf[:, cols] = mix(xb_buf[...], s[HALO:, :] * inv_cnt - ext[HALO:, :] + anchor, g)

        state_ref[0] = u_buf[TM:TM + HALO, :]
        u_buf[0:HALO, :] = u_buf[TM:TM + HALO, :]
        _outproj_residual(h_buf, lambda rs, cs: x_ref[0, rs, cs], wout_ref, set_r, TM)

    @pl.when(step == n_prompt)
    def _():
        load = pltpu.make_async_copy(st_hbm.at[layer], st_buf, sems.at[0])
        shift_old = pltpu.make_async_copy(st_buf.at[pl.ds(1, POOL_STATE - 1)],
                                          nst_hbm.at[pl.ds(0, POOL_STATE - 1)], sems.at[1])
        append_new = pltpu.make_async_copy(us_buf, nst_hbm.at[POOL_STATE - 1], sems.at[2])
        load.start()
        norm_previous_tile()
        xb = xs_ref[...].astype(BF16)
        load.wait()
        shift_old.start()
        for g, w in enumerate(POOL_WINDOWS):
            cols = slice(g * POOL_GROUP, (g + 1) * POOL_GROUP)
            u = jnp.dot(xb, win_ref[:, cols], preferred_element_type=F32)
            us_buf[:, cols] = u
            acc = u
            for j in range(1, w):
                acc = acc + st_buf[POOL_STATE - j, :, cols]
            h_buf[0:rows_s, cols] = mix(xb, acc * (1.0 / min(w, PAST_LEN + 1)) - u, g)
        append_new.start()

        def set_ys(rs, cs, v):
            ys_ref[rs, cs] = v

        _outproj_ln(h_buf, lambda rs, cs: xs_ref[rs, cs], wout_ref, g_ref, b_ref,
                    set_ys, lambda rs, cs: ys_ref[rs, cs], rows_s)
        shift_old.wait()
        append_new.wait()


def _pool_layer(x, xs, state_rows, layer, w_in, w_out, params):
    B, S, D = x.shape
    R = xs.shape[0]
    tiles_per_batch = S // TM
    n_prompt = B * tiles_per_batch

    def tile(step):
        step = jnp.clip(step, 0, n_prompt - 1)
        return step // tiles_per_batch, step % tiles_per_batch

    hbm = pl.BlockSpec(memory_space=pl.ANY)
    (wgrp, wgrp_spec), (scale, scale_spec), (g, g_spec), (b, b_spec) = params
    return pl.pallas_call(
        functools.partial(_pool_layer_kernel, layer, tiles_per_batch),
        grid=(n_prompt + 1,),
        in_specs=[pl.BlockSpec((1, TM, D), lambda i: (*tile(i), 0)), _resident(xs.shape), hbm,
                  hbm, wgrp_spec, scale_spec, hbm, g_spec, b_spec],
        out_specs=[
            pl.BlockSpec((1, TM, D), lambda i: (*tile(i - 1), 0)),
            pl.BlockSpec((1, HALO, D), lambda i: (tile(i)[0], 0, 0)),
            pl.BlockSpec((R, D), lambda i: (0, 0)),
            pl.BlockSpec(memory_space=pl.ANY),
        ],
        out_shape=[jax.ShapeDtypeStruct((B, S, D), F32), jax.ShapeDtypeStruct((B, HALO, D), F32),
                   jax.ShapeDtypeStruct((R, D), F32), jax.ShapeDtypeStruct((POOL_STATE, R, D), F32)],
        scratch_shapes=[pltpu.VMEM((TM, D), BF16), pltpu.VMEM((HALO + TM, D), F32), pltpu.VMEM((TM, D), BF16),
                        pltpu.VMEM((TM, D), F32), pltpu.VMEM((POOL_STATE, R, D), F32), pltpu.VMEM((R, D), F32),
                        pltpu.VMEM(w_in.shape[1:], BF16), pltpu.VMEM(w_out.shape[1:], BF16),
                        pltpu.VMEM((CAST_SLOTS, CAST_ROWS, w_in.shape[2]), F32),
                        pltpu.SemaphoreType.DMA((3,)), pltpu.SemaphoreType.DMA((CAST_SLOTS,))],
        compiler_params=_params(1),
        name="pool_layer",
    )(x, xs, state_rows, w_in, wgrp, scale, w_out, g, b)


def _project_kv(xb, cos, sa, sb, wkv_ref):
    kv = jnp.dot(xb, wkv_ref[...].astype(BF16), preferred_element_type=F32)
    k_slabs = [_rope(kv[:, j * LANES:(j + 1) * LANES], cos, sa, sb) for j in range(KV_DIM // LANES)]
    return k_slabs, kv[:, KV_DIM:]


def _kv_prompt_kernel(x_ref, cos_ref, sa_ref, sb_ref, wkv_ref, knew_ref, vnew_ref, kdup_ref, vt_ref):
    k_slabs, v = _project_kv(x_ref[0].astype(BF16), cos_ref[...], sa_ref[...], sb_ref[...], wkv_ref)
    low = lax.broadcasted_iota(jnp.int32, (TK, LANES), 1) < HEAD_DIM
    for j, k in enumerate(k_slabs):
        swapped = pltpu.roll(k, HEAD_DIM, 1)
        kdup_ref[0, :, (2 * j) * LANES:(2 * j + 1) * LANES] = jnp.where(low, k, swapped).astype(BF16)
        kdup_ref[0, :, (2 * j + 1) * LANES:(2 * j + 2) * LANES] = jnp.where(low, swapped, k).astype(BF16)
    for i in range(TK // WINDOW):
        vt_ref[0, i] = v[i * WINDOW:(i + 1) * WINDOW, :].T.astype(BF16)

    @pl.when(pl.program_id(1) == pl.num_programs(1) - 1)
    def _():
        for j, k in enumerate(k_slabs):
            knew_ref[0, :, j * LANES:(j + 1) * LANES] = k[TK - WINDOW:, :]
        vnew_ref[0] = v[TK - WINDOW:, :]


def _kv_prompt(x, tables, wkv):
    B, S, D = x.shape
    tab = pl.BlockSpec((TK, LANES), lambda bi, t: (t, 0))
    last = pl.BlockSpec((1, WINDOW, KV_DIM), lambda bi, t: (bi, 0, 0))
    return pl.pallas_call(
        _kv_prompt_kernel,
        grid=(B, S // TK),
        in_specs=[pl.BlockSpec((1, TK, D), lambda bi, t: (bi, t, 0)), tab, tab, tab, _resident(wkv.shape)],
        out_specs=[last, last,
                   pl.BlockSpec((1, TK, N_KV_HEADS * LANES), lambda bi, t: (bi, t, 0)),
                   pl.BlockSpec((1, TK // WINDOW, KV_DIM, WINDOW), lambda bi, t: (bi, t, 0, 0))],
        out_shape=[jax.ShapeDtypeStruct((B, WINDOW, KV_DIM), F32), jax.ShapeDtypeStruct((B, WINDOW, KV_DIM), F32),
                   jax.ShapeDtypeStruct((B, S, N_KV_HEADS * LANES), BF16),
                   jax.ShapeDtypeStruct((B, S // WINDOW, KV_DIM, WINDOW), BF16)],
        compiler_params=_params(2),
        name="kv_prompt",
    )(x, *tables, wkv)


def _kv_sample_kernel(x_ref, cos_ref, sa_ref, sb_ref, wkv_ref, k_ref, v_ref):
    k_slabs, v = _project_kv(x_ref[...].astype(BF16), cos_ref[...], sa_ref[...], sb_ref[...], wkv_ref)
    for j, k in enumerate(k_slabs):
        k_ref[:, j * LANES:(j + 1) * LANES] = k
    v_ref[...] = v


def _kv_sample(x, tables, wkv):
    R, D = x.shape
    out = pl.BlockSpec((R, KV_DIM), lambda i: (0, 0))
    return pl.pallas_call(
        _kv_sample_kernel,
        grid=(1,),
        in_specs=[_resident(x.shape)] + [_resident(t.shape) for t in tables] + [_resident(wkv.shape)],
        out_specs=[out, out],
        out_shape=[jax.ShapeDtypeStruct((R, KV_DIM), F32), jax.ShapeDtypeStruct((R, KV_DIM), F32)],
        compiler_params=_params(1),
        name="kv_sample",
    )(x, *tables, wkv)


def _attn_layer_kernel(layer, tiles_per_batch,
                       x_ref, cos_ref, sa_ref, sb_ref, bias_ref, kdup_ref, vt_ref, xs_ref, cos_s_ref, sa_s_ref, sb_s_ref,
                       win_hbm, sink_ref, wout_hbm, g_ref, b_ref,
                       o_ref, qs_ref, gs_ref, xb_buf, q_buf, h_buf, g_buf, r_buf, win_ref, wout_ref, stage, wsems):
    step = pl.program_id(0)
    n_prompt = pl.num_programs(0) - 1
    group_cols = GROUP * HEAD_DIM

    def set_out(rs, cs, v):
        o_ref[0, rs, cs] = v

    def set_r(rs, cs, v):
        r_buf[rs, cs] = v

    def norm_previous_tile():
        return _layer_norm(lambda rs, cs: r_buf[rs, cs], set_out, g_ref, b_ref, TM)

    @pl.when(step == 0)
    def _():
        r_buf[...] = jnp.zeros(r_buf.shape, F32)
        _load_as_bf16(win_hbm.at[layer], win_ref, stage, wsems)
        _load_as_bf16(wout_hbm.at[layer], wout_ref, stage, wsems)

    @pl.when(step < n_prompt)
    def _():
        t = step % tiles_per_batch
        xb_buf[...] = x_ref[0].astype(BF16)

        def project_q(kv, anchor):
            q = jnp.dot(xb_buf[...], win_ref[:, kv * group_cols:(kv + 1) * group_cols], preferred_element_type=F32)
            if anchor is not None:
                q = q + anchor
            for j in range(group_cols // LANES):
                qj = _rope(q[:, j * LANES:(j + 1) * LANES], cos_ref[...], sa_ref[...], sb_ref[...])
                c0 = kv * group_cols + j * LANES
                q_buf[:, c0:c0 + LANES] = qj.astype(BF16)

        def project_gate(kv, anchor):
            cols = slice(kv * group_cols, (kv + 1) * group_cols)
            gate = jnp.dot(xb_buf[...], win_ref[:, D_MODEL + kv * group_cols:D_MODEL + (kv + 1) * group_cols],
                           preferred_element_type=F32)
            g_buf[:, cols] = _silu(gate) if anchor is None else _silu(gate) + anchor

        low_half = lax.broadcasted_iota(jnp.int32, (WINDOW, LANES), 1) < HEAD_DIM
        ones_rows = jnp.ones((SUM_ROWS, 2 * WINDOW), BF16)

        def block_ids(qb):
            blk = t * (TM // WINDOW) + qb
            return blk, jnp.maximum(blk - 1, 0)

        def group_heads(kv):
            return [(kv * (GROUP // 2) + pair, par) for pair in range(GROUP // 2) for par in range(2)]

        def scores(qb, kv):
            blk, prev_blk = block_ids(qb)
            prev = pl.multiple_of(prev_blk * WINDOW, WINDOW)
            cur = pl.multiple_of(blk * WINDOW, WINDOW)
            rows = slice(qb * WINDOW, (qb + 1) * WINDOW)
            ks = slice(kv * LANES, (kv + 1) * LANES)
            k2 = jnp.concatenate([kdup_ref[0, pl.ds(prev, WINDOW), ks], kdup_ref[0, pl.ds(cur, WINDOW), ks]], axis=0)
            q_all = []
            for slab, par in group_heads(kv):
                q_slab = q_buf[rows, slab * LANES:(slab + 1) * LANES]
                q_all.append(jnp.where(low_half == (par == 0), q_slab, jnp.zeros_like(q_slab)))
            return lax.dot_general(k2, jnp.concatenate(q_all, axis=0), (((1,), (1,)), ((), ())),
                                   preferred_element_type=F32)

        def finish(qb, kv, s_t):
            blk, prev_blk = block_ids(qb)
            bias = bias_ref[jnp.minimum(blk, 1)]
            rows = slice(qb * WINDOW, (qb + 1) * WINDOW)
            vs = slice(kv * HEAD_DIM, (kv + 1) * HEAD_DIM)
            v_aug = jnp.concatenate([vt_ref[0, prev_blk, vs, :], vt_ref[0, blk, vs, :]], axis=1)
            v_aug = jnp.concatenate([v_aug, ones_rows], axis=0)
            for pair in range(GROUP // 2):
                slab = kv * (GROUP // 2) + pair
                p_t, sink_terms = [], []
                for par in range(2):
                    i = 2 * pair + par
                    s = s_t[:, i * WINDOW:(i + 1) * WINDOW] + bias
                    sink = sink_ref[2 * slab + par]
                    m = jnp.maximum(jnp.max(s, axis=0, keepdims=True), sink)
                    p_t.append(jnp.exp(s - m).astype(BF16))
                    sink_terms.append(jnp.exp(sink - m))
                o_t = jnp.dot(v_aug, jnp.concatenate(p_t, axis=1), preferred_element_type=F32)
                both = []
                for par in range(2):
                    cs = slice(par * WINDOW, (par + 1) * WINDOW)
                    inv = 1.0 / (o_t[HEAD_DIM:HEAD_DIM + 1, cs] + sink_terms[par])
                    both.append(o_t[:HEAD_DIM, cs] * inv)
                attn = jnp.concatenate(both, axis=0).T
                cs = slice(slab * LANES, (slab + 1) * LANES)
                h_buf[rows, cs] = (attn * g_buf[rows, cs]).astype(BF16)

        project_q(0, None)
        anchors = norm_previous_tile()
        early = 2 * (N_KV_HEADS // 2)
        per_proj = len(anchors) // early

        def next_anchor():
            if not anchors:
                return None
            anchor = sum(anchors.pop(0) for _ in range(per_proj))
            return jnp.tile(anchor, (TM // anchor.shape[0], group_cols // anchor.shape[1]))

        for kv in range(N_KV_HEADS):
            s_t = [scores(qb, kv) for qb in range(TM // WINDOW)]
            if kv + 1 < N_KV_HEADS:
                project_q(kv + 1, next_anchor())
            project_gate(kv, next_anchor())
            for qb in range(TM // WINDOW):
                finish(qb, kv, s_t[qb])
        _outproj_residual(h_buf, lambda rs, cs: x_ref[0, rs, cs], wout_ref, set_r, TM)

    @pl.when(step == n_prompt)
    def _():
        norm_previous_tile()
        xb = xs_ref[...].astype(BF16)
        for n in range(N_CHUNKS):
            q = jnp.dot(xb, win_ref[:, n * COL_CHUNK:(n + 1) * COL_CHUNK], preferred_element_type=F32)
            for j in range(COL_CHUNK // LANES):
                c0 = n * COL_CHUNK + j * LANES
                qs_ref[:, c0:c0 + LANES] = _rope(q[:, j * LANES:(j + 1) * LANES],
                                                 cos_s_ref[...], sa_s_ref[...], sb_s_ref[...])
            gs_ref[:, n * COL_CHUNK:(n + 1) * COL_CHUNK] = jnp.dot(
                xb, win_ref[:, D_MODEL + n * COL_CHUNK:D_MODEL + (n + 1) * COL_CHUNK],
                preferred_element_type=F32)


def _band_bias():
    key = np.arange(2 * WINDOW)[:, None]
    qry = np.arange(WINDOW)[None, :]
    band = (key > qry) & (key <= qry + WINDOW)
    return jnp.asarray(np.where(np.stack([band & (key >= WINDOW), band]), 0.0, NEG).astype(np.float32))


def _attn_layer(x, tables, bias, kdup, vt, xs, tables_s, layer, w_in, sinks, w_out, g, b):
    B, S, D = x.shape
    R = xs.shape[0]
    tiles_per_batch = S // TM
    n_prompt = B * tiles_per_batch

    def tile(step):
        step = jnp.clip(step, 0, n_prompt - 1)
        return step // tiles_per_batch, step % tiles_per_batch

    tab = pl.BlockSpec((TM, LANES), lambda i: (tile(i)[1], 0))

    def per_batch(a):
        return pl.BlockSpec((1,) + a.shape[1:], lambda i: (tile(i)[0],) + (0,) * (a.ndim - 1))

    rows = pl.BlockSpec((R, D), lambda i: (0, 0))
    return pl.pallas_call(
        functools.partial(_attn_layer_kernel, layer, tiles_per_batch),
        grid=(n_prompt + 1,),
        in_specs=[
            pl.BlockSpec((1, TM, D), lambda i: (*tile(i), 0)), tab, tab, tab, _resident(bias.shape),
            per_batch(kdup), per_batch(vt),
            _resident(xs.shape)] + [_resident(t.shape) for t in tables_s] + [
            pl.BlockSpec(memory_space=pl.ANY), pl.BlockSpec(memory_space=pltpu.SMEM),
            pl.BlockSpec(memory_space=pl.ANY), g[1], b[1],
        ],
        out_specs=[pl.BlockSpec((1, TM, D), lambda i: (*tile(i - 1), 0)), rows, rows],
        out_shape=[jax.ShapeDtypeStruct((B, S, D), F32),
                   jax.ShapeDtypeStruct((R, D), F32), jax.ShapeDtypeStruct((R, D), F32)],
        scratch_shapes=[pltpu.VMEM((TM, D), BF16), pltpu.VMEM((TM, D), BF16), pltpu.VMEM((TM, D), BF16),
                        pltpu.VMEM((TM, D), F32), pltpu.VMEM((TM, D), F32),
                        pltpu.VMEM(w_in.shape[1:], BF16), pltpu.VMEM(w_out.shape[1:], BF16),
                        pltpu.VMEM((CAST_SLOTS, CAST_ROWS, w_in.shape[2]), F32),
                        pltpu.SemaphoreType.DMA((CAST_SLOTS,))],
        compiler_params=_params(1),
        name="attn_layer",
    )(x, *tables, bias, kdup, vt, xs, *tables_s, w_in, sinks, w_out, g[0], b[0])


def _attn_sample_kernel(q_ref, kn_ref, vn_ref, ck_ref, cv_ref, sink_ref, o_ref, nk_ref, nv_ref):
    nb = q_ref.shape[0]
    head_of_lane = lax.broadcasted_iota(jnp.int32, (N_HEADS, D_MODEL), 1) // HEAD_DIM
    own_head = head_of_lane == lax.broadcasted_iota(jnp.int32, (N_HEADS, D_MODEL), 0)
    low_half = lax.broadcasted_iota(jnp.int32, (N_HEADS, LANES), 1) < HEAD_DIM
    last_row = lax.broadcasted_iota(jnp.int32, (WINDOW, KV_DIM), 0) == WINDOW - 1
    sink = sink_ref[...]
    heads_per_slab = LANES // HEAD_DIM
    slabs_per_group = GROUP // heads_per_slab

    def body(i, carry):
        newk = jnp.where(last_row, kn_ref[pl.ds(i, 1), :], pltpu.roll(ck_ref[i], WINDOW - 1, 0))
        newv = jnp.where(last_row, vn_ref[pl.ds(i, 1), :], pltpu.roll(cv_ref[i], WINDOW - 1, 0))
        nk_ref[i] = newk
        nv_ref[i] = newv
        qh = jnp.where(own_head, jnp.broadcast_to(q_ref[pl.ds(i, 1), :], (N_HEADS, D_MODEL)), 0.0)
        folded = []
        for kv in range(N_KV_HEADS):
            w = qh[:, kv * GROUP * HEAD_DIM:kv * GROUP * HEAD_DIM + LANES]
            for sl in range(1, slabs_per_group):
                c0 = kv * GROUP * HEAD_DIM + sl * LANES
                w = w + qh[:, c0:c0 + LANES]
            folded.append(w + pltpu.roll(w, HEAD_DIM, 1))
        qg = jnp.concatenate([jnp.where(low_half, folded[2 * j], folded[2 * j + 1])
                              for j in range(N_KV_HEADS // 2)], axis=1)
        s = lax.dot_general(qg.astype(BF16), newk.astype(BF16), (((1,), (1,)), ((), ())),
                            preferred_element_type=F32)
        m = jnp.maximum(jnp.max(s, axis=-1, keepdims=True), sink)
        p = jnp.exp(s - m)
        denom = jnp.sum(p, axis=-1, keepdims=True) + jnp.exp(sink - m)
        og = jnp.dot(p.astype(BF16), newv.astype(BF16), preferred_element_type=F32) / denom
        slabs = []
        for kv in range(N_KV_HEADS):
            xs = og[:, (kv // 2) * LANES:(kv // 2 + 1) * LANES]
            rolled = pltpu.roll(xs, HEAD_DIM, 1)
            both = jnp.where(low_half, xs, rolled) if kv % 2 == 0 else jnp.where(low_half, rolled, xs)
            slabs.extend([both] * slabs_per_group)
        full = jnp.concatenate(slabs, axis=1)
        o_ref[pl.ds(i, 1), :] = jnp.sum(jnp.where(own_head, full, 0.0), axis=0, keepdims=True)
        return carry

    lax.fori_loop(0, nb, body, 0, unroll=True)


def _attn_sample(q, kn, vn, ck, cv, sinks_col):
    R, D = q.shape
    nb = SAMPLE_ATTN_BATCH
    row2 = lambda w: pl.BlockSpec((nb, w), lambda i: (i, 0))
    cache = pl.BlockSpec((nb, WINDOW, KV_DIM), lambda i: (i, 0, 0))
    return pl.pallas_call(
        _attn_sample_kernel,
        grid=(R // nb,),
        in_specs=[row2(D), row2(KV_DIM), row2(KV_DIM), cache, cache, _resident(sinks_col.shape)],
        out_specs=[row2(D), cache, cache],
        out_shape=[jax.ShapeDtypeStruct((R, D), F32),
                   jax.ShapeDtypeStruct(ck.shape, F32), jax.ShapeDtypeStruct(cv.shape, F32)],
        compiler_params=_params(1),
        name="attn_sample",
    )(q, kn, vn, ck, cv, sinks_col)


def _gated_out_sample_kernel(x_ref, a_ref, gate_ref, wout_ref, g_ref, b_ref, o_ref, h_buf):
    rows = x_ref.shape[0]
    h_buf[...] = (a_ref[...] * _silu(gate_ref[...])).astype(BF16)

    def set_rows(rs, cs, v):
        o_ref[rs, cs] = v

    for n in range(N_CHUNKS):
        cols = slice(n * COL_CHUNK, (n + 1) * COL_CHUNK)
        y = jnp.dot(h_buf[...], wout_ref[:, cols].astype(BF16), preferred_element_type=F32)
        o_ref[:, cols] = ALPHA * x_ref[:, cols] + y
    _layer_norm(lambda rs, cs: o_ref[rs, cs], set_rows, g_ref, b_ref, rows)


def _gated_out_sample(x, a, gate, wout, g, b):
    R, D = x.shape
    return pl.pallas_call(
        _gated_out_sample_kernel,
        grid=(1,),
        in_specs=[_resident(t.shape) for t in (x, a, gate)] + [wout[1], g[1], b[1]],
        out_specs=pl.BlockSpec((R, D), lambda i: (0, 0)),
        out_shape=jax.ShapeDtypeStruct((R, D), F32),
        scratch_shapes=[pltpu.VMEM((R, D), BF16)],
        compiler_params=_params(1),
        name="gated_out_sample",
    )(x, a, gate, wout[0], g[0], b[0])


def _rope_tables(pos, scale=1.0):
    half = ROT_DIM // 2
    inv_freq = ROPE_THETA ** (-jnp.arange(0, ROT_DIM, 2, dtype=F32) / ROT_DIM)
    ang = pos.astype(F32)[:, None] * inv_freq[None, :]
    cos, sin = jnp.cos(ang), jnp.sin(ang)
    n = pos.shape[0]
    rest = jnp.zeros((n, HEAD_DIM - ROT_DIM), F32)
    zero = jnp.zeros((n, half), F32)
    cos_h = jnp.concatenate([cos, cos, rest + 1.0], axis=1)
    sa_h = jnp.concatenate([-sin, zero, rest], axis=1)
    sb_h = jnp.concatenate([zero, sin, rest], axis=1)
    rep = LANES // HEAD_DIM
    return tuple(jnp.tile(a, (1, rep)) * scale for a in (cos_h, sa_h, sb_h))


def kernel(x_prompt, x_sample, state_pool, cache_k, cache_v, w_in_a, w_grp_a, scale_a, w_out_a,
           w_kv, w_in_b, sinks_b, w_out_b, ln_g, ln_b):
    B, S, D = x_prompt.shape
    R = x_sample.shape[0]
    xp = x_prompt
    xs = x_sample.reshape(R, D)
    pos_p, pos_s = jnp.arange(S, dtype=jnp.int32), jnp.full((R,), PAST_LEN, jnp.int32)
    tab_p, tab_s = _rope_tables(pos_p), _rope_tables(pos_s)
    qtab_p, qtab_s = _rope_tables(pos_p, SM_SCALE), _rope_tables(pos_s, SM_SCALE)
    scale_a, ln_g, ln_b = (p.reshape(p.shape[0], 1, D) for p in (scale_a, ln_g, ln_b))
    state_rows = state_pool.transpose(0, 2, 1, 3)
    pool_p, pool_s = [], []
    for i in range(N_A_LAYERS):
        params = [_layer(p, i) for p in (w_grp_a, scale_a, ln_g, ln_b)]
        xp, sp, xs, ss = _pool_layer(xp, xs, state_rows, i, w_in_a, w_out_a, params)
        pool_p.append(sp[:, HALO - POOL_STATE:])
        pool_s.append(ss)
    new_k_p, new_v_p, kdup_p, vt_p = _kv_prompt(xp, tab_p, w_kv)
    k_s, v_s = _kv_sample(xs, tab_s, w_kv)
    bias = _band_bias()
    ck = cache_k.reshape(R, WINDOW, KV_DIM)
    cv = cache_v.reshape(R, WINDOW, KV_DIM)
    for j in range(DEPTH - N_A_LAYERS):
        i = N_A_LAYERS + j
        wout, g, b = _layer(w_out_b, j), _layer(ln_g, i), _layer(ln_b, i)
        xp, q_s, gate_s = _attn_layer(xp, qtab_p, bias, kdup_p, vt_p, xs, qtab_s, j, w_in_b, sinks_b[j], w_out_b, g, b)
        a_s, nk, nv = _attn_sample(q_s, k_s, v_s, ck, cv, sinks_b[j][:, None])
        xs = _gated_out_sample(xs, a_s, gate_s, wout, g, b)
    kv4 = (N_KV_HEADS, HEAD_DIM)
    return (xp, xs.reshape(R, 1, D), jnp.stack(pool_p, axis=0), jnp.stack(pool_s, axis=0).transpose(0, 2, 1, 3),
            new_k_p.reshape(B, WINDOW, *kv4), new_v_p.reshape(B, WINDOW, *kv4),
            nk.reshape(R, WINDOW, *kv4), nv.reshape(R, WINDOW, *kv4))
```
